```python
import math
import jax
import jax.numpy as jnp
from jax import lax
import numpy as np

D_MODEL = 4096
BATCH = 4
SEQ = 2048
DEPTH = 1
DEC_BATCH = 128
DEC_SEQ = 1
PAST_LEN = 16384
PAGE_SIZE = 128

BRANCH_WIDTH = D_MODEL // 2
GLA_HEADS = 4
GLA_DK = D_MODEL // 4
GLA_DV = BRANCH_WIDTH
GLA_HEAD_DK = GLA_DK // GLA_HEADS
GLA_HEAD_DV = GLA_DV // GLA_HEADS
GLA_GATE_RANK = 16
GLA_GATE_TAU = 16.0
GLA_CHUNK = 64
S5_WIDTH = BRANCH_WIDTH
S5_GROUP = 16
S5_GROUPS = S5_WIDTH // S5_GROUP
S5_STATE = 64
S5_DT_MIN = 0.001
S5_DT_MAX = 0.1
N_BRANCH = 2
MOE_GROUPS = 8
MOE_EXPERTS_PER_GROUP = 8
N_EXPERTS = MOE_GROUPS * MOE_EXPERTS_PER_GROUP
MOE_TOP_K = 2
MOE_HIDDEN = D_MODEL // 8
MOE_BLOCK = 128
LN_EPS = 1e-5
DEEPNORM_ALPHA = (2 * DEPTH) ** 0.25
DEEPNORM_BETA = (8 * DEPTH) ** -0.25
Q_COLS = GLA_DK
K_COLS = GLA_DK
V_COLS = GLA_DV
R_COLS = GLA_DV
A_COLS = GLA_GATE_RANK
U_COLS = S5_WIDTH
G_COLS = N_BRANCH * D_MODEL
IN_COLS = Q_COLS + K_COLS + V_COLS + R_COLS + A_COLS + U_COLS + G_COLS
IN_SPLITS = (Q_COLS, Q_COLS + K_COLS, Q_COLS + K_COLS + V_COLS, Q_COLS + K_COLS + V_COLS + R_COLS, Q_COLS + K_COLS + V_COLS + R_COLS + A_COLS, Q_COLS + K_COLS + V_COLS + R_COLS + A_COLS + U_COLS)

kernel_name = 'hybrid_gla_s5_hiermoe_step'


def _layer_norm(x):
    xf = x.astype(jnp.float32)
    mu = jnp.mean(xf, axis=-1, keepdims=True)
    var = jnp.mean(jnp.square(xf - mu), axis=-1, keepdims=True)
    return ((xf - mu) * lax.rsqrt(var + LN_EPS)).astype(x.dtype)


def _split_heads(a, n_heads):
    bsz, t, w = a.shape
    return a.reshape(bsz, t, n_heads, w // n_heads).transpose(0, 2, 1, 3)


def _gla(q, k, v, g_log, s0):
    bsz, nh, t, _ = q.shape
    dv = v.shape[-1]
    c = min(GLA_CHUNK, t)
    pad = (-t) % c
    if pad:
        pw = ((0, 0), (0, 0), (0, pad), (0, 0))
        q, k, v, g_log = (jnp.pad(a, pw) for a in (q, k, v, g_log))
    n_chunks = (t + pad) // c

    def to_chunks(a):
        return jnp.moveaxis(a.reshape(bsz, nh, n_chunks, c, a.shape[-1]), 2, 0)

    causal = jnp.tril(jnp.ones((c, c), dtype=bool))[None, None, :, :, None]

    def chunk_step(s, inp):
        qc, kc, vc, gc = inp
        b = jnp.cumsum(gc, axis=2)
        o_inter = jnp.einsum('bhtd,bhde->bhte', qc * jnp.exp(b).astype(qc.dtype), s)
        rel = jnp.where(causal, b[:, :, :, None, :] - b[:, :, None, :, :], -jnp.inf)
        decay = jnp.exp(rel).astype(qc.dtype)
        scores = jnp.einsum('bhtd,bhsd,bhtsd->bhts', qc, kc, decay)
        o = o_inter + jnp.einsum('bhts,bhse->bhte', scores, vc)
        b_last = b[:, :, -1:, :]
        k_dec = kc * jnp.exp(b_last - b).astype(kc.dtype)
        s_new = jnp.exp(b_last[:, :, 0, :, None]).astype(s.dtype) * s + jnp.einsum('bhsd,bhse->bhde', k_dec, vc)
        return s_new.astype(s.dtype), o.astype(qc.dtype)

    s_final, o = lax.scan(chunk_step, s0, (to_chunks(q), to_chunks(k), to_chunks(v), to_chunks(g_log)))
    o = jnp.moveaxis(o, 0, 2).reshape(bsz, nh, n_chunks * c, dv)[:, :, :t]
    return o, s_final


def _s5_discretise(a_re, a_im, b_re, b_im, log_dt):
    lam_re = jnp.minimum(a_re.astype(jnp.float32), -1e-4)
    lam_im = a_im.astype(jnp.float32)
    dt = jnp.exp(log_dt.astype(jnp.float32))[:, None]
    mag = jnp.exp(lam_re * dt)
    lbar_re = mag * jnp.cos(lam_im * dt)
    lbar_im = mag * jnp.sin(lam_im * dt)
    den = jnp.square(lam_re) + jnp.square(lam_im)
    f_re = ((lbar_re - 1.0) * lam_re + lbar_im * lam_im) / den
    f_im = (lbar_im * lam_re - (lbar_re - 1.0) * lam_im) / den
    br = b_re.astype(jnp.float32)
    bi = b_im.astype(jnp.float32)
    bbar_re = f_re[..., None] * br - f_im[..., None] * bi
    bbar_im = f_re[..., None] * bi + f_im[..., None] * br
    return lbar_re, lbar_im, bbar_re, bbar_im


def _s5_scan(u, h0_re, h0_im, lbar_re, lbar_im, bbar_re, bbar_im):
    bu_re = jnp.einsum('btgh,gph->btgp', u, bbar_re)
    bu_im = jnp.einsum('btgh,gph->btgp', u, bbar_im)
    bu_re = bu_re.at[:, 0].add(lbar_re * h0_re - lbar_im * h0_im)
    bu_im = bu_im.at[:, 0].add(lbar_re * h0_im + lbar_im * h0_re)
    a_re = jnp.broadcast_to(lbar_re, bu_re.shape)
    a_im = jnp.broadcast_to(lbar_im, bu_im.shape)

    def combine(e1, e2):
        a1r, a1i, b1r, b1i = e1
        a2r, a2i, b2r, b2i = e2
        return (a2r * a1r - a2i * a1i,
                a2r * a1i + a2i * a1r,
                a2r * b1r - a2i * b1i + b2r,
                a2r * b1i + a2i * b1r + b2i)

    _, _, h_re, h_im = lax.associative_scan(combine, (a_re, a_im, bu_re, bu_im), axis=1)
    return h_re, h_im


def _hier_moe(x, p):
    lead = x.shape[:-1]
    xf = x.reshape(-1, D_MODEL)
    n_tok = xf.shape[0]
    g_logits = (xf @ p['w_router_group'] + p['b_router_group']).astype(jnp.float32)
    g_idx = jnp.argmax(g_logits, axis=-1).astype(jnp.int32)
    g_prob = jnp.take_along_axis(jax.nn.softmax(g_logits, axis=-1), g_idx[:, None], axis=-1)
    e_logits = jnp.einsum('nd,gde->nge', xf, p['w_router_expert']) + p['b_router_expert']
    e_sel = jnp.take_along_axis(e_logits, g_idx[:, None, None], axis=1)[:, 0].astype(jnp.float32)
    top_v, top_i = lax.top_k(e_sel, MOE_TOP_K)
    weights = (jax.nn.softmax(top_v, axis=-1) * g_prob).reshape(-1)
    expert_id = (g_idx[:, None] * MOE_EXPERTS_PER_GROUP + top_i.astype(jnp.int32)).reshape(-1)
    token_id = jnp.repeat(jnp.arange(n_tok, dtype=jnp.int32), MOE_TOP_K)
    n_assign = n_tok * MOE_TOP_K
    order = jnp.argsort(expert_id)
    s_exp = expert_id[order]
    s_tok = token_id[order]
    s_w = weights[order]
    counts = jax.ops.segment_sum(jnp.ones_like(expert_id), expert_id, num_segments=N_EXPERTS)
    starts = jnp.cumsum(counts) - counts
    padded = (counts + MOE_BLOCK - 1) // MOE_BLOCK * MOE_BLOCK
    padded_end = jnp.cumsum(padded)
    padded_start = padded_end - padded
    dest = padded_start[s_exp] + jnp.arange(n_assign, dtype=jnp.int32) - starts[s_exp]
    n_blocks = (n_assign + N_EXPERTS * (MOE_BLOCK - 1) + MOE_BLOCK - 1) // MOE_BLOCK
    n_rows = n_blocks * MOE_BLOCK
    rows = jnp.zeros((n_rows, D_MODEL), x.dtype).at[dest].set(xf[s_tok])
    row_w = jnp.zeros((n_rows,), jnp.float32).at[dest].set(s_w)
    row_tok = jnp.zeros((n_rows,), jnp.int32).at[dest].set(s_tok)
    block_start = jnp.arange(n_blocks, dtype=jnp.int32) * MOE_BLOCK
    block_exp = jnp.minimum(jnp.searchsorted(padded_end, block_start, side='right'), N_EXPERTS - 1)
    wg = p['w_moe_gate'].reshape(N_EXPERTS, D_MODEL, MOE_HIDDEN)
    wu = p['w_moe_up'].reshape(N_EXPERTS, D_MODEL, MOE_HIDDEN)
    wd = p['w_moe_down'].reshape(N_EXPERTS, MOE_HIDDEN, D_MODEL)

    def block_ffn(args):
        xb, e = args
        h = jax.nn.silu(xb @ wg[e]) * (xb @ wu[e])
        return h @ wd[e]

    out = lax.map(block_ffn, (rows.reshape(n_blocks, MOE_BLOCK, D_MODEL), block_exp))
    out = out.reshape(n_rows, D_MODEL) * row_w[:, None].astype(x.dtype)
    y = jax.ops.segment_sum(out, row_tok, num_segments=n_tok)
    return y.reshape(lead + (D_MODEL,))


def _trunk_layer(x, gla_s0, s5_h0_re, s5_h0_im, p):
    bsz, t, _ = x.shape
    proj = x @ p['w_in']
    q, k, v, r, a_low, u, gate_in = jnp.split(proj, IN_SPLITS, axis=-1)
    g_log = jax.nn.log_sigmoid((a_low @ p['w_gla_gate_up'] + p['b_gla_gate_up']).astype(jnp.float32)) / GLA_GATE_TAU
    o, gla_new = _gla(_split_heads(q * GLA_HEAD_DK ** -0.5, GLA_HEADS), _split_heads(k, GLA_HEADS),
                      _split_heads(v, GLA_HEADS), _split_heads(g_log, GLA_HEADS), gla_s0)
    o = _layer_norm(o.transpose(0, 2, 1, 3)).reshape(bsz, t, GLA_DV) * p['w_gla_norm']
    o = o * jax.nn.silu(r)
    lbar_re, lbar_im, bbar_re, bbar_im = _s5_discretise(p['s5_a_re'], p['s5_a_im'], p['s5_b_re'], p['s5_b_im'], p['s5_log_dt'])
    uf = u.astype(jnp.float32).reshape(bsz, t, S5_GROUPS, S5_GROUP)
    h_re, h_im = _s5_scan(uf, s5_h0_re.astype(jnp.float32), s5_h0_im.astype(jnp.float32), lbar_re, lbar_im, bbar_re, bbar_im)
    y = (jnp.einsum('btgp,ghp->btgh', h_re, p['s5_c_re'].astype(jnp.float32))
         - jnp.einsum('btgp,ghp->btgh', h_im, p['s5_c_im'].astype(jnp.float32))
         + p['s5_d'].astype(jnp.float32).reshape(S5_GROUPS, S5_GROUP) * uf)
    y = jax.nn.gelu(y.reshape(bsz, t, S5_WIDTH)).astype(x.dtype)
    z = y * jax.nn.sigmoid(y @ p['w_s5_glu'] + p['b_s5_glu'])
    branches = jnp.stack([o, z], axis=2)
    proj_b = jnp.einsum('btnc,ncd->btnd', branches, p['w_branch'])
    gates = jax.nn.sigmoid(gate_in.reshape(bsz, t, N_BRANCH, D_MODEL))
    mixed = jnp.sum(gates * proj_b, axis=2) @ p['w_out']
    x1 = _layer_norm(DEEPNORM_ALPHA * x + mixed) * p['ln1_g'] + p['ln1_b']
    x2 = _layer_norm(DEEPNORM_ALPHA * x1 + _hier_moe(x1, p)) * p['ln2_g'] + p['ln2_b']
    return x2, gla_new, h_re[:, -1].astype(s5_h0_re.dtype), h_im[:, -1].astype(s5_h0_im.dtype)


def setup_inputs(seed: int = 0) -> dict:
    key = jax.random.key(seed)
    ks = jax.random.split(key, 32)

    def nrm(k, shape, scale):
        return scale * jax.random.normal(k, shape, jnp.float32)

    beta = DEEPNORM_BETA
    col_scale = jnp.concatenate([
        jnp.ones((Q_COLS + K_COLS,), jnp.float32),
        jnp.full((V_COLS,), beta, jnp.float32),
        jnp.ones((R_COLS + A_COLS,), jnp.float32),
        jnp.full((U_COLS,), beta, jnp.float32),
        jnp.ones((G_COLS,), jnp.float32)])
    return {
        'x_prompt': nrm(ks[0], (BATCH, SEQ, D_MODEL), 1.0),
        'x_sample': nrm(ks[1], (DEC_BATCH, DEC_SEQ, D_MODEL), 1.0),
        'state_gla': nrm(ks[2], (DEC_BATCH, GLA_HEADS, GLA_HEAD_DK, GLA_HEAD_DV), 0.1),
        'state_s5_re': nrm(ks[3], (DEC_BATCH, S5_GROUPS, S5_STATE), 0.1),
        'state_s5_im': nrm(ks[4], (DEC_BATCH, S5_GROUPS, S5_STATE), 0.1),
        'w_in': nrm(ks[5], (D_MODEL, IN_COLS), D_MODEL ** -0.5) * col_scale,
        'w_gla_gate_up': nrm(ks[6], (GLA_GATE_RANK, GLA_DK), GLA_GATE_RANK ** -0.5),
        'b_gla_gate_up': nrm(ks[7], (GLA_DK,), 0.1),
        'w_gla_norm': 1.0 + nrm(ks[8], (GLA_DV,), 0.02),
        's5_a_re': -0.5 + nrm(ks[9], (S5_GROUPS, S5_STATE), 0.01),
        's5_a_im': math.pi * jnp.arange(S5_STATE, dtype=jnp.float32)[None, :] + nrm(ks[10], (S5_GROUPS, S5_STATE), 0.01),
        's5_b_re': nrm(ks[11], (S5_GROUPS, S5_STATE, S5_GROUP), (2 * S5_GROUP) ** -0.5),
        's5_b_im': nrm(ks[12], (S5_GROUPS, S5_STATE, S5_GROUP), (2 * S5_GROUP) ** -0.5),
        's5_c_re': nrm(ks[13], (S5_GROUPS, S5_GROUP, S5_STATE), S5_STATE ** -0.5),
        's5_c_im': nrm(ks[14], (S5_GROUPS, S5_GROUP, S5_STATE), S5_STATE ** -0.5),
        's5_d': nrm(ks[15], (S5_WIDTH,), 1.0),
        's5_log_dt': jax.random.uniform(ks[16], (S5_GROUPS,), jnp.float32, math.log(S5_DT_MIN), math.log(S5_DT_MAX)),
        'w_s5_glu': nrm(ks[17], (S5_WIDTH, S5_WIDTH), S5_WIDTH ** -0.5),
        'b_s5_glu': nrm(ks[18], (S5_WIDTH,), 0.02),
        'w_branch': nrm(ks[19], (N_BRANCH, BRANCH_WIDTH, D_MODEL), BRANCH_WIDTH ** -0.5 * beta),
        'w_out': nrm(ks[20], (D_MODEL, D_MODEL), D_MODEL ** -0.5 * beta),
        'ln1_g': 1.0 + nrm(ks[21], (D_MODEL,), 0.02),
        'ln1_b': nrm(ks[22], (D_MODEL,), 0.02),
        'w_router_group': nrm(ks[23], (D_MODEL, MOE_GROUPS), D_MODEL ** -0.5),
        'b_router_group': nrm(ks[24], (MOE_GROUPS,), 0.01),
        'w_router_expert': nrm(ks[25], (MOE_GROUPS, D_MODEL, MOE_EXPERTS_PER_GROUP), D_MODEL ** -0.5),
        'b_router_expert': nrm(ks[26], (MOE_GROUPS, MOE_EXPERTS_PER_GROUP), 0.01),
        'w_moe_gate': nrm(ks[27], (MOE_GROUPS, MOE_EXPERTS_PER_GROUP, D_MODEL, MOE_HIDDEN), D_MODEL ** -0.5),
        'w_moe_up': nrm(ks[28], (MOE_GROUPS, MOE_EXPERTS_PER_GROUP, D_MODEL, MOE_HIDDEN), D_MODEL ** -0.5 * beta),
        'w_moe_down': nrm(ks[29], (MOE_GROUPS, MOE_EXPERTS_PER_GROUP, MOE_HIDDEN, D_MODEL), MOE_HIDDEN ** -0.5 * beta),
        'ln2_g': 1.0 + nrm(ks[30], (D_MODEL,), 0.02),
        'ln2_b': nrm(ks[31], (D_MODEL,), 0.02),
    }


def reference(x_prompt, x_sample, state_gla, state_s5_re, state_s5_im, w_in, w_gla_gate_up, b_gla_gate_up,
              w_gla_norm, s5_a_re, s5_a_im, s5_b_re, s5_b_im, s5_c_re, s5_c_im, s5_d, s5_log_dt, w_s5_glu,
              b_s5_glu, w_branch, w_out, ln1_g, ln1_b, w_router_group, b_router_group, w_router_expert,
              b_router_expert, w_moe_gate, w_moe_up, w_moe_down, ln2_g, ln2_b):
    p = {
        'w_in': w_in, 'w_gla_gate_up': w_gla_gate_up, 'b_gla_gate_up': b_gla_gate_up, 'w_gla_norm': w_gla_norm,
        's5_a_re': s5_a_re, 's5_a_im': s5_a_im, 's5_b_re': s5_b_re, 's5_b_im': s5_b_im,
        's5_c_re': s5_c_re, 's5_c_im': s5_c_im, 's5_d': s5_d, 's5_log_dt': s5_log_dt,
        'w_s5_glu': w_s5_glu, 'b_s5_glu': b_s5_glu, 'w_branch': w_branch, 'w_out': w_out,
        'ln1_g': ln1_g, 'ln1_b': ln1_b, 'w_router_group': w_router_group, 'b_router_group': b_router_group,
        'w_router_expert': w_router_expert, 'b_router_expert': b_router_expert,
        'w_moe_gate': w_moe_gate, 'w_moe_up': w_moe_up, 'w_moe_down': w_moe_down,
        'ln2_g': ln2_g, 'ln2_b': ln2_b,
    }
    bp = x_prompt.shape[0]
    gla_p = jnp.zeros((bp,) + state_gla.shape[1:], state_gla.dtype)
    s5_re_p = jnp.zeros((bp,) + state_s5_re.shape[1:], state_s5_re.dtype)
    s5_im_p = jnp.zeros((bp,) + state_s5_im.shape[1:], state_s5_im.dtype)
    gla_s, s5_re_s, s5_im_s = state_gla, state_s5_re, state_s5_im
    y_prompt, y_sample = x_prompt, x_sample
    for _layer in range(DEPTH):
        y_prompt, gla_p, s5_re_p, s5_im_p = _trunk_layer(y_prompt, gla_p, s5_re_p, s5_im_p, p)
        y_sample, gla_s, s5_re_s, s5_im_s = _trunk_layer(y_sample, gla_s, s5_re_s, s5_im_s, p)
    return (y_prompt, y_sample, gla_p, s5_re_p, s5_im_p, gla_s, s5_re_s, s5_im_s)
```

```python
import functools
import math

import jax
import jax.numpy as jnp
import numpy as np
from jax import lax
from jax.experimental import pallas as pl
from jax.experimental.pallas import tpu as pltpu

F32 = jnp.float32
BF16 = jnp.bfloat16
HIGHEST = lax.Precision.HIGHEST

D_MODEL = 4096
N_PROMPT_SEQ = 4
SEQ = 2048
N_SAMPLE = 128
N_PROMPT = N_PROMPT_SEQ * SEQ
N_TOK = N_PROMPT + N_SAMPLE
BRANCH = D_MODEL // 2
GLA_HEADS = 4
GLA_DK = D_MODEL // 4
GLA_HDK = GLA_DK // GLA_HEADS
GLA_HDV = BRANCH // GLA_HEADS
GLA_RANK = 16
GLA_TAU = 16.0
GLA_CHUNK = 64
GLA_LEVELS = 6
S5_GROUP = 16
S5_GROUPS = BRANCH // S5_GROUP
S5_STATE = 64
S5_CHUNK = 8
S5_LANE_GROUPS = 8
S5_BLOCKS = S5_GROUPS // S5_LANE_GROUPS
N_EXPERTS = 64
MOE_HIDDEN = D_MODEL // 8
MOE_BLOCK = 128
MOE_HSPLIT = 2
LN_EPS = 1e-5
ALPHA = 2.0 ** 0.25
COL_A = 6144
COL_U = 6160
COL_G = 8208
IN_COLS = 16400

VMEM_LIMIT = 56 * 1024 * 1024
TM = 1040
TN = 512


def _cp(sem):
    return pltpu.CompilerParams(dimension_semantics=sem, vmem_limit_bytes=VMEM_LIMIT)


def _sigmoid(x):
    return 1.0 / (1.0 + jnp.exp(-x))


def _log_sigmoid(z):
    return jnp.minimum(z, 0.0) - jnp.log1p(jnp.exp(-jnp.abs(z)))


def _gelu_tanh(x):
    return 0.5 * x * (1.0 + jnp.tanh(math.sqrt(2.0 / math.pi) * (x + 0.044715 * (x * x * x))))


def _layer_norm(x):
    mu = jnp.mean(x, axis=-1, keepdims=True)
    xc = x - mu
    var = jnp.mean(xc * xc, axis=-1, keepdims=True)
    return xc * lax.rsqrt(var + LN_EPS)


def _mm_body(a_ref, w_ref, *rest, epilogue):
    *extra, o_ref = rest
    acc = jnp.dot(a_ref[...].astype(BF16), w_ref[...].astype(BF16), preferred_element_type=F32)
    o_ref[...] = epilogue(acc, *[e[...] for e in extra]).astype(o_ref.dtype)


def _mm(a, w, *, col0, ncols, tn, out_dtype, name, epilogue=lambda acc: acc, extra=(), extra_specs=()):
    m, k = a.shape
    cb0 = col0 // tn
    return pl.pallas_call(
        functools.partial(_mm_body, epilogue=epilogue),
        grid=(m // TM, ncols // tn),
        in_specs=[pl.BlockSpec((TM, k), lambda i, j: (i, 0)),
                  pl.BlockSpec((k, tn), lambda i, j: (0, cb0 + j)),
                  *extra_specs],
        out_specs=pl.BlockSpec((TM, tn), lambda i, j: (i, j)),
        out_shape=jax.ShapeDtypeStruct((m, ncols), out_dtype),
        compiler_params=_cp(("arbitrary", "arbitrary")),
        name=name,
    )(a, w, *extra)


def _gla_coeff_matrix():
    c = GLA_CHUNK
    t = np.arange(c)[:, None]
    u = np.arange(c)[None, :]
    blocks = [(u <= t), (u > t)]
    for lvl in range(1, GLA_LEVELS + 1):
        m = 1 << lvl
        half = m // 2
        mid = (t // m) * m + half - 1
        lower = (t % m) >= half
        blocks.append(np.where(lower, (u > mid) & (u <= t), (u > t) & (u <= mid)))
    return np.concatenate(blocks, axis=0).astype(np.float32)


def _gla_level_masks():
    c = GLA_CHUNK
    t = lax.broadcasted_iota(jnp.int32, (c, c), 0)
    s = lax.broadcasted_iota(jnp.int32, (c, c), 1)
    masks = []
    for lvl in range(1, GLA_LEVELS + 1):
        m = 1 << lvl
        half = m // 2
        masks.append(((t >> lvl) == (s >> lvl)) & ((t & (m - 1)) >= half) & ((s & (m - 1)) < half))
    return masks


def _gla_out_norm(o, r, wn):
    return _layer_norm(o) * wn * (r * _sigmoid(r))


def _gla_prompt_body(q_ref, k_ref, v_ref, r_ref, a_ref, wgu_ref, bgu_ref, wn_ref, cm_ref,
                     o_ref, st_ref, s_scr, *, n_sub):
    c = GLA_CHUNK

    @pl.when(pl.program_id(2) == 0)
    def _():
        s_scr[...] = jnp.zeros_like(s_scr)

    masks = _gla_level_masks()
    wgu = wgu_ref[...]
    bgu = bgu_ref[...]
    wn = wn_ref[...]
    cm = cm_ref[...]
    nt = (((1,), (1,)), ((), ()))
    tn = (((0,), (0,)), ((), ()))

    def chunk(ci, carry):
        rows = pl.ds(pl.multiple_of(ci * c, c), c)
        q = q_ref[rows, :] * (GLA_HDK ** -0.5)
        k = k_ref[rows, :]
        v = v_ref[rows, :]
        z = jnp.dot(a_ref[rows, :], wgu, preferred_element_type=F32, precision=HIGHEST) + bgu
        g = _log_sigmoid(z) * (1.0 / GLA_TAU)
        f = jnp.exp(jnp.dot(cm, g, preferred_element_type=F32, precision=HIGHEST))
        st = s_scr[...]
        vb = v.astype(BF16)
        o = lax.dot_general((q * f[0:c]).astype(BF16), st.astype(BF16), nt, preferred_element_type=F32)
        scores = jnp.zeros((c, c), F32)
        for lvl in range(GLA_LEVELS):
            fl = f[(2 + lvl) * c:(3 + lvl) * c]
            p = lax.dot_general((q * fl).astype(BF16), (k * fl).astype(BF16), nt, preferred_element_type=F32)
            scores = scores + jnp.where(masks[lvl], p, 0.0)
        diag = jnp.sum(q * k, axis=1, keepdims=True)
        o = o + jnp.dot(scores.astype(BF16), vb, preferred_element_type=F32) + diag * v
        kd = (k * f[c:2 * c]).astype(BF16)
        s_scr[...] = st * f[c - 1:c, :] + lax.dot_general(vb, kd, tn, preferred_element_type=F32)
        o_ref[rows, :] = _gla_out_norm(o, r_ref[rows, :], wn).astype(o_ref.dtype)
        return carry

    lax.fori_loop(0, n_sub, chunk, 0)

    @pl.when(pl.program_id(2) == pl.num_programs(2) - 1)
    def _():
        st_ref[0, 0] = s_scr[...]


def _gla_prompt(qkvr, a_low, wgu, bgu, wn):
    n_sub = 4
    tt = GLA_CHUNK * n_sub
    nt_steps = SEQ // tt
    cm = jnp.asarray(_gla_coeff_matrix())
    rows = lambda b, h, c: b * nt_steps + c
    o, st = pl.pallas_call(
        functools.partial(_gla_prompt_body, n_sub=n_sub),
        grid=(N_PROMPT_SEQ, GLA_HEADS, nt_steps),
        in_specs=[
            pl.BlockSpec((tt, GLA_HDK), lambda b, h, c: (rows(b, h, c), h)),
            pl.BlockSpec((tt, GLA_HDK), lambda b, h, c: (rows(b, h, c), GLA_HEADS + h)),
            pl.BlockSpec((tt, GLA_HDV), lambda b, h, c: (rows(b, h, c), GLA_HEADS + h)),
            pl.BlockSpec((tt, GLA_HDV), lambda b, h, c: (rows(b, h, c), 2 * GLA_HEADS + h)),
            pl.BlockSpec((tt, 128), lambda b, h, c: (rows(b, h, c), 0)),
            pl.BlockSpec((128, GLA_HDK), lambda b, h, c: (0, h)),
            pl.BlockSpec((1, GLA_HDK), lambda b, h, c: (0, h)),
            pl.BlockSpec((1, GLA_HDV), lambda b, h, c: (0, h)),
            pl.BlockSpec(cm.shape, lambda b, h, c: (0, 0)),
        ],
        out_specs=[
            pl.BlockSpec((tt, GLA_HDV), lambda b, h, c: (rows(b, h, c), h)),
            pl.BlockSpec((1, 1, GLA_HDV, GLA_HDK), lambda b, h, c: (b, h, 0, 0)),
        ],
        out_shape=[
            jax.ShapeDtypeStruct((N_PROMPT, BRANCH), BF16),
            jax.ShapeDtypeStruct((N_PROMPT_SEQ, GLA_HEADS, GLA_HDV, GLA_HDK), F32),
        ],
        scratch_shapes=[pltpu.VMEM((GLA_HDV, GLA_HDK), F32)],
        compiler_params=_cp(("arbitrary", "arbitrary", "arbitrary")),
        name="gla_prompt",
    )(qkvr, qkvr, qkvr, qkvr, a_low, wgu, bgu, wn, cm)
    return o, st


GLA_SB = 16


def _gla_sample_body(q_ref, k_ref, v_ref, r_ref, a_ref, wgu_ref, bgu_ref, wn_ref, s_ref, o_ref, so_ref):
    q = q_ref[...] * (GLA_HDK ** -0.5)
    k = k_ref[...]
    v = v_ref[...]
    z = jnp.dot(a_ref[...], wgu_ref[...], preferred_element_type=F32, precision=HIGHEST) + bgu_ref[...]
    eg = jnp.exp(_log_sigmoid(z) * (1.0 / GLA_TAU))
    qe = (q * eg).astype(BF16)
    eg_t = eg.T
    k_t = k.T
    rows = []
    for n in range(GLA_SB):
        s0 = s_ref[n, 0]
        rows.append(jnp.dot(qe, s0.astype(BF16), preferred_element_type=F32)[n:n + 1])
        so_ref[n, 0] = s0 * eg_t[:, n:n + 1] + k_t[:, n:n + 1] * v[n:n + 1, :]
    o = jnp.concatenate(rows, axis=0) + jnp.sum(q * k, axis=1, keepdims=True) * v
    o_ref[...] = _gla_out_norm(o, r_ref[...], wn_ref[...]).astype(o_ref.dtype)


def _gla_sample(qkvr, a_low, wgu, bgu, wn, state):
    r0 = N_PROMPT // GLA_SB
    o, st = pl.pallas_call(
        _gla_sample_body,
        grid=(GLA_HEADS, N_SAMPLE // GLA_SB),
        in_specs=[
            pl.BlockSpec((GLA_SB, GLA_HDK), lambda h, i: (r0 + i, h)),
            pl.BlockSpec((GLA_SB, GLA_HDK), lambda h, i: (r0 + i, GLA_HEADS + h)),
            pl.BlockSpec((GLA_SB, GLA_HDV), lambda h, i: (r0 + i, GLA_HEADS + h)),
            pl.BlockSpec((GLA_SB, GLA_HDV), lambda h, i: (r0 + i, 2 * GLA_HEADS + h)),
            pl.BlockSpec((GLA_SB, 128), lambda h, i: (r0 + i, 0)),
            pl.BlockSpec((128, GLA_HDK), lambda h, i: (0, h)),
            pl.BlockSpec((1, GLA_HDK), lambda h, i: (0, h)),
            pl.BlockSpec((1, GLA_HDV), lambda h, i: (0, h)),
            pl.BlockSpec((GLA_SB, 1, GLA_HDK, GLA_HDV), lambda h, i: (i, h, 0, 0)),
        ],
        out_specs=[
            pl.BlockSpec((GLA_SB, GLA_HDV), lambda h, i: (i, h)),
            pl.BlockSpec((GLA_SB, 1, GLA_HDK, GLA_HDV), lambda h, i: (i, h, 0, 0)),
        ],
        out_shape=[
            jax.ShapeDtypeStruct((N_SAMPLE, BRANCH), BF16),
            jax.ShapeDtypeStruct(state.shape, F32),
        ],
        compiler_params=_cp(("arbitrary", "arbitrary")),
        name="gla_sample",
    )(qkvr, qkvr, qkvr, qkvr, a_low, wgu, bgu, wn, state)
    return o, st


def _s5_weights(a_re, a_im, b_re, b_im, c_re, c_im, d, log_dt):
    L = S5_CHUNK
    lam_re = jnp.minimum(a_re, -1e-4)
    lam_im = a_im
    dt = jnp.exp(log_dt)[:, None]
    kk = jnp.arange(L + 1, dtype=F32)[:, None, None]
    pow_re = jnp.exp(lam_re * dt * kk) * jnp.cos(lam_im * dt * kk)
    pow_im = jnp.exp(lam_re * dt * kk) * jnp.sin(lam_im * dt * kk)
    lbar_re, lbar_im = pow_re[1], pow_im[1]
    den = lam_re * lam_re + lam_im * lam_im
    f_re = ((lbar_re - 1.0) * lam_re + lbar_im * lam_im) / den
    f_im = (lbar_im * lam_re - (lbar_re - 1.0) * lam_im) / den
    bb_re = f_re[..., None] * b_re - f_im[..., None] * b_im
    bb_im = f_re[..., None] * b_im + f_im[..., None] * b_re
    lb_re = pow_re[:L, :, :, None] * bb_re - pow_im[:L, :, :, None] * bb_im
    lb_im = pow_re[:L, :, :, None] * bb_im + pow_im[:L, :, :, None] * bb_re
    cl_re = c_re[None] * pow_re[1:, :, None, :] - c_im[None] * pow_im[1:, :, None, :]
    cl_im = c_re[None] * pow_im[1:, :, None, :] + c_im[None] * pow_re[1:, :, None, :]
    kern = (jnp.einsum('ghp,kgpi->kghi', c_re, lb_re, precision=HIGHEST)
            - jnp.einsum('ghp,kgpi->kghi', c_im, lb_im, precision=HIGHEST))
    eye = jnp.eye(S5_LANE_GROUPS, dtype=F32)
    nb, ng = S5_BLOCKS, S5_LANE_GROUPS
    sp = jnp.arange(L)[:, None]
    s = jnp.arange(L)[None, :]
    lag = s - sp
    kern_b = kern.reshape(L, nb, ng, S5_GROUP, S5_GROUP)
    toe = jnp.where((lag >= 0)[:, :, None, None, None, None], kern_b[jnp.clip(lag, 0)], 0.0)
    t_mat = jnp.einsum('abjghi,Gg->jaGibgh', toe, eye).reshape(nb, L * 128, L * 128)
    lbz_re = lb_re[::-1].reshape(L, nb, ng, S5_STATE, S5_GROUP)
    lbz_im = lb_im[::-1].reshape(L, nb, ng, S5_STATE, S5_GROUP)
    wz = jnp.concatenate([
        jnp.einsum('ajgpi,Gg->jaGigp', lbz_re, eye).reshape(nb, L * 128, ng * S5_STATE),
        jnp.einsum('ajgpi,Gg->jaGigp', lbz_im, eye).reshape(nb, L * 128, ng * S5_STATE)], axis=2)
    clb_re = cl_re.reshape(L, nb, ng, S5_GROUP, S5_STATE)
    clb_im = cl_im.reshape(L, nb, ng, S5_GROUP, S5_STATE)
    wc = jnp.concatenate([
        jnp.einsum('bjghp,Gg->jGpbgh', clb_re, eye).reshape(nb, ng * S5_STATE, L * 128),
        jnp.einsum('bjghp,Gg->jGpbgh', -clb_im, eye).reshape(nb, ng * S5_STATE, L * 128)], axis=1)
    n_steps = int(math.log2(SEQ // L))
    mult = (L * (2.0 ** jnp.arange(n_steps, dtype=F32)))[:, None, None]
    sc_re = jnp.exp(lam_re * dt * mult) * jnp.cos(lam_im * dt * mult)
    sc_im = jnp.exp(lam_re * dt * mult) * jnp.sin(lam_im * dt * mult)

    def state_lanes(x):
        lead = x.shape[:-2]
        return jnp.moveaxis(x.reshape(lead + (nb, ng * S5_STATE)), -2, 0)

    scan_mult = jnp.concatenate([state_lanes(sc_re), state_lanes(sc_im)], axis=-1)
    lbar1 = jnp.concatenate([state_lanes(lbar_re[None]), state_lanes(lbar_im[None])], axis=-1)
    dvec = jnp.tile(d.reshape(nb, 1, 128), (1, 1, L))
    return t_mat.astype(BF16), wz.astype(BF16), wc.astype(BF16), scan_mult, lbar1, dvec


def _s5_prompt_body(u_ref, t_ref, wz_ref, wc_ref, sm_ref, d_ref, y_ref, fre_ref, fim_ref, *, n_steps):
    L = S5_CHUNK
    n_rows = SEQ // L
    ns = S5_LANE_GROUPS * S5_STATE
    v = jnp.concatenate([u_ref[pl.ds(s, n_rows, stride=L), :] for s in range(L)], axis=1)
    vb = v.astype(BF16)
    z = jnp.dot(vb, wz_ref[0], preferred_element_type=F32)
    hr, hi = z[:, :ns], z[:, ns:]
    pos = lax.broadcasted_iota(jnp.int32, (n_rows, ns), 0)
    sm = sm_ref[0]
    for d in range(n_steps):
        sh = 1 << d
        ar, ai = sm[d:d + 1, :ns], sm[d:d + 1, ns:]
        keep = pos >= sh
        pr = jnp.where(keep, pltpu.roll(hr, sh, 0), 0.0)
        pi = jnp.where(keep, pltpu.roll(hi, sh, 0), 0.0)
        hr, hi = hr + ar * pr - ai * pi, hi + ar * pi + ai * pr
    fre_ref[0] = hr[n_rows - 1:n_rows]
    fim_ref[0] = hi[n_rows - 1:n_rows]
    first = pos >= 1
    h_prev = jnp.concatenate([jnp.where(first, pltpu.roll(hr, 1, 0), 0.0),
                              jnp.where(first, pltpu.roll(hi, 1, 0), 0.0)], axis=1)
    y = (jnp.dot(vb, t_ref[0], preferred_element_type=F32)
         + jnp.dot(h_prev.astype(BF16), wc_ref[0], preferred_element_type=F32)
         + d_ref[0] * v)
    y = _gelu_tanh(y)
    for s in range(L):
        y_ref[pl.ds(s, n_rows, stride=L), :] = y[:, s * 128:(s + 1) * 128]


def _s5_sample_body(u_ref, k0_ref, wb_ref, wc0_ref, l1_ref, d_ref, hre_ref, him_ref,
                    y_ref, sre_ref, sim_ref):
    ns = S5_LANE_GROUPS * S5_STATE
    us = u_ref[...]
    usb = us.astype(BF16)
    h0r, h0i = hre_ref[...], him_ref[...]
    l1 = l1_ref[0]
    bu = jnp.dot(usb, wb_ref[0], preferred_element_type=F32)
    sre_ref[...] = l1[:, :ns] * h0r - l1[:, ns:] * h0i + bu[:, :ns]
    sim_ref[...] = l1[:, :ns] * h0i + l1[:, ns:] * h0r + bu[:, ns:]
    h0 = jnp.concatenate([h0r, h0i], axis=1).astype(BF16)
    ys = (jnp.dot(usb, k0_ref[0], preferred_element_type=F32)
          + jnp.dot(h0, wc0_ref[0], preferred_element_type=F32)
          + d_ref[0] * us)
    y_ref[...] = _gelu_tanh(ys)


def _s5(u, weights, st_re, st_im):
    t_mat, wz, wc, scan_mult, lbar1, dvec = weights
    L = S5_CHUNK
    ns = S5_LANE_GROUPS * S5_STATE
    n_steps = scan_mult.shape[1]
    wblk = lambda a: pl.BlockSpec((1,) + a.shape[1:], lambda j, b: (j, 0, 0))
    y, f_re, f_im = pl.pallas_call(
        functools.partial(_s5_prompt_body, n_steps=n_steps),
        grid=(S5_BLOCKS, N_PROMPT_SEQ),
        in_specs=[
            pl.BlockSpec((SEQ, 128), lambda j, b: (b, j)),
            wblk(t_mat), wblk(wz), wblk(wc), wblk(scan_mult), wblk(dvec),
        ],
        out_specs=[
            pl.BlockSpec((SEQ, 128), lambda j, b: (b, j)),
            pl.BlockSpec((1, 1, ns), lambda j, b: (b, 0, j)),
            pl.BlockSpec((1, 1, ns), lambda j, b: (b, 0, j)),
        ],
        out_shape=[
            jax.ShapeDtypeStruct((N_PROMPT, BRANCH), F32),
            jax.ShapeDtypeStruct((N_PROMPT_SEQ, 1, S5_GROUPS * S5_STATE), F32),
            jax.ShapeDtypeStruct((N_PROMPT_SEQ, 1, S5_GROUPS * S5_STATE), F32),
        ],
        compiler_params=_cp(("arbitrary", "arbitrary")),
        name="s5_prompt",
    )(u, t_mat, wz, wc, scan_mult, dvec)
    k0 = t_mat[:, 0:128, 0:128]
    wb = wz[:, (L - 1) * 128:L * 128, :]
    wc0 = wc[:, :, 0:128]
    d0 = dvec[:, :, 0:128]
    sblk = lambda a: pl.BlockSpec((1,) + a.shape[1:], lambda j: (j, 0, 0))
    r0 = N_PROMPT // N_SAMPLE
    y_s, s_re, s_im = pl.pallas_call(
        _s5_sample_body,
        grid=(S5_BLOCKS,),
        in_specs=[
            pl.BlockSpec((N_SAMPLE, 128), lambda j: (r0, j)),
            sblk(k0), sblk(wb), sblk(wc0), sblk(lbar1), sblk(d0),
            pl.BlockSpec((N_SAMPLE, ns), lambda j: (0, j)),
            pl.BlockSpec((N_SAMPLE, ns), lambda j: (0, j)),
        ],
        out_specs=[
            pl.BlockSpec((N_SAMPLE, 128), lambda j: (0, j)),
            pl.BlockSpec((N_SAMPLE, ns), lambda j: (0, j)),
            pl.BlockSpec((N_SAMPLE, ns), lambda j: (0, j)),
        ],
        out_shape=[
            jax.ShapeDtypeStruct((N_SAMPLE, BRANCH), F32),
            jax.ShapeDtypeStruct((N_SAMPLE, S5_GROUPS * S5_STATE), F32),
            jax.ShapeDtypeStruct((N_SAMPLE, S5_GROUPS * S5_STATE), F32),
        ],
        compiler_params=_cp(("arbitrary",)),
        name="s5_sample",
    )(u, k0, wb, wc0, lbar1, d0, st_re, st_im)
    return jnp.concatenate([y, y_s], axis=0), f_re, f_im, s_re, s_im


def _merge_body(o_ref, z_ref, w0_ref, w1_ref, g0_ref, g1_ref, out_ref):
    p0 = jnp.dot(o_ref[...], w0_ref[0].astype(BF16), preferred_element_type=F32)
    p1 = jnp.dot(z_ref[...], w1_ref[0].astype(BF16), preferred_element_type=F32)
    out_ref[...] = (g0_ref[...] * p0 + g1_ref[...] * p1).astype(out_ref.dtype)


def _merge(o, z, w_branch, gates):
    ncb = D_MODEL // TN
    return pl.pallas_call(
        _merge_body,
        grid=(N_TOK // TM, ncb),
        in_specs=[
            pl.BlockSpec((TM, BRANCH), lambda i, j: (i, 0)),
            pl.BlockSpec((TM, BRANCH), lambda i, j: (i, 0)),
            pl.BlockSpec((1, BRANCH, TN), lambda i, j: (0, 0, j)),
            pl.BlockSpec((1, BRANCH, TN), lambda i, j: (1, 0, j)),
            pl.BlockSpec((TM, TN), lambda i, j: (i, j)),
            pl.BlockSpec((TM, TN), lambda i, j: (i, ncb + j)),
        ],
        out_specs=pl.BlockSpec((TM, TN), lambda i, j: (i, j)),
        out_shape=jax.ShapeDtypeStruct((N_TOK, D_MODEL), BF16),
        compiler_params=_cp(("arbitrary", "arbitrary")),
        name="merge",
    )(o, z, w_branch, w_branch, gates, gates)


OUT_TM = 416
OUT_TK = 512


def _out_ln_body(a_ref, w_ref, x_ref, g_ref, b_ref, o_ref):
    kk = pl.program_id(1)
    part = jnp.dot(a_ref[...], w_ref[...], preferred_element_type=F32)

    @pl.when(kk == 0)
    def _():
        o_ref[...] = part

    @pl.when(kk > 0)
    def _():
        o_ref[...] += part

    @pl.when(kk == pl.num_programs(1) - 1)
    def _():
        o_ref[...] = _layer_norm(ALPHA * x_ref[...] + o_ref[...]) * g_ref[...] + b_ref[...]


def _out_ln(pre, w_out_bf, x_all, g, b):
    return pl.pallas_call(
        _out_ln_body,
        grid=(N_TOK // OUT_TM, D_MODEL // OUT_TK),
        in_specs=[
            pl.BlockSpec((OUT_TM, OUT_TK), lambda i, k: (i, k)),
            pl.BlockSpec((OUT_TK, D_MODEL), lambda i, k: (k, 0)),
            pl.BlockSpec((OUT_TM, D_MODEL), lambda i, k: (i, 0)),
            pl.BlockSpec((1, D_MODEL), lambda i, k: (0, 0)),
            pl.BlockSpec((1, D_MODEL), lambda i, k: (0, 0)),
        ],
        out_specs=pl.BlockSpec((OUT_TM, D_MODEL), lambda i, k: (i, 0)),
        out_shape=jax.ShapeDtypeStruct((N_TOK, D_MODEL), F32),
        compiler_params=_cp(("arbitrary", "arbitrary")),
        name="out_ln1",
    )(pre, w_out_bf, x_all, g, b)


def _router_body(x_ref, w_ref, b_ref, ids_ref, wts_ref):
    logits = jnp.dot(x_ref[...], w_ref[...], preferred_element_type=F32, precision=HIGHEST) + b_ref[...]
    lane = lax.broadcasted_iota(jnp.int32, logits.shape, 1)
    neg = -jnp.inf
    big = 1 << 20
    gl = jnp.where(lane < 8, logits, neg)
    gmax = jnp.max(gl, axis=1, keepdims=True)
    gidx = jnp.min(jnp.where(gl == gmax, lane, big), axis=1, keepdims=True)
    gprob = 1.0 / jnp.sum(jnp.exp(gl - gmax), axis=1, keepdims=True)
    in_group = (lane >= 8) & (lane < 8 + N_EXPERTS) & (((lane - 8) >> 3) == gidx)
    el = jnp.where(in_group, logits, neg)
    v1 = jnp.max(el, axis=1, keepdims=True)
    i1 = jnp.min(jnp.where(el == v1, lane, big), axis=1, keepdims=True)
    el2 = jnp.where(lane == i1, neg, el)
    v2 = jnp.max(el2, axis=1, keepdims=True)
    i2 = jnp.min(jnp.where(el2 == v2, lane, big), axis=1, keepdims=True)
    e2 = jnp.exp(v2 - v1)
    w1 = gprob / (1.0 + e2)
    w2 = gprob * e2 / (1.0 + e2)
    ids_ref[...] = jnp.where(lane == 0, i1 - 8, jnp.where(lane == 1, i2 - 8, 0))
    wts_ref[...] = jnp.where(lane == 0, w1, jnp.where(lane == 1, w2, 0.0))


def _router(x1, w_r, b_r):
    return pl.pallas_call(
        _router_body,
        grid=(N_TOK // OUT_TM,),
        in_specs=[
            pl.BlockSpec((OUT_TM, D_MODEL), lambda i: (i, 0)),
            pl.BlockSpec((D_MODEL, 128), lambda i: (0, 0)),
            pl.BlockSpec((1, 128), lambda i: (0, 0)),
        ],
        out_specs=[pl.BlockSpec((OUT_TM, 128), lambda i: (i, 0)),
                   pl.BlockSpec((OUT_TM, 128), lambda i: (i, 0))],
        out_shape=[jax.ShapeDtypeStruct((N_TOK, 128), jnp.int32),
                   jax.ShapeDtypeStruct((N_TOK, 128), F32)],
        compiler_params=_cp(("arbitrary",)),
        name="router",
    )(x1, w_r, b_r)


N_ASSIGN = 2 * N_TOK
MOE_NBLOCKS = (N_ASSIGN + N_EXPERTS * (MOE_BLOCK - 1) + MOE_BLOCK - 1) // MOE_BLOCK
MOE_ROWS = MOE_NBLOCKS * MOE_BLOCK
MOE_HC = MOE_HIDDEN // MOE_HSPLIT


def _row_copy(src_hbm, row, dst, dst_row, sem):
    return pltpu.make_async_copy(src_hbm.at[pl.ds(row, 1), :], dst.at[pl.ds(dst_row, 1), :], sem)


def _moe_body(bexp_ref, tok_ref, nblk_ref, x_hbm, wg_ref, wu_ref, wd_ref, o_ref, xbuf, sem):
    del bexp_ref
    i = pl.program_id(0)
    h = pl.program_id(1)
    valid = i < nblk_ref[0]

    @pl.when(valid & (h == 0))
    def _():
        def issue(r, c):
            _row_copy(x_hbm, tok_ref[i * MOE_BLOCK + r], xbuf, r, sem).start()
            return c

        lax.fori_loop(0, MOE_BLOCK, issue, 0)

        def wait(r, c):
            _row_copy(x_hbm, 0, xbuf, r, sem).wait()
            return c

        lax.fori_loop(0, MOE_BLOCK, wait, 0)

    @pl.when(valid)
    def _():
        xb = xbuf[...].astype(BF16)
        hg = jnp.dot(xb, wg_ref[0].astype(BF16), preferred_element_type=F32)
        hu = jnp.dot(xb, wu_ref[0].astype(BF16), preferred_element_type=F32)
        hh = (hg * _sigmoid(hg) * hu).astype(BF16)
        part = jnp.dot(hh, wd_ref[0].astype(BF16), preferred_element_type=F32)

        @pl.when(h == 0)
        def _():
            o_ref[...] = part

        @pl.when(h > 0)
        def _():
            o_ref[...] += part

    @pl.when(jnp.logical_not(valid) & (h == 0))
    def _():
        o_ref[...] = jnp.zeros_like(o_ref)


def _moe_experts(x1, wg, wu, wd, bexp, row_tok, nblk):
    def hsel(i, h):
        return jnp.where(i % 2 == 0, h, MOE_HSPLIT - 1 - h)

    grid_spec = pltpu.PrefetchScalarGridSpec(
        num_scalar_prefetch=3,
        grid=(MOE_NBLOCKS, MOE_HSPLIT),
        in_specs=[
            pl.BlockSpec(memory_space=pl.ANY),
            pl.BlockSpec((1, D_MODEL, MOE_HC), lambda i, h, be, rt, nb: (be[i], 0, hsel(i, h))),
            pl.BlockSpec((1, D_MODEL, MOE_HC), lambda i, h, be, rt, nb: (be[i], 0, hsel(i, h))),
            pl.BlockSpec((1, MOE_HC, D_MODEL), lambda i, h, be, rt, nb: (be[i], hsel(i, h), 0)),
        ],
        out_specs=pl.BlockSpec((MOE_BLOCK, D_MODEL), lambda i, h, be, rt, nb: (i, 0)),
        scratch_shapes=[pltpu.VMEM((MOE_BLOCK, D_MODEL), F32), pltpu.SemaphoreType.DMA(())],
    )
    return pl.pallas_call(
        _moe_body,
        grid_spec=grid_spec,
        out_shape=jax.ShapeDtypeStruct((MOE_ROWS, D_MODEL), F32),
        compiler_params=_cp(("arbitrary", "arbitrary")),
        name="moe_experts",
    )(bexp, row_tok, nblk, x1, wg, wu, wd)


CMB_TM = 128
CMB_PROMPT_TILES = N_PROMPT // CMB_TM


def _combine_body(pos_ref, eo_hbm, wts_ref, x1_ref, g_ref, b_ref, yp_ref, ys_ref, buf, sem):
    i = pl.program_id(0)

    def issue(r, c):
        a = 2 * (i * CMB_TM + r)
        _row_copy(eo_hbm, pos_ref[a], buf.at[0], r, sem).start()
        _row_copy(eo_hbm, pos_ref[a + 1], buf.at[1], r, sem).start()
        return c

    lax.fori_loop(0, CMB_TM, issue, 0)

    def wait(r, c):
        _row_copy(eo_hbm, 0, buf.at[0], r, sem).wait()
        _row_copy(eo_hbm, 0, buf.at[1], r, sem).wait()
        return c

    lax.fori_loop(0, CMB_TM, wait, 0)
    w = wts_ref[...]
    y = w[:, 0:1] * buf[0] + w[:, 1:2] * buf[1]
    x2 = _layer_norm(ALPHA * x1_ref[...] + y) * g_ref[...] + b_ref[...]

    @pl.when(i < CMB_PROMPT_TILES)
    def _():
        yp_ref[...] = x2

    @pl.when(i >= CMB_PROMPT_TILES)
    def _():
        ys_ref[...] = x2


def _combine(pos, eo, wts, x1, g, b):
    grid_spec = pltpu.PrefetchScalarGridSpec(
        num_scalar_prefetch=1,
        grid=(N_TOK // CMB_TM,),
        in_specs=[
            pl.BlockSpec(memory_space=pl.ANY),
            pl.BlockSpec((CMB_TM, 128), lambda i, p: (i, 0)),
            pl.BlockSpec((CMB_TM, D_MODEL), lambda i, p: (i, 0)),
            pl.BlockSpec((1, D_MODEL), lambda i, p: (0, 0)),
            pl.BlockSpec((1, D_MODEL), lambda i, p: (0, 0)),
        ],
        out_specs=[
            pl.BlockSpec((CMB_TM, D_MODEL), lambda i, p: (jnp.minimum(i, CMB_PROMPT_TILES - 1), 0)),
            pl.BlockSpec((CMB_TM, D_MODEL), lambda i, p: (0, 0)),
        ],
        scratch_shapes=[pltpu.VMEM((2, CMB_TM, D_MODEL), F32), pltpu.SemaphoreType.DMA(())],
    )
    return pl.pallas_call(
        _combine_body,
        grid_spec=grid_spec,
        out_shape=[jax.ShapeDtypeStruct((N_PROMPT, D_MODEL), F32),
                   jax.ShapeDtypeStruct((N_SAMPLE, D_MODEL), F32)],
        compiler_params=_cp(("arbitrary",)),
        name="combine_ln2",
    )(pos, eo, wts, x1, g, b)


def _route_positions(ids):
    eid = ids[:, :2].reshape(-1)
    onehot = (eid[:, None] == jnp.arange(N_EXPERTS, dtype=jnp.int32)[None, :]).astype(jnp.int32)
    csum = jnp.cumsum(onehot, axis=0)
    rank = jnp.take_along_axis(csum, eid[:, None], axis=1)[:, 0] - 1
    counts = csum[-1]
    padded = (counts + MOE_BLOCK - 1) // MOE_BLOCK * MOE_BLOCK
    pend = jnp.cumsum(padded)
    pos = (pend - padded)[eid] + rank
    row_tok = jnp.zeros((MOE_ROWS,), jnp.int32).at[pos].set(jnp.arange(N_ASSIGN, dtype=jnp.int32) // 2)
    nblk = pend[-1] // MOE_BLOCK
    blocks = jnp.arange(MOE_NBLOCKS, dtype=jnp.int32)
    bexp = jnp.minimum(jnp.searchsorted(pend, blocks * MOE_BLOCK, side='right'), N_EXPERTS - 1).astype(jnp.int32)
    bexp = jnp.where(blocks < nblk, bexp, bexp[jnp.maximum(nblk - 1, 0)])
    return pos.astype(jnp.int32), row_tok, bexp, nblk.reshape(1).astype(jnp.int32)


def kernel(x_prompt, x_sample, state_gla, state_s5_re, state_s5_im, w_in, w_gla_gate_up, b_gla_gate_up, w_gla_norm, s5_a_re, s5_a_im, s5_b_re, s5_b_im, s5_c_re, s5_c_im, s5_d, s5_log_dt, w_s5_glu, b_s5_glu, w_branch, w_out, ln1_g, ln1_b, w_router_group, b_router_group, w_router_expert, b_router_expert, w_moe_gate, w_moe_up, w_moe_down, ln2_g, ln2_b):
    x_all = jnp.concatenate([x_prompt.reshape(N_PROMPT, D_MODEL), x_sample.reshape(N_SAMPLE, D_MODEL)], axis=0)
    x_bf = x_all.astype(BF16)

    qkvr = _mm(x_bf, w_in, col0=0, ncols=COL_A, tn=TN, out_dtype=F32, name="proj_qkvr")
    w_a = jnp.pad(w_in[:, COL_A:COL_U], ((0, 0), (0, 128 - GLA_RANK)))
    a_low = _mm(x_bf, w_a, col0=0, ncols=128, tn=128, out_dtype=F32, name="proj_a")
    w_ug = w_in[:, COL_U:]
    u = _mm(x_bf, w_ug, col0=0, ncols=BRANCH, tn=TN, out_dtype=F32, name="proj_u")
    gates = _mm(x_bf, w_ug, col0=BRANCH, ncols=2 * D_MODEL, tn=TN, out_dtype=F32, name="proj_gates",
                epilogue=_sigmoid)

    wgu = jnp.pad(w_gla_gate_up, ((0, 128 - GLA_RANK), (0, 0)))
    bgu = b_gla_gate_up.reshape(1, GLA_DK)
    wn = w_gla_norm.reshape(1, BRANCH)
    o_p, gla_p_t = _gla_prompt(qkvr, a_low, wgu, bgu, wn)
    o_s, gla_s = _gla_sample(qkvr, a_low, wgu, bgu, wn, state_gla)
    o_all = jnp.concatenate([o_p, o_s], axis=0)
    gla_p = jnp.swapaxes(gla_p_t, 2, 3)

    s5w = _s5_weights(s5_a_re, s5_a_im, s5_b_re, s5_b_im, s5_c_re, s5_c_im, s5_d, s5_log_dt)
    y, s5_re_p, s5_im_p, s5_re_s, s5_im_s = _s5(
        u, s5w, state_s5_re.reshape(N_SAMPLE, -1), state_s5_im.reshape(N_SAMPLE, -1))
    z = _mm(y, w_s5_glu, col0=0, ncols=BRANCH, tn=TN, out_dtype=BF16, name="s5_glu",
            epilogue=lambda acc, yt, bt: yt * _sigmoid(acc + bt),
            extra=(y, b_s5_glu.reshape(1, BRANCH)),
            extra_specs=(pl.BlockSpec((TM, TN), lambda i, j: (i, j)), pl.BlockSpec((1, TN), lambda i, j: (0, j))))

    pre = _merge(o_all, z, w_branch, gates)
    x1 = _out_ln(pre, w_out.astype(BF16), x_all, ln1_g.reshape(1, D_MODEL), ln1_b.reshape(1, D_MODEL))

    w_r = jnp.concatenate([w_router_group,
                           jnp.moveaxis(w_router_expert, 0, 1).reshape(D_MODEL, N_EXPERTS),
                           jnp.zeros((D_MODEL, 128 - 8 - N_EXPERTS), F32)], axis=1)
    b_r = jnp.concatenate([b_router_group, b_router_expert.reshape(-1),
                           jnp.zeros((128 - 8 - N_EXPERTS,), F32)]).reshape(1, 128)
    ids, wts = _router(x1, w_r, b_r)
    pos, row_tok, bexp, nblk = _route_positions(ids)
    eo = _moe_experts(x1,
                      w_moe_gate.reshape(N_EXPERTS, D_MODEL, MOE_HIDDEN),
                      w_moe_up.reshape(N_EXPERTS, D_MODEL, MOE_HIDDEN),
                      w_moe_down.reshape(N_EXPERTS, MOE_HIDDEN, D_MODEL),
                      bexp, row_tok, nblk)
    y_p, y_s = _combine(pos, eo, wts, x1, ln2_g.reshape(1, D_MODEL), ln2_b.reshape(1, D_MODEL))

    return (y_p.reshape(N_PROMPT_SEQ, SEQ, D_MODEL), y_s.reshape(N_SAMPLE, 1, D_MODEL),
            gla_p,
            s5_re_p.reshape(N_PROMPT_SEQ, S5_GROUPS, S5_STATE), s5_im_p.reshape(N_PROMPT_SEQ, S5_GROUPS, S5_STATE),
            gla_s,
            s5_re_s.reshape(N_SAMPLE, S5_GROUPS, S5_STATE), s5_im_s.reshape(N_SAMPLE, S5_GROUPS, S5_STATE))
```

```python
import functools
import math

import jax
import jax.numpy as jnp
import numpy as np
from jax import lax
from jax.experimental import pallas as pl
from jax.experimental.pallas import tpu as pltpu

F32 = jnp.float32
BF16 = jnp.bfloat16
HIGHEST = lax.Precision.HIGHEST

D_MODEL = 4096
N_PROMPT_SEQ = 4
SEQ = 2048
N_SAMPLE = 128
N_PROMPT = N_PROMPT_SEQ * SEQ
N_TOK = N_PROMPT + N_SAMPLE
BRANCH = D_MODEL // 2
GLA_HEADS = 4
GLA_DK = D_MODEL // 4
GLA_HDK = GLA_DK // GLA_HEADS
GLA_HDV = BRANCH // GLA_HEADS
GLA_RANK = 16
GLA_TAU = 16.0
GLA_CHUNK = 64
GLA_LEVELS = 6
S5_GROUP = 16
S5_GROUPS = BRANCH // S5_GROUP
S5_STATE = 64
S5_CHUNK = 8
S5_LANE_GROUPS = 8
S5_BLOCKS = S5_GROUPS // S5_LANE_GROUPS
N_EXPERTS = 64
MOE_HIDDEN = D_MODEL // 8
MOE_BLOCK = 128
MOE_HSPLIT = 2
LN_EPS = 1e-5
ALPHA = 2.0 ** 0.25
COL_A = 6144
COL_U = 6160
COL_G = 8208
IN_COLS = 16400

VMEM_LIMIT = 56 * 1024 * 1024
TM = 1040
TN = 512


_NT = (((1,), (1,)), ((), ()))
_TN = (((0,), (0,)), ((), ()))


def _cp(sem):
    return pltpu.CompilerParams(dimension_semantics=sem, vmem_limit_bytes=VMEM_LIMIT)


def _sigmoid(x):
    return 1.0 / (1.0 + jnp.exp(-x))


def _log_sigmoid(z):
    return jnp.minimum(z, 0.0) - jnp.log1p(jnp.exp(-jnp.abs(z)))


def _gelu_tanh(x):
    return 0.5 * x * (1.0 + jnp.tanh(math.sqrt(2.0 / math.pi) * (x + 0.044715 * (x * x * x))))


def _layer_norm(x):
    mu = jnp.mean(x, axis=-1, keepdims=True)
    xc = x - mu
    var = jnp.mean(xc * xc, axis=-1, keepdims=True)
    return xc * lax.rsqrt(var + LN_EPS)


def _mm_body(a_ref, w_ref, *rest, epilogue):
    *extra, o_ref = rest
    acc = jnp.dot(a_ref[...].astype(BF16), w_ref[...].astype(BF16), preferred_element_type=F32)
    o_ref[...] = epilogue(acc, *[e[...] for e in extra]).astype(o_ref.dtype)


def _mm(a, w, *, col0, ncols, tn, out_dtype, name, epilogue=lambda acc: acc, extra=(), extra_specs=()):
    m, k = a.shape
    cb0 = col0 // tn
    return pl.pallas_call(
        functools.partial(_mm_body, epilogue=epilogue),
        grid=(m // TM, ncols // tn),
        in_specs=[pl.BlockSpec((TM, k), lambda i, j: (i, 0)),
                  pl.BlockSpec((k, tn), lambda i, j: (0, cb0 + j)),
                  *extra_specs],
        out_specs=pl.BlockSpec((TM, tn), lambda i, j: (i, j)),
        out_shape=jax.ShapeDtypeStruct((m, ncols), out_dtype),
        compiler_params=_cp(("arbitrary", "arbitrary")),
        name=name,
    )(a, w, *extra)


def _mm_t_body(a_ref, wt_ref, o_ref, *, epilogue):
    acc = lax.dot_general(a_ref[...], wt_ref[...].astype(BF16), _NT, preferred_element_type=F32)
    o_ref[...] = epilogue(acc).astype(o_ref.dtype)


def _mm_t(a, wt, *, row0, ncols, tn, out_dtype, name, epilogue=lambda acc: acc):
    m, k = a.shape
    return pl.pallas_call(
        functools.partial(_mm_t_body, epilogue=epilogue),
        grid=(m // TM, ncols // tn),
        in_specs=[pl.BlockSpec((TM, k), lambda i, j: (i, 0)),
                  pl.BlockSpec((pl.Element(tn), pl.Element(k)),
                               lambda i, j: (pl.multiple_of(row0 + j * tn, 8), 0))],
        out_specs=pl.BlockSpec((TM, tn), lambda i, j: (i, j)),
        out_shape=jax.ShapeDtypeStruct((m, ncols), out_dtype),
        compiler_params=_cp(("arbitrary", "arbitrary")),
        name=name,
    )(a, wt)


def _gla_coeff_matrix():
    c = GLA_CHUNK
    t = np.arange(c)[:, None]
    u = np.arange(c)[None, :]
    blocks = [(u <= t), (u > t)]
    for lvl in range(1, GLA_LEVELS + 1):
        m = 1 << lvl
        half = m // 2
        mid = (t // m) * m + half - 1
        lower = (t % m) >= half
        blocks.append(np.where(lower, (u > mid) & (u <= t), (u > t) & (u <= mid)))
    return np.concatenate(blocks, axis=0).astype(np.float32)


def _gla_level_masks():
    c = GLA_CHUNK
    t = lax.broadcasted_iota(jnp.int32, (c, c), 0)
    s = lax.broadcasted_iota(jnp.int32, (c, c), 1)
    masks = []
    for lvl in range(1, GLA_LEVELS + 1):
        m = 1 << lvl
        half = m // 2
        masks.append(((t >> lvl) == (s >> lvl)) & ((t & (m - 1)) >= half) & ((s & (m - 1)) < half))
    return masks


def _gla_out_norm(o, r, wn):
    return _layer_norm(o) * wn * (r * _sigmoid(r))


def _gla_prompt_body(q_ref, k_ref, v_ref, r_ref, a_ref, wgu_ref, bgu_ref, wn_ref, cm_ref,
                     o_ref, st_ref, s_scr, *, n_sub):
    c = GLA_CHUNK

    @pl.when(pl.program_id(2) == 0)
    def _():
        s_scr[...] = jnp.zeros_like(s_scr)

    masks = _gla_level_masks()
    wgu = wgu_ref[...]
    bgu = bgu_ref[...]
    wn = wn_ref[...]
    cm = cm_ref[...]
    nt, tn = _NT, _TN

    def chunk(ci, carry):
        rows = pl.ds(pl.multiple_of(ci * c, c), c)
        q = q_ref[rows, :] * (GLA_HDK ** -0.5)
        k = k_ref[rows, :]
        v = v_ref[rows, :]
        z = jnp.dot(a_ref[rows, :], wgu, preferred_element_type=F32, precision=HIGHEST) + bgu
        g = _log_sigmoid(z) * (1.0 / GLA_TAU)
        f = jnp.exp(jnp.dot(cm, g, preferred_element_type=F32, precision=HIGHEST))
        st = s_scr[...]
        vb = v.astype(BF16)
        o = lax.dot_general((q * f[0:c]).astype(BF16), st.astype(BF16), nt, preferred_element_type=F32)
        scores = jnp.zeros((c, c), F32)
        for lvl in range(GLA_LEVELS):
            fl = f[(2 + lvl) * c:(3 + lvl) * c]
            p = lax.dot_general((q * fl).astype(BF16), (k * fl).astype(BF16), nt, preferred_element_type=F32)
            scores = scores + jnp.where(masks[lvl], p, 0.0)
        diag = jnp.sum(q * k, axis=1, keepdims=True)
        o = o + jnp.dot(scores.astype(BF16), vb, preferred_element_type=F32) + diag * v
        kd = (k * f[c:2 * c]).astype(BF16)
        s_scr[...] = st * f[c - 1:c, :] + lax.dot_general(vb, kd, tn, preferred_element_type=F32)
        o_ref[rows, :] = _gla_out_norm(o, r_ref[rows, :], wn).astype(o_ref.dtype)
        return carry

    lax.fori_loop(0, n_sub, chunk, 0)

    @pl.when(pl.program_id(2) == pl.num_programs(2) - 1)
    def _():
        st_ref[0, 0] = s_scr[...]


def _gla_prompt(qkvr, a_low, wgu, bgu, wn):
    n_sub = 4
    tt = GLA_CHUNK * n_sub
    nt_steps = SEQ // tt
    cm = jnp.asarray(_gla_coeff_matrix())
    rows = lambda b, h, c: b * nt_steps + c
    o, st = pl.pallas_call(
        functools.partial(_gla_prompt_body, n_sub=n_sub),
        grid=(N_PROMPT_SEQ, GLA_HEADS, nt_steps),
        in_specs=[
            pl.BlockSpec((tt, GLA_HDK), lambda b, h, c: (rows(b, h, c), h)),
            pl.BlockSpec((tt, GLA_HDK), lambda b, h, c: (rows(b, h, c), GLA_HEADS + h)),
            pl.BlockSpec((tt, GLA_HDV), lambda b, h, c: (rows(b, h, c), GLA_HEADS + h)),
            pl.BlockSpec((tt, GLA_HDV), lambda b, h, c: (rows(b, h, c), 2 * GLA_HEADS + h)),
            pl.BlockSpec((tt, 128), lambda b, h, c: (rows(b, h, c), 0)),
            pl.BlockSpec((128, GLA_HDK), lambda b, h, c: (0, h)),
            pl.BlockSpec((1, GLA_HDK), lambda b, h, c: (0, h)),
            pl.BlockSpec((1, GLA_HDV), lambda b, h, c: (0, h)),
            pl.BlockSpec(cm.shape, lambda b, h, c: (0, 0)),
        ],
        out_specs=[
            pl.BlockSpec((tt, GLA_HDV), lambda b, h, c: (rows(b, h, c), h)),
            pl.BlockSpec((1, 1, GLA_HDV, GLA_HDK), lambda b, h, c: (b, h, 0, 0)),
        ],
        out_shape=[
            jax.ShapeDtypeStruct((N_PROMPT, BRANCH), BF16),
            jax.ShapeDtypeStruct((N_PROMPT_SEQ, GLA_HEADS, GLA_HDV, GLA_HDK), F32),
        ],
        scratch_shapes=[pltpu.VMEM((GLA_HDV, GLA_HDK), F32)],
        compiler_params=_cp(("arbitrary", "arbitrary", "arbitrary")),
        name="gla_prompt",
    )(qkvr, qkvr, qkvr, qkvr, a_low, wgu, bgu, wn, cm)
    return o, st


GLA_SB = 16


def _gla_sample_body(q_ref, k_ref, v_ref, r_ref, a_ref, wgu_ref, bgu_ref, wn_ref, s_ref, o_ref, so_ref):
    q = q_ref[...] * (GLA_HDK ** -0.5)
    k = k_ref[...]
    v = v_ref[...]
    z = jnp.dot(a_ref[...], wgu_ref[...], preferred_element_type=F32, precision=HIGHEST) + bgu_ref[...]
    eg = jnp.exp(_log_sigmoid(z) * (1.0 / GLA_TAU))
    qe = (q * eg).astype(BF16)
    eg_t = eg.T
    k_t = k.T
    rows = []
    for n in range(GLA_SB):
        s0 = s_ref[n, 0]
        rows.append(jnp.dot(qe, s0.astype(BF16), preferred_element_type=F32)[n:n + 1])
        so_ref[n, 0] = s0 * eg_t[:, n:n + 1] + k_t[:, n:n + 1] * v[n:n + 1, :]
    o = jnp.concatenate(rows, axis=0) + jnp.sum(q * k, axis=1, keepdims=True) * v
    o_ref[...] = _gla_out_norm(o, r_ref[...], wn_ref[...]).astype(o_ref.dtype)


def _gla_sample(qkvr, a_low, wgu, bgu, wn, state):
    r0 = N_PROMPT // GLA_SB
    o, st = pl.pallas_call(
        _gla_sample_body,
        grid=(GLA_HEADS, N_SAMPLE // GLA_SB),
        in_specs=[
            pl.BlockSpec((GLA_SB, GLA_HDK), lambda h, i: (r0 + i, h)),
            pl.BlockSpec((GLA_SB, GLA_HDK), lambda h, i: (r0 + i, GLA_HEADS + h)),
            pl.BlockSpec((GLA_SB, GLA_HDV), lambda h, i: (r0 + i, GLA_HEADS + h)),
            pl.BlockSpec((GLA_SB, GLA_HDV), lambda h, i: (r0 + i, 2 * GLA_HEADS + h)),
            pl.BlockSpec((GLA_SB, 128), lambda h, i: (r0 + i, 0)),
            pl.BlockSpec((128, GLA_HDK), lambda h, i: (0, h)),
            pl.BlockSpec((1, GLA_HDK), lambda h, i: (0, h)),
            pl.BlockSpec((1, GLA_HDV), lambda h, i: (0, h)),
            pl.BlockSpec((GLA_SB, 1, GLA_HDK, GLA_HDV), lambda h, i: (i, h, 0, 0)),
        ],
        out_specs=[
            pl.BlockSpec((GLA_SB, GLA_HDV), lambda h, i: (i, h)),
            pl.BlockSpec((GLA_SB, 1, GLA_HDK, GLA_HDV), lambda h, i: (i, h, 0, 0)),
        ],
        out_shape=[
            jax.ShapeDtypeStruct((N_SAMPLE, BRANCH), BF16),
            jax.ShapeDtypeStruct(state.shape, F32),
        ],
        compiler_params=_cp(("arbitrary", "arbitrary")),
        name="gla_sample",
    )(qkvr, qkvr, qkvr, qkvr, a_low, wgu, bgu, wn, state)
    return o, st


def _s5_weights(a_re, a_im, b_re, b_im, c_re, c_im, d, log_dt):
    L = S5_CHUNK
    lam_re = jnp.minimum(a_re, -1e-4)
    lam_im = a_im
    dt = jnp.exp(log_dt)[:, None]
    kk = jnp.arange(L + 1, dtype=F32)[:, None, None]
    pow_re = jnp.exp(lam_re * dt * kk) * jnp.cos(lam_im * dt * kk)
    pow_im = jnp.exp(lam_re * dt * kk) * jnp.sin(lam_im * dt * kk)
    lbar_re, lbar_im = pow_re[1], pow_im[1]
    den = lam_re * lam_re + lam_im * lam_im
    f_re = ((lbar_re - 1.0) * lam_re + lbar_im * lam_im) / den
    f_im = (lbar_im * lam_re - (lbar_re - 1.0) * lam_im) / den
    bb_re = f_re[..., None] * b_re - f_im[..., None] * b_im
    bb_im = f_re[..., None] * b_im + f_im[..., None] * b_re
    nb, ng = S5_BLOCKS, S5_LANE_GROUPS
    bb_re_t = jnp.swapaxes(bb_re, 1, 2)
    bb_im_t = jnp.swapaxes(bb_im, 1, 2)
    lb_re = (pow_re[:L, :, None, :] * bb_re_t - pow_im[:L, :, None, :] * bb_im_t).reshape(L, nb, 128, S5_STATE)
    lb_im = (pow_re[:L, :, None, :] * bb_im_t + pow_im[:L, :, None, :] * bb_re_t).reshape(L, nb, 128, S5_STATE)
    cl_re = (c_re[None] * pow_re[:, :, None, :] - c_im[None] * pow_im[:, :, None, :]).reshape(L + 1, nb, 128, S5_STATE)
    cl_im = (c_re[None] * pow_im[:, :, None, :] + c_im[None] * pow_re[:, :, None, :]).reshape(L + 1, nb, 128, S5_STATE)
    tile_p = jnp.tile(jnp.eye(S5_STATE, dtype=F32), (1, ng))
    same_group = jnp.kron(jnp.eye(ng, dtype=F32), jnp.ones((S5_GROUP, S5_STATE), F32))

    def expand(x):
        return jnp.einsum('...rp,pc->...rc', x, tile_p, precision=HIGHEST) * same_group

    zpow = jnp.concatenate([expand(lb_re), expand(lb_im)], axis=-1)
    cpow = jnp.concatenate([expand(cl_re), expand(-cl_im)], axis=-1)
    wz = jnp.moveaxis(zpow[::-1], 0, 1).reshape(nb, L * 128, 2 * ng * S5_STATE)
    wct = jnp.moveaxis(cpow[1:], 0, 1).reshape(nb, L * 128, 2 * ng * S5_STATE)
    bd = jnp.einsum('kjrc,jqc->kjrq', zpow, cpow[0], precision=HIGHEST)
    sp = jnp.arange(L)[:, None]
    s = jnp.arange(L)[None, :]
    lag = s - sp
    toe = jnp.where((lag >= 0)[:, :, None, None, None], bd[jnp.clip(lag, 0)], 0.0)
    t_mat = jnp.transpose(toe, (2, 0, 3, 1, 4)).reshape(nb, L * 128, L * 128)
    n_steps = int(math.log2(SEQ // L))
    mult = (L * (2.0 ** jnp.arange(n_steps, dtype=F32)))[:, None, None]
    sc_re = jnp.exp(lam_re * dt * mult) * jnp.cos(lam_im * dt * mult)
    sc_im = jnp.exp(lam_re * dt * mult) * jnp.sin(lam_im * dt * mult)

    def state_lanes(x):
        lead = x.shape[:-2]
        return jnp.moveaxis(x.reshape(lead + (nb, ng * S5_STATE)), -2, 0)

    scan_mult = jnp.concatenate([state_lanes(sc_re), state_lanes(sc_im)], axis=-1)
    lbar1 = jnp.concatenate([state_lanes(lbar_re[None]), state_lanes(lbar_im[None])], axis=-1)
    dvec = jnp.tile(d.reshape(nb, 1, 128), (1, 1, L))
    return t_mat.astype(BF16), wz.astype(BF16), wct.astype(BF16), scan_mult, lbar1, dvec


def _s5_prompt_body(u_ref, t_ref, wz_ref, wc_ref, sm_ref, d_ref, y_ref, fre_ref, fim_ref, *, n_steps):
    L = S5_CHUNK
    n_rows = SEQ // L
    ns = S5_LANE_GROUPS * S5_STATE
    v = jnp.concatenate([u_ref[pl.ds(s, n_rows, stride=L), :] for s in range(L)], axis=1)
    vb = v.astype(BF16)
    z = jnp.dot(vb, wz_ref[0], preferred_element_type=F32)
    hr, hi = z[:, :ns], z[:, ns:]
    pos = lax.broadcasted_iota(jnp.int32, (n_rows, ns), 0)
    sm = sm_ref[0]
    for d in range(n_steps):
        sh = 1 << d
        ar, ai = sm[d:d + 1, :ns], sm[d:d + 1, ns:]
        keep = pos >= sh
        pr = jnp.where(keep, pltpu.roll(hr, sh, 0), 0.0)
        pi = jnp.where(keep, pltpu.roll(hi, sh, 0), 0.0)
        hr, hi = hr + ar * pr - ai * pi, hi + ar * pi + ai * pr
    fre_ref[0] = hr[n_rows - 1:n_rows]
    fim_ref[0] = hi[n_rows - 1:n_rows]
    first = pos >= 1
    h_prev = jnp.concatenate([jnp.where(first, pltpu.roll(hr, 1, 0), 0.0),
                              jnp.where(first, pltpu.roll(hi, 1, 0), 0.0)], axis=1)
    y = (jnp.dot(vb, t_ref[0], preferred_element_type=F32)
         + lax.dot_general(h_prev.astype(BF16), wc_ref[0], _NT, preferred_element_type=F32)
         + d_ref[0] * v)
    y = _gelu_tanh(y)
    for s in range(L):
        y_ref[pl.ds(s, n_rows, stride=L), :] = y[:, s * 128:(s + 1) * 128]


def _s5_sample_body(u_ref, k0_ref, wb_ref, wc0_ref, l1_ref, d_ref, hre_ref, him_ref,
                    y_ref, sre_ref, sim_ref):
    ns = S5_LANE_GROUPS * S5_STATE
    us = u_ref[...]
    usb = us.astype(BF16)
    h0r, h0i = hre_ref[...], him_ref[...]
    l1 = l1_ref[0]
    bu = jnp.dot(usb, wb_ref[0], preferred_element_type=F32)
    sre_ref[...] = l1[:, :ns] * h0r - l1[:, ns:] * h0i + bu[:, :ns]
    sim_ref[...] = l1[:, :ns] * h0i + l1[:, ns:] * h0r + bu[:, ns:]
    h0 = jnp.concatenate([h0r, h0i], axis=1).astype(BF16)
    ys = (jnp.dot(usb, k0_ref[0], preferred_element_type=F32)
          + lax.dot_general(h0, wc0_ref[0], _NT, preferred_element_type=F32)
          + d_ref[0] * us)
    y_ref[...] = _gelu_tanh(ys)


def _s5(u, weights, st_re, st_im):
    t_mat, wz, wc, scan_mult, lbar1, dvec = weights
    L = S5_CHUNK
    ns = S5_LANE_GROUPS * S5_STATE
    n_steps = scan_mult.shape[1]
    wblk = lambda a: pl.BlockSpec((1,) + a.shape[1:], lambda j, b: (j, 0, 0))
    y, f_re, f_im = pl.pallas_call(
        functools.partial(_s5_prompt_body, n_steps=n_steps),
        grid=(S5_BLOCKS, N_PROMPT_SEQ),
        in_specs=[
            pl.BlockSpec((SEQ, 128), lambda j, b: (b, j)),
            wblk(t_mat), wblk(wz), wblk(wc), wblk(scan_mult), wblk(dvec),
        ],
        out_specs=[
            pl.BlockSpec((SEQ, 128), lambda j, b: (b, j)),
            pl.BlockSpec((1, 1, ns), lambda j, b: (b, 0, j)),
            pl.BlockSpec((1, 1, ns), lambda j, b: (b, 0, j)),
        ],
        out_shape=[
            jax.ShapeDtypeStruct((N_PROMPT, BRANCH), F32),
            jax.ShapeDtypeStruct((N_PROMPT_SEQ, 1, S5_GROUPS * S5_STATE), F32),
            jax.ShapeDtypeStruct((N_PROMPT_SEQ, 1, S5_GROUPS * S5_STATE), F32),
        ],
        compiler_params=_cp(("arbitrary", "arbitrary")),
        name="s5_prompt",
    )(u, t_mat, wz, wc, scan_mult, dvec)
    k0 = t_mat[:, 0:128, 0:128]
    wb = wz[:, (L - 1) * 128:L * 128, :]
    wc0 = wc[:, 0:128, :]
    d0 = dvec[:, :, 0:128]
    sblk = lambda a: pl.BlockSpec((1,) + a.shape[1:], lambda j: (j, 0, 0))
    r0 = N_PROMPT // N_SAMPLE
    y_s, s_re, s_im = pl.pallas_call(
        _s5_sample_body,
        grid=(S5_BLOCKS,),
        in_specs=[
            pl.BlockSpec((N_SAMPLE, 128), lambda j: (r0, j)),
            sblk(k0), sblk(wb), sblk(wc0), sblk(lbar1), sblk(d0),
            pl.BlockSpec((N_SAMPLE, ns), lambda j: (0, j)),
            pl.BlockSpec((N_SAMPLE, ns), lambda j: (0, j)),
        ],
        out_specs=[
            pl.BlockSpec((N_SAMPLE, 128), lambda j: (0, j)),
            pl.BlockSpec((N_SAMPLE, ns), lambda j: (0, j)),
            pl.BlockSpec((N_SAMPLE, ns), lambda j: (0, j)),
        ],
        out_shape=[
            jax.ShapeDtypeStruct((N_SAMPLE, BRANCH), F32),
            jax.ShapeDtypeStruct((N_SAMPLE, S5_GROUPS * S5_STATE), F32),
            jax.ShapeDtypeStruct((N_SAMPLE, S5_GROUPS * S5_STATE), F32),
        ],
        compiler_params=_cp(("arbitrary",)),
        name="s5_sample",
    )(u, k0, wb, wc0, lbar1, d0, st_re, st_im)
    return jnp.concatenate([y, y_s], axis=0), f_re, f_im, s_re, s_im


def _merge_body(o_ref, z_ref, w0_ref, w1_ref, g0_ref, g1_ref, out_ref):
    p0 = jnp.dot(o_ref[...], w0_ref[0].astype(BF16), preferred_element_type=F32)
    p1 = jnp.dot(z_ref[...], w1_ref[0].astype(BF16), preferred_element_type=F32)
    out_ref[...] = (g0_ref[...] * p0 + g1_ref[...] * p1).astype(out_ref.dtype)


def _merge(o, z, w_branch, gates):
    ncb = D_MODEL // TN
    return pl.pallas_call(
        _merge_body,
        grid=(N_TOK // TM, ncb),
        in_specs=[
            pl.BlockSpec((TM, BRANCH), lambda i, j: (i, 0)),
            pl.BlockSpec((TM, BRANCH), lambda i, j: (i, 0)),
            pl.BlockSpec((1, BRANCH, TN), lambda i, j: (0, 0, j)),
            pl.BlockSpec((1, BRANCH, TN), lambda i, j: (1, 0, j)),
            pl.BlockSpec((TM, TN), lambda i, j: (i, j)),
            pl.BlockSpec((TM, TN), lambda i, j: (i, ncb + j)),
        ],
        out_specs=pl.BlockSpec((TM, TN), lambda i, j: (i, j)),
        out_shape=jax.ShapeDtypeStruct((N_TOK, D_MODEL), BF16),
        compiler_params=_cp(("arbitrary", "arbitrary")),
        name="merge",
    )(o, z, w_branch, w_branch, gates, gates)


OUT_TM = 416
OUT_TK = 512


def _out_ln_body(a_ref, w_ref, x_ref, g_ref, b_ref, o_ref):
    kk = pl.program_id(1)
    part = jnp.dot(a_ref[...], w_ref[...], preferred_element_type=F32)

    @pl.when(kk == 0)
    def _():
        o_ref[...] = part

    @pl.when(kk > 0)
    def _():
        o_ref[...] += part

    @pl.when(kk == pl.num_programs(1) - 1)
    def _():
        o_ref[...] = _layer_norm(ALPHA * x_ref[...] + o_ref[...]) * g_ref[...] + b_ref[...]


def _out_ln(pre, w_out_bf, x_all, g, b):
    return pl.pallas_call(
        _out_ln_body,
        grid=(N_TOK // OUT_TM, D_MODEL // OUT_TK),
        in_specs=[
            pl.BlockSpec((OUT_TM, OUT_TK), lambda i, k: (i, k)),
            pl.BlockSpec((OUT_TK, D_MODEL), lambda i, k: (k, 0)),
            pl.BlockSpec((OUT_TM, D_MODEL), lambda i, k: (i, 0)),
            pl.BlockSpec((1, D_MODEL), lambda i, k: (0, 0)),
            pl.BlockSpec((1, D_MODEL), lambda i, k: (0, 0)),
        ],
        out_specs=pl.BlockSpec((OUT_TM, D_MODEL), lambda i, k: (i, 0)),
        out_shape=jax.ShapeDtypeStruct((N_TOK, D_MODEL), F32),
        compiler_params=_cp(("arbitrary", "arbitrary")),
        name="out_ln1",
    )(pre, w_out_bf, x_all, g, b)


def _router_body(x_ref, w_ref, b_ref, ids_ref, wts_ref):
    logits = jnp.dot(x_ref[...], w_ref[...], preferred_element_type=F32, precision=HIGHEST) + b_ref[...]
    lane = lax.broadcasted_iota(jnp.int32, logits.shape, 1)
    neg = -jnp.inf
    big = 1 << 20
    gl = jnp.where(lane < 8, logits, neg)
    gmax = jnp.max(gl, axis=1, keepdims=True)
    gidx = jnp.min(jnp.where(gl == gmax, lane, big), axis=1, keepdims=True)
    gprob = 1.0 / jnp.sum(jnp.exp(gl - gmax), axis=1, keepdims=True)
    in_group = (lane >= 8) & (lane < 8 + N_EXPERTS) & (((lane - 8) >> 3) == gidx)
    el = jnp.where(in_group, logits, neg)
    v1 = jnp.max(el, axis=1, keepdims=True)
    i1 = jnp.min(jnp.where(el == v1, lane, big), axis=1, keepdims=True)
    el2 = jnp.where(lane == i1, neg, el)
    v2 = jnp.max(el2, axis=1, keepdims=True)
    i2 = jnp.min(jnp.where(el2 == v2, lane, big), axis=1, keepdims=True)
    e2 = jnp.exp(v2 - v1)
    w1 = gprob / (1.0 + e2)
    w2 = gprob * e2 / (1.0 + e2)
    ids_ref[...] = jnp.where(lane == 0, i1 - 8, jnp.where(lane == 1, i2 - 8, 0))
    wts_ref[...] = jnp.where(lane == 0, w1, jnp.where(lane == 1, w2, 0.0))


def _router(x1, w_r, b_r):
    return pl.pallas_call(
        _router_body,
        grid=(N_TOK // OUT_TM,),
        in_specs=[
            pl.BlockSpec((OUT_TM, D_MODEL), lambda i: (i, 0)),
            pl.BlockSpec((D_MODEL, 128), lambda i: (0, 0)),
            pl.BlockSpec((1, 128), lambda i: (0, 0)),
        ],
        out_specs=[pl.BlockSpec((OUT_TM, 128), lambda i: (i, 0)),
                   pl.BlockSpec((OUT_TM, 128), lambda i: (i, 0))],
        out_shape=[jax.ShapeDtypeStruct((N_TOK, 128), jnp.int32),
                   jax.ShapeDtypeStruct((N_TOK, 128), F32)],
        compiler_params=_cp(("arbitrary",)),
        name="router",
    )(x1, w_r, b_r)


N_ASSIGN = 2 * N_TOK
MOE_NBLOCKS = (N_ASSIGN + N_EXPERTS * (MOE_BLOCK - 1) + MOE_BLOCK - 1) // MOE_BLOCK
MOE_ROWS = MOE_NBLOCKS * MOE_BLOCK
MOE_HC = MOE_HIDDEN // MOE_HSPLIT


def _row_copy(src_hbm, row, dst, dst_row, sem):
    return pltpu.make_async_copy(src_hbm.at[pl.ds(row, 1), :], dst.at[pl.ds(dst_row, 1), :], sem)


MOE_MACRO = 4
MOE_NMACRO = N_ASSIGN // (MOE_MACRO * MOE_BLOCK) + N_EXPERTS + 1
MOE_DCOLS = 1024


def _block_copy(acc, t, o_hbm, blk, sem):
    return pltpu.make_async_copy(acc.at[pl.ds(pl.multiple_of(t * MOE_BLOCK, MOE_BLOCK), MOE_BLOCK), :],
                                 o_hbm.at[pl.ds(pl.multiple_of(blk * MOE_BLOCK, MOE_BLOCK), MOE_BLOCK), :], sem)


def _moe_body(mexp_ref, mstart_ref, mnsub_ref, tok_ref, x_hbm, wg_ref, wu_ref, wd_ref, o_hbm,
              stage, xb, acc, gsem, osem):
    del mexp_ref
    m = pl.program_id(0)
    h = pl.program_id(1)
    nsub = mnsub_ref[m]
    start = mstart_ref[m]

    @pl.when((h == 0) & (nsub > 0))
    def _():
        def sub(t, c):
            base = (start + t) * MOE_BLOCK

            def issue(r, c2):
                _row_copy(x_hbm, tok_ref[base + r], stage, r, gsem).start()
                return c2

            lax.fori_loop(0, MOE_BLOCK, issue, 0)

            def wait(r, c2):
                _row_copy(x_hbm, 0, stage, r, gsem).wait()
                return c2

            lax.fori_loop(0, MOE_BLOCK, wait, 0)
            xb[pl.ds(pl.multiple_of(t * MOE_BLOCK, MOE_BLOCK), MOE_BLOCK), :] = stage[...].astype(BF16)
            return c

        lax.fori_loop(0, nsub, sub, 0)
        acc[...] = jnp.zeros_like(acc)

    for ns in range(1, MOE_MACRO + 1):
        @pl.when(nsub == ns)
        def _(ns=ns):
            rows = ns * MOE_BLOCK
            x = xb[0:rows, :]
            hg = jnp.dot(x, wg_ref[0].astype(BF16), preferred_element_type=F32)
            hu = jnp.dot(x, wu_ref[0].astype(BF16), preferred_element_type=F32)
            hh = (hg * _sigmoid(hg) * hu).astype(BF16)
            wd = wd_ref[0].astype(BF16)
            for cc in range(D_MODEL // MOE_DCOLS):
                cols = slice(cc * MOE_DCOLS, (cc + 1) * MOE_DCOLS)
                acc[0:rows, cols] += jnp.dot(hh, wd[:, cols], preferred_element_type=F32)

    @pl.when((h == pl.num_programs(1) - 1) & (nsub > 0))
    def _():
        def put(t, c):
            _block_copy(acc, t, o_hbm, start + t, osem).start()
            return c

        lax.fori_loop(0, nsub, put, 0)

        def done(t, c):
            _block_copy(acc, t, o_hbm, start + t, osem).wait()
            return c

        lax.fori_loop(0, nsub, done, 0)

    @pl.when((h == 0) & (nsub == 0))
    def _():
        acc[0:MOE_BLOCK, :] = jnp.zeros((MOE_BLOCK, D_MODEL), F32)
        for t in range(MOE_MACRO):
            @pl.when(start + t < MOE_NBLOCKS)
            def _(t=t):
                cp = _block_copy(acc, 0, o_hbm, start + t, osem)
                cp.start()
                cp.wait()


def _moe_experts(x1, wg, wu, wd, mexp, mstart, mnsub, row_tok):
    grid_spec = pltpu.PrefetchScalarGridSpec(
        num_scalar_prefetch=4,
        grid=(MOE_NMACRO, MOE_HSPLIT),
        in_specs=[
            pl.BlockSpec(memory_space=pl.ANY),
            pl.BlockSpec((1, D_MODEL, MOE_HC), lambda m, h, me, ms, mn, rt: (me[m], 0, h)),
            pl.BlockSpec((1, D_MODEL, MOE_HC), lambda m, h, me, ms, mn, rt: (me[m], 0, h)),
            pl.BlockSpec((1, MOE_HC, D_MODEL), lambda m, h, me, ms, mn, rt: (me[m], h, 0)),
        ],
        out_specs=pl.BlockSpec(memory_space=pl.ANY),
        scratch_shapes=[pltpu.VMEM((MOE_BLOCK, D_MODEL), F32),
                        pltpu.VMEM((MOE_MACRO * MOE_BLOCK, D_MODEL), BF16),
                        pltpu.VMEM((MOE_MACRO * MOE_BLOCK, D_MODEL), F32),
                        pltpu.SemaphoreType.DMA(()), pltpu.SemaphoreType.DMA(())],
    )
    return pl.pallas_call(
        _moe_body,
        grid_spec=grid_spec,
        out_shape=jax.ShapeDtypeStruct((MOE_ROWS, D_MODEL), F32),
        compiler_params=_cp(("arbitrary", "arbitrary")),
        name="moe_experts",
    )(mexp, mstart, mnsub, row_tok, x1, wg, wu, wd)


CMB_TM = 128
CMB_PROMPT_TILES = N_PROMPT // CMB_TM


def _combine_body(pos_ref, eo_hbm, wts_ref, x1_ref, g_ref, b_ref, yp_ref, ys_ref, buf, sem):
    i = pl.program_id(0)

    def issue(r, c):
        a = 2 * (i * CMB_TM + r)
        _row_copy(eo_hbm, pos_ref[a], buf.at[0], r, sem).start()
        _row_copy(eo_hbm, pos_ref[a + 1], buf.at[1], r, sem).start()
        return c

    lax.fori_loop(0, CMB_TM, issue, 0)

    def wait(r, c):
        _row_copy(eo_hbm, 0, buf.at[0], r, sem).wait()
        _row_copy(eo_hbm, 0, buf.at[1], r, sem).wait()
        return c

    lax.fori_loop(0, CMB_TM, wait, 0)
    w = wts_ref[...]
    y = w[:, 0:1] * buf[0] + w[:, 1:2] * buf[1]
    x2 = _layer_norm(ALPHA * x1_ref[...] + y) * g_ref[...] + b_ref[...]

    @pl.when(i < CMB_PROMPT_TILES)
    def _():
        yp_ref[...] = x2

    @pl.when(i >= CMB_PROMPT_TILES)
    def _():
        ys_ref[...] = x2


def _combine(pos, eo, wts, x1, g, b):
    grid_spec = pltpu.PrefetchScalarGridSpec(
        num_scalar_prefetch=1,
        grid=(N_TOK // CMB_TM,),
        in_specs=[
            pl.BlockSpec(memory_space=pl.ANY),
            pl.BlockSpec((CMB_TM, 128), lambda i, p: (i, 0)),
            pl.BlockSpec((CMB_TM, D_MODEL), lambda i, p: (i, 0)),
            pl.BlockSpec((1, D_MODEL), lambda i, p: (0, 0)),
            pl.BlockSpec((1, D_MODEL), lambda i, p: (0, 0)),
        ],
        out_specs=[
            pl.BlockSpec((CMB_TM, D_MODEL), lambda i, p: (jnp.minimum(i, CMB_PROMPT_TILES - 1), 0)),
            pl.BlockSpec((CMB_TM, D_MODEL), lambda i, p: (0, 0)),
        ],
        scratch_shapes=[pltpu.VMEM((2, CMB_TM, D_MODEL), F32), pltpu.SemaphoreType.DMA(())],
    )
    return pl.pallas_call(
        _combine_body,
        grid_spec=grid_spec,
        out_shape=[jax.ShapeDtypeStruct((N_PROMPT, D_MODEL), F32),
                   jax.ShapeDtypeStruct((N_SAMPLE, D_MODEL), F32)],
        compiler_params=_cp(("arbitrary",)),
        name="combine_ln2",
    )(pos, eo, wts, x1, g, b)


def _route_positions(ids):
    eid = ids[:, :2].reshape(-1)
    onehot = (eid[:, None] == jnp.arange(N_EXPERTS, dtype=jnp.int32)[None, :]).astype(jnp.int32)
    csum = jnp.cumsum(onehot, axis=0)
    rank = jnp.take_along_axis(csum, eid[:, None], axis=1)[:, 0] - 1
    counts = csum[-1]
    nblk_e = (counts + MOE_BLOCK - 1) // MOE_BLOCK
    bend = jnp.cumsum(nblk_e)
    bstart = bend - nblk_e
    pos = bstart[eid] * MOE_BLOCK + rank
    row_tok = jnp.zeros((MOE_ROWS,), jnp.int32).at[pos].set(jnp.arange(N_ASSIGN, dtype=jnp.int32) // 2)
    nstep_e = (nblk_e + MOE_MACRO - 1) // MOE_MACRO
    send = jnp.cumsum(nstep_e)
    n_steps, n_blocks = send[-1], bend[-1]
    m = jnp.arange(MOE_NMACRO, dtype=jnp.int32)
    e_of_m = jnp.minimum(jnp.searchsorted(send, m, side='right'), N_EXPERTS - 1).astype(jnp.int32)
    local = m - (send - nstep_e)[e_of_m]
    valid = m < n_steps
    mexp = jnp.where(valid, e_of_m, e_of_m[jnp.maximum(n_steps - 1, 0)])
    mstart = jnp.where(valid, bstart[e_of_m] + MOE_MACRO * local, n_blocks + MOE_MACRO * (m - n_steps))
    mnsub = jnp.where(valid, jnp.clip(nblk_e[e_of_m] - MOE_MACRO * local, 0, MOE_MACRO), 0)
    return (pos.astype(jnp.int32), row_tok, mexp.astype(jnp.int32), mstart.astype(jnp.int32),
            mnsub.astype(jnp.int32))


def kernel(x_prompt, x_sample, state_gla, state_s5_re, state_s5_im, w_in, w_gla_gate_up, b_gla_gate_up, w_gla_norm, s5_a_re, s5_a_im, s5_b_re, s5_b_im, s5_c_re, s5_c_im, s5_d, s5_log_dt, w_s5_glu, b_s5_glu, w_branch, w_out, ln1_g, ln1_b, w_router_group, b_router_group, w_router_expert, b_router_expert, w_moe_gate, w_moe_up, w_moe_down, ln2_g, ln2_b):
    x_all = jnp.concatenate([x_prompt.reshape(N_PROMPT, D_MODEL), x_sample.reshape(N_SAMPLE, D_MODEL)], axis=0)
    x_bf = x_all.astype(BF16)

    w_in_t = w_in.T
    qkvr = _mm_t(x_bf, w_in_t, row0=0, ncols=COL_A, tn=TN, out_dtype=F32, name="proj_qkvr")
    a_low = _mm_t(x_bf, w_in_t, row0=COL_A, ncols=128, tn=128, out_dtype=F32, name="proj_a")
    u = _mm_t(x_bf, w_in_t, row0=COL_U, ncols=BRANCH, tn=TN, out_dtype=F32, name="proj_u")
    gates = _mm_t(x_bf, w_in_t, row0=COL_G, ncols=2 * D_MODEL, tn=TN, out_dtype=F32, name="proj_gates",
                  epilogue=_sigmoid)

    wgu = jnp.pad(w_gla_gate_up, ((0, 128 - GLA_RANK), (0, 0)))
    bgu = b_gla_gate_up.reshape(1, GLA_DK)
    wn = w_gla_norm.reshape(1, BRANCH)
    o_p, gla_p_t = _gla_prompt(qkvr, a_low, wgu, bgu, wn)
    o_s, gla_s = _gla_sample(qkvr, a_low, wgu, bgu, wn, state_gla)
    o_all = jnp.concatenate([o_p, o_s], axis=0)
    gla_p = jnp.swapaxes(gla_p_t, 2, 3)

    s5w = _s5_weights(s5_a_re, s5_a_im, s5_b_re, s5_b_im, s5_c_re, s5_c_im, s5_d, s5_log_dt)
    y, s5_re_p, s5_im_p, s5_re_s, s5_im_s = _s5(
        u, s5w, state_s5_re.reshape(N_SAMPLE, -1), state_s5_im.reshape(N_SAMPLE, -1))
    z = _mm(y, w_s5_glu, col0=0, ncols=BRANCH, tn=TN, out_dtype=BF16, name="s5_glu",
            epilogue=lambda acc, yt, bt: yt * _sigmoid(acc + bt),
            extra=(y, b_s5_glu.reshape(1, BRANCH)),
            extra_specs=(pl.BlockSpec((TM, TN), lambda i, j: (i, j)), pl.BlockSpec((1, TN), lambda i, j: (0, j))))

    pre = _merge(o_all, z, w_branch, gates)
    x1 = _out_ln(pre, w_out.astype(BF16), x_all, ln1_g.reshape(1, D_MODEL), ln1_b.reshape(1, D_MODEL))

    w_r = jnp.concatenate([w_router_group,
                           jnp.moveaxis(w_router_expert, 0, 1).reshape(D_MODEL, N_EXPERTS),
                           jnp.zeros((D_MODEL, 128 - 8 - N_EXPERTS), F32)], axis=1)
    b_r = jnp.concatenate([b_router_group, b_router_expert.reshape(-1),
                           jnp.zeros((128 - 8 - N_EXPERTS,), F32)]).reshape(1, 128)
    ids, wts = _router(x1, w_r, b_r)
    pos, row_tok, mexp, mstart, mnsub = _route_positions(ids)
    eo = _moe_experts(x1,
                      w_moe_gate.reshape(N_EXPERTS, D_MODEL, MOE_HIDDEN),
                      w_moe_up.reshape(N_EXPERTS, D_MODEL, MOE_HIDDEN),
                      w_moe_down.reshape(N_EXPERTS, MOE_HIDDEN, D_MODEL),
                      mexp, mstart, mnsub, row_tok)
    y_p, y_s = _combine(pos, eo, wts, x1, ln2_g.reshape(1, D_MODEL), ln2_b.reshape(1, D_MODEL))

    return (y_p.reshape(N_PROMPT_SEQ, SEQ, D_MODEL), y_s.reshape(N_SAMPLE, 1, D_MODEL),
            gla_p,
            s5_re_p.reshape(N_PROMPT_SEQ, S5_GROUPS, S5_STATE), s5_im_p.reshape(N_PROMPT_SEQ, S5_GROUPS, S5_STATE),
            gla_s,
            s5_re_s.reshape(N_SAMPLE, S5_GROUPS, S5_STATE), s5_im_s.reshape(N_SAMPLE, S5_GROUPS, S5_STATE))
```

```python
import functools
import math

import jax
import jax.numpy as jnp
import numpy as np
from jax import lax
from jax.experimental import pallas as pl
from jax.experimental.pallas import tpu as pltpu

F32 = jnp.float32
BF16 = jnp.bfloat16
HIGHEST = lax.Precision.HIGHEST

D_MODEL = 4096
N_PROMPT_SEQ = 4
SEQ = 2048
N_SAMPLE = 128
N_PROMPT = N_PROMPT_SEQ * SEQ
N_TOK = N_PROMPT + N_SAMPLE
BRANCH = D_MODEL // 2
GLA_HEADS = 4
GLA_DK = D_MODEL // 4
GLA_HDK = GLA_DK // GLA_HEADS
GLA_HDV = BRANCH // GLA_HEADS
GLA_RANK = 16
GLA_TAU = 16.0
GLA_CHUNK = 64
GLA_LEVELS = 6
S5_GROUP = 16
S5_GROUPS = BRANCH // S5_GROUP
S5_STATE = 64
S5_CHUNK = 8
S5_LANE_GROUPS = 8
S5_BLOCKS = S5_GROUPS // S5_LANE_GROUPS
N_EXPERTS = 64
MOE_HIDDEN = D_MODEL // 8
MOE_BLOCK = 128
MOE_HSPLIT = 2
LN_EPS = 1e-5
ALPHA = 2.0 ** 0.25
COL_A = 6144
COL_U = 6160
COL_G = 8208
IN_COLS = 16400

VMEM_LIMIT = 56 * 1024 * 1024
TM = 1040
TN = 512


_NT = (((1,), (1,)), ((), ()))
_TN = (((0,), (0,)), ((), ()))


def _cp(sem):
    return pltpu.CompilerParams(dimension_semantics=sem, vmem_limit_bytes=VMEM_LIMIT)


def _sigmoid(x):
    return 1.0 / (1.0 + jnp.exp(-x))


def _log_sigmoid(z):
    return jnp.minimum(z, 0.0) - jnp.log1p(jnp.exp(-jnp.abs(z)))


def _gelu_tanh(x):
    return 0.5 * x * (1.0 + jnp.tanh(math.sqrt(2.0 / math.pi) * (x + 0.044715 * (x * x * x))))


def _layer_norm(x):
    mu = jnp.mean(x, axis=-1, keepdims=True)
    xc = x - mu
    var = jnp.mean(xc * xc, axis=-1, keepdims=True)
    return xc * lax.rsqrt(var + LN_EPS)


def _mm_body(a_ref, w_ref, *rest, epilogue):
    *extra, o_ref = rest
    acc = jnp.dot(a_ref[...].astype(BF16), w_ref[...].astype(BF16), preferred_element_type=F32)
    o_ref[...] = epilogue(acc, *[e[...] for e in extra]).astype(o_ref.dtype)


def _mm(a, w, *, col0, ncols, tn, out_dtype, name, epilogue=lambda acc: acc, extra=(), extra_specs=()):
    m, k = a.shape
    cb0 = col0 // tn
    return pl.pallas_call(
        functools.partial(_mm_body, epilogue=epilogue),
        grid=(m // TM, ncols // tn),
        in_specs=[pl.BlockSpec((TM, k), lambda i, j: (i, 0)),
                  pl.BlockSpec((k, tn), lambda i, j: (0, cb0 + j)),
                  *extra_specs],
        out_specs=pl.BlockSpec((TM, tn), lambda i, j: (i, j)),
        out_shape=jax.ShapeDtypeStruct((m, ncols), out_dtype),
        compiler_params=_cp(("arbitrary", "arbitrary")),
        name=name,
    )(a, w, *extra)


def _mm_t_body(a_ref, wt_ref, o_ref, *, epilogue):
    acc = lax.dot_general(a_ref[...], wt_ref[...].astype(BF16), _NT, preferred_element_type=F32)
    o_ref[...] = epilogue(acc).astype(o_ref.dtype)


def _mm_t(a, wt, *, row0, ncols, tn, out_dtype, name, epilogue=lambda acc: acc):
    m, k = a.shape
    return pl.pallas_call(
        functools.partial(_mm_t_body, epilogue=epilogue),
        grid=(m // TM, ncols // tn),
        in_specs=[pl.BlockSpec((TM, k), lambda i, j: (i, 0)),
                  pl.BlockSpec((pl.Element(tn), pl.Element(k)),
                               lambda i, j: (pl.multiple_of(row0 + j * tn, 8), 0))],
        out_specs=pl.BlockSpec((TM, tn), lambda i, j: (i, j)),
        out_shape=jax.ShapeDtypeStruct((m, ncols), out_dtype),
        compiler_params=_cp(("arbitrary", "arbitrary")),
        name=name,
    )(a, wt)


def _gla_coeff_matrix():
    c = GLA_CHUNK
    t = np.arange(c)[:, None]
    u = np.arange(c)[None, :]
    blocks = [(u <= t), (u > t)]
    for lvl in range(1, GLA_LEVELS + 1):
        m = 1 << lvl
        half = m // 2
        mid = (t // m) * m + half - 1
        lower = (t % m) >= half
        blocks.append(np.where(lower, (u > mid) & (u <= t), (u > t) & (u <= mid)))
    return np.concatenate(blocks, axis=0).astype(np.float32)


def _gla_level_masks():
    c = GLA_CHUNK
    t = lax.broadcasted_iota(jnp.int32, (c, c), 0)
    s = lax.broadcasted_iota(jnp.int32, (c, c), 1)
    masks = []
    for lvl in range(1, GLA_LEVELS + 1):
        m = 1 << lvl
        half = m // 2
        masks.append(((t >> lvl) == (s >> lvl)) & ((t & (m - 1)) >= half) & ((s & (m - 1)) < half))
    return masks


def _gla_out_norm(o, r, wn):
    return _layer_norm(o) * wn * (r * _sigmoid(r))


def _gla_prompt_body(q_ref, k_ref, v_ref, r_ref, a_ref, wgu_ref, bgu_ref, wn_ref, cm_ref,
                     o_ref, st_ref, s_scr, *, n_sub):
    c = GLA_CHUNK

    @pl.when(pl.program_id(2) == 0)
    def _():
        s_scr[...] = jnp.zeros_like(s_scr)

    masks = _gla_level_masks()
    wgu = wgu_ref[...]
    bgu = bgu_ref[...]
    wn = wn_ref[...]
    cm = cm_ref[...]
    nt, tn = _NT, _TN

    def chunk(ci, carry):
        rows = pl.ds(pl.multiple_of(ci * c, c), c)
        q = q_ref[rows, :] * (GLA_HDK ** -0.5)
        k = k_ref[rows, :]
        v = v_ref[rows, :]
        z = jnp.dot(a_ref[rows, :], wgu, preferred_element_type=F32, precision=HIGHEST) + bgu
        g = _log_sigmoid(z) * (1.0 / GLA_TAU)
        g_hi = g.astype(BF16)
        r1 = g - g_hi.astype(F32)
        g_mid = r1.astype(BF16)
        g_lo = (r1 - g_mid.astype(F32)).astype(BF16)
        g3 = jnp.concatenate([g_hi, g_mid, g_lo], axis=0)
        f = jnp.exp(jnp.dot(cm, g3, preferred_element_type=F32))
        st = s_scr[...]
        vb = v.astype(BF16)
        o = lax.dot_general((q * f[0:c]).astype(BF16), st.astype(BF16), nt, preferred_element_type=F32)
        scores = jnp.zeros((c, c), F32)
        for lvl in range(GLA_LEVELS):
            fl = f[(2 + lvl) * c:(3 + lvl) * c]
            p = lax.dot_general((q * fl).astype(BF16), (k * fl).astype(BF16), nt, preferred_element_type=F32)
            scores = scores + jnp.where(masks[lvl], p, 0.0)
        diag = jnp.sum(q * k, axis=1, keepdims=True)
        o = o + jnp.dot(scores.astype(BF16), vb, preferred_element_type=F32) + diag * v
        kd = (k * f[c:2 * c]).astype(BF16)
        s_scr[...] = st * f[c - 1:c, :] + lax.dot_general(vb, kd, tn, preferred_element_type=F32)
        o_ref[rows, :] = _gla_out_norm(o, r_ref[rows, :], wn).astype(o_ref.dtype)
        return carry

    lax.fori_loop(0, n_sub, chunk, 0, unroll=True)

    @pl.when(pl.program_id(2) == pl.num_programs(2) - 1)
    def _():
        st_ref[0, 0] = s_scr[...]


def _gla_prompt(qkvr, a_low, wgu, bgu, wn):
    n_sub = 4
    tt = GLA_CHUNK * n_sub
    nt_steps = SEQ // tt
    cm = jnp.asarray(np.tile(_gla_coeff_matrix(), (1, 3)), dtype=BF16)
    rows = lambda b, h, c: b * nt_steps + c
    o, st = pl.pallas_call(
        functools.partial(_gla_prompt_body, n_sub=n_sub),
        grid=(N_PROMPT_SEQ, GLA_HEADS, nt_steps),
        in_specs=[
            pl.BlockSpec((tt, GLA_HDK), lambda b, h, c: (rows(b, h, c), h)),
            pl.BlockSpec((tt, GLA_HDK), lambda b, h, c: (rows(b, h, c), GLA_HEADS + h)),
            pl.BlockSpec((tt, GLA_HDV), lambda b, h, c: (rows(b, h, c), GLA_HEADS + h)),
            pl.BlockSpec((tt, GLA_HDV), lambda b, h, c: (rows(b, h, c), 2 * GLA_HEADS + h)),
            pl.BlockSpec((tt, 128), lambda b, h, c: (rows(b, h, c), 0)),
            pl.BlockSpec((128, GLA_HDK), lambda b, h, c: (0, h)),
            pl.BlockSpec((1, GLA_HDK), lambda b, h, c: (0, h)),
            pl.BlockSpec((1, GLA_HDV), lambda b, h, c: (0, h)),
            pl.BlockSpec(cm.shape, lambda b, h, c: (0, 0)),
        ],
        out_specs=[
            pl.BlockSpec((tt, GLA_HDV), lambda b, h, c: (rows(b, h, c), h)),
            pl.BlockSpec((1, 1, GLA_HDV, GLA_HDK), lambda b, h, c: (b, h, 0, 0)),
        ],
        out_shape=[
            jax.ShapeDtypeStruct((N_PROMPT, BRANCH), BF16),
            jax.ShapeDtypeStruct((N_PROMPT_SEQ, GLA_HEADS, GLA_HDV, GLA_HDK), F32),
        ],
        scratch_shapes=[pltpu.VMEM((GLA_HDV, GLA_HDK), F32)],
        compiler_params=_cp(("arbitrary", "arbitrary", "arbitrary")),
        name="gla_prompt",
    )(qkvr, qkvr, qkvr, qkvr, a_low, wgu, bgu, wn, cm)
    return o, st


GLA_SB = 16


def _gla_sample_body(q_ref, k_ref, v_ref, r_ref, a_ref, wgu_ref, bgu_ref, wn_ref, s_ref, o_ref, so_ref):
    q = q_ref[...] * (GLA_HDK ** -0.5)
    k = k_ref[...]
    v = v_ref[...]
    z = jnp.dot(a_ref[...], wgu_ref[...], preferred_element_type=F32, precision=HIGHEST) + bgu_ref[...]
    eg = jnp.exp(_log_sigmoid(z) * (1.0 / GLA_TAU))
    qe = (q * eg).astype(BF16)
    eg_t = eg.T
    k_t = k.T
    rows = []
    for n in range(GLA_SB):
        s0 = s_ref[n, 0]
        rows.append(jnp.dot(qe, s0.astype(BF16), preferred_element_type=F32)[n:n + 1])
        so_ref[n, 0] = s0 * eg_t[:, n:n + 1] + k_t[:, n:n + 1] * v[n:n + 1, :]
    o = jnp.concatenate(rows, axis=0) + jnp.sum(q * k, axis=1, keepdims=True) * v
    o_ref[...] = _gla_out_norm(o, r_ref[...], wn_ref[...]).astype(o_ref.dtype)


def _gla_sample(qkvr, a_low, wgu, bgu, wn, state):
    r0 = N_PROMPT // GLA_SB
    o, st = pl.pallas_call(
        _gla_sample_body,
        grid=(GLA_HEADS, N_SAMPLE // GLA_SB),
        in_specs=[
            pl.BlockSpec((GLA_SB, GLA_HDK), lambda h, i: (r0 + i, h)),
            pl.BlockSpec((GLA_SB, GLA_HDK), lambda h, i: (r0 + i, GLA_HEADS + h)),
            pl.BlockSpec((GLA_SB, GLA_HDV), lambda h, i: (r0 + i, GLA_HEADS + h)),
            pl.BlockSpec((GLA_SB, GLA_HDV), lambda h, i: (r0 + i, 2 * GLA_HEADS + h)),
            pl.BlockSpec((GLA_SB, 128), lambda h, i: (r0 + i, 0)),
            pl.BlockSpec((128, GLA_HDK), lambda h, i: (0, h)),
            pl.BlockSpec((1, GLA_HDK), lambda h, i: (0, h)),
            pl.BlockSpec((1, GLA_HDV), lambda h, i: (0, h)),
            pl.BlockSpec((GLA_SB, 1, GLA_HDK, GLA_HDV), lambda h, i: (i, h, 0, 0)),
        ],
        out_specs=[
            pl.BlockSpec((GLA_SB, GLA_HDV), lambda h, i: (i, h)),
            pl.BlockSpec((GLA_SB, 1, GLA_HDK, GLA_HDV), lambda h, i: (i, h, 0, 0)),
        ],
        out_shape=[
            jax.ShapeDtypeStruct((N_SAMPLE, BRANCH), BF16),
            jax.ShapeDtypeStruct(state.shape, F32),
        ],
        compiler_params=_cp(("arbitrary", "arbitrary")),
        name="gla_sample",
    )(qkvr, qkvr, qkvr, qkvr, a_low, wgu, bgu, wn, state)
    return o, st


def _s5_weights(a_re, a_im, b_re, b_im, c_re, c_im, d, log_dt):
    L = S5_CHUNK
    lam_re = jnp.minimum(a_re, -1e-4)
    lam_im = a_im
    dt = jnp.exp(log_dt)[:, None]
    kk = jnp.arange(L + 1, dtype=F32)[:, None, None]
    pow_re = jnp.exp(lam_re * dt * kk) * jnp.cos(lam_im * dt * kk)
    pow_im = jnp.exp(lam_re * dt * kk) * jnp.sin(lam_im * dt * kk)
    lbar_re, lbar_im = pow_re[1], pow_im[1]
    den = lam_re * lam_re + lam_im * lam_im
    f_re = ((lbar_re - 1.0) * lam_re + lbar_im * lam_im) / den
    f_im = (lbar_im * lam_re - (lbar_re - 1.0) * lam_im) / den
    bb_re = f_re[..., None] * b_re - f_im[..., None] * b_im
    bb_im = f_re[..., None] * b_im + f_im[..., None] * b_re
    nb, ng = S5_BLOCKS, S5_LANE_GROUPS
    bb_re_t = jnp.swapaxes(bb_re, 1, 2)
    bb_im_t = jnp.swapaxes(bb_im, 1, 2)
    lb_re = (pow_re[:L, :, None, :] * bb_re_t - pow_im[:L, :, None, :] * bb_im_t).reshape(L, nb, 128, S5_STATE)
    lb_im = (pow_re[:L, :, None, :] * bb_im_t + pow_im[:L, :, None, :] * bb_re_t).reshape(L, nb, 128, S5_STATE)
    cl_re = (c_re[None] * pow_re[:, :, None, :] - c_im[None] * pow_im[:, :, None, :]).reshape(L + 1, nb, 128, S5_STATE)
    cl_im = (c_re[None] * pow_im[:, :, None, :] + c_im[None] * pow_re[:, :, None, :]).reshape(L + 1, nb, 128, S5_STATE)
    tile_p = jnp.tile(jnp.eye(S5_STATE, dtype=F32), (1, ng))
    same_group = jnp.kron(jnp.eye(ng, dtype=F32), jnp.ones((S5_GROUP, S5_STATE), F32))

    def expand(x):
        return jnp.einsum('...rp,pc->...rc', x, tile_p, precision=HIGHEST) * same_group

    zpow = jnp.concatenate([expand(lb_re), expand(lb_im)], axis=-1)
    cpow = jnp.concatenate([expand(cl_re), expand(-cl_im)], axis=-1)
    wz = jnp.moveaxis(zpow[::-1], 0, 1).reshape(nb, L * 128, 2 * ng * S5_STATE)
    wct = jnp.moveaxis(cpow[1:], 0, 1).reshape(nb, L * 128, 2 * ng * S5_STATE)
    bd = jnp.einsum('kjrc,jqc->kjrq', zpow, cpow[0], precision=HIGHEST)
    sp = jnp.arange(L)[:, None]
    s = jnp.arange(L)[None, :]
    lag = s - sp
    toe = jnp.where((lag >= 0)[:, :, None, None, None], bd[jnp.clip(lag, 0)], 0.0)
    t_mat = jnp.transpose(toe, (2, 0, 3, 1, 4)).reshape(nb, L * 128, L * 128)
    n_steps = int(math.log2(SEQ // L))
    mult = (L * (2.0 ** jnp.arange(n_steps, dtype=F32)))[:, None, None]
    sc_re = jnp.exp(lam_re * dt * mult) * jnp.cos(lam_im * dt * mult)
    sc_im = jnp.exp(lam_re * dt * mult) * jnp.sin(lam_im * dt * mult)

    def state_lanes(x):
        lead = x.shape[:-2]
        return jnp.moveaxis(x.reshape(lead + (nb, ng * S5_STATE)), -2, 0)

    scan_mult = jnp.concatenate([state_lanes(sc_re), state_lanes(sc_im)], axis=-1)
    lbar1 = jnp.concatenate([state_lanes(lbar_re[None]), state_lanes(lbar_im[None])], axis=-1)
    dvec = jnp.tile(d.reshape(nb, 1, 128), (1, 1, L))
    return t_mat.astype(BF16), wz.astype(BF16), wct.astype(BF16), scan_mult, lbar1, dvec


def _s5_prompt_body(u_ref, t_ref, wz_ref, wc_ref, sm_ref, d_ref, y_ref, fre_ref, fim_ref, *, n_steps):
    L = S5_CHUNK
    n_rows = SEQ // L
    ns = S5_LANE_GROUPS * S5_STATE
    v = jnp.concatenate([u_ref[pl.ds(s, n_rows, stride=L), :] for s in range(L)], axis=1)
    vb = v.astype(BF16)
    z = jnp.dot(vb, wz_ref[0], preferred_element_type=F32)
    hr, hi = z[:, :ns], z[:, ns:]
    pos = lax.broadcasted_iota(jnp.int32, (n_rows, ns), 0)
    sm = sm_ref[0]
    for d in range(n_steps):
        sh = 1 << d
        ar, ai = sm[d:d + 1, :ns], sm[d:d + 1, ns:]
        keep = pos >= sh
        pr = jnp.where(keep, pltpu.roll(hr, sh, 0), 0.0)
        pi = jnp.where(keep, pltpu.roll(hi, sh, 0), 0.0)
        hr, hi = hr + ar * pr - ai * pi, hi + ar * pi + ai * pr
    fre_ref[0] = hr[n_rows - 1:n_rows]
    fim_ref[0] = hi[n_rows - 1:n_rows]
    first = pos >= 1
    h_prev = jnp.concatenate([jnp.where(first, pltpu.roll(hr, 1, 0), 0.0),
                              jnp.where(first, pltpu.roll(hi, 1, 0), 0.0)], axis=1)
    y = (jnp.dot(vb, t_ref[0], preferred_element_type=F32)
         + lax.dot_general(h_prev.astype(BF16), wc_ref[0], _NT, preferred_element_type=F32)
         + d_ref[0] * v)
    y = _gelu_tanh(y)
    for s in range(L):
        y_ref[pl.ds(s, n_rows, stride=L), :] = y[:, s * 128:(s + 1) * 128]


def _s5_sample_body(u_ref, k0_ref, wb_ref, wc0_ref, l1_ref, d_ref, hre_ref, him_ref,
                    y_ref, sre_ref, sim_ref):
    ns = S5_LANE_GROUPS * S5_STATE
    us = u_ref[...]
    usb = us.astype(BF16)
    h0r, h0i = hre_ref[...], him_ref[...]
    l1 = l1_ref[0]
    bu = jnp.dot(usb, wb_ref[0], preferred_element_type=F32)
    sre_ref[...] = l1[:, :ns] * h0r - l1[:, ns:] * h0i + bu[:, :ns]
    sim_ref[...] = l1[:, :ns] * h0i + l1[:, ns:] * h0r + bu[:, ns:]
    h0 = jnp.concatenate([h0r, h0i], axis=1).astype(BF16)
    ys = (jnp.dot(usb, k0_ref[0], preferred_element_type=F32)
          + lax.dot_general(h0, wc0_ref[0], _NT, preferred_element_type=F32)
          + d_ref[0] * us)
    y_ref[...] = _gelu_tanh(ys)


def _s5(u, weights, st_re, st_im):
    t_mat, wz, wc, scan_mult, lbar1, dvec = weights
    L = S5_CHUNK
    ns = S5_LANE_GROUPS * S5_STATE
    n_steps = scan_mult.shape[1]
    wblk = lambda a: pl.BlockSpec((1,) + a.shape[1:], lambda j, b: (j, 0, 0))
    y, f_re, f_im = pl.pallas_call(
        functools.partial(_s5_prompt_body, n_steps=n_steps),
        grid=(S5_BLOCKS, N_PROMPT_SEQ),
        in_specs=[
            pl.BlockSpec((SEQ, 128), lambda j, b: (b, j)),
            wblk(t_mat), wblk(wz), wblk(wc), wblk(scan_mult), wblk(dvec),
        ],
        out_specs=[
            pl.BlockSpec((SEQ, 128), lambda j, b: (b, j)),
            pl.BlockSpec((1, 1, ns), lambda j, b: (b, 0, j)),
            pl.BlockSpec((1, 1, ns), lambda j, b: (b, 0, j)),
        ],
        out_shape=[
            jax.ShapeDtypeStruct((N_PROMPT, BRANCH), F32),
            jax.ShapeDtypeStruct((N_PROMPT_SEQ, 1, S5_GROUPS * S5_STATE), F32),
            jax.ShapeDtypeStruct((N_PROMPT_SEQ, 1, S5_GROUPS * S5_STATE), F32),
        ],
        compiler_params=_cp(("arbitrary", "arbitrary")),
        name="s5_prompt",
    )(u, t_mat, wz, wc, scan_mult, dvec)
    k0 = t_mat[:, 0:128, 0:128]
    wb = wz[:, (L - 1) * 128:L * 128, :]
    wc0 = wc[:, 0:128, :]
    d0 = dvec[:, :, 0:128]
    sblk = lambda a: pl.BlockSpec((1,) + a.shape[1:], lambda j: (j, 0, 0))
    r0 = N_PROMPT // N_SAMPLE
    y_s, s_re, s_im = pl.pallas_call(
        _s5_sample_body,
        grid=(S5_BLOCKS,),
        in_specs=[
            pl.BlockSpec((N_SAMPLE, 128), lambda j: (r0, j)),
            sblk(k0), sblk(wb), sblk(wc0), sblk(lbar1), sblk(d0),
            pl.BlockSpec((N_SAMPLE, ns), lambda j: (0, j)),
            pl.BlockSpec((N_SAMPLE, ns), lambda j: (0, j)),
        ],
        out_specs=[
            pl.BlockSpec((N_SAMPLE, 128), lambda j: (0, j)),
            pl.BlockSpec((N_SAMPLE, ns), lambda j: (0, j)),
            pl.BlockSpec((N_SAMPLE, ns), lambda j: (0, j)),
        ],
        out_shape=[
            jax.ShapeDtypeStruct((N_SAMPLE, BRANCH), F32),
            jax.ShapeDtypeStruct((N_SAMPLE, S5_GROUPS * S5_STATE), F32),
            jax.ShapeDtypeStruct((N_SAMPLE, S5_GROUPS * S5_STATE), F32),
        ],
        compiler_params=_cp(("arbitrary",)),
        name="s5_sample",
    )(u, k0, wb, wc0, lbar1, d0, st_re, st_im)
    return jnp.concatenate([y, y_s], axis=0), f_re, f_im, s_re, s_im


def _merge_body(o_ref, z_ref, w0_ref, w1_ref, g0_ref, g1_ref, out_ref):
    p0 = jnp.dot(o_ref[...], w0_ref[0].astype(BF16), preferred_element_type=F32)
    p1 = jnp.dot(z_ref[...], w1_ref[0].astype(BF16), preferred_element_type=F32)
    out_ref[...] = (g0_ref[...] * p0 + g1_ref[...] * p1).astype(out_ref.dtype)


def _merge(o, z, w_branch, gates):
    ncb = D_MODEL // TN
    return pl.pallas_call(
        _merge_body,
        grid=(N_TOK // TM, ncb),
        in_specs=[
            pl.BlockSpec((TM, BRANCH), lambda i, j: (i, 0)),
            pl.BlockSpec((TM, BRANCH), lambda i, j: (i, 0)),
            pl.BlockSpec((1, BRANCH, TN), lambda i, j: (0, 0, j)),
            pl.BlockSpec((1, BRANCH, TN), lambda i, j: (1, 0, j)),
            pl.BlockSpec((TM, TN), lambda i, j: (i, j)),
            pl.BlockSpec((TM, TN), lambda i, j: (i, ncb + j)),
        ],
        out_specs=pl.BlockSpec((TM, TN), lambda i, j: (i, j)),
        out_shape=jax.ShapeDtypeStruct((N_TOK, D_MODEL), BF16),
        compiler_params=_cp(("arbitrary", "arbitrary")),
        name="merge",
    )(o, z, w_branch, w_branch, gates, gates)


OUT_TM = 416
OUT_TK = 512


def _out_ln_body(a_ref, w_ref, x_ref, g_ref, b_ref, wr_ref, br_ref, o_ref, ids_ref, wts_ref):
    kk = pl.program_id(1)
    part = jnp.dot(a_ref[...], w_ref[...], preferred_element_type=F32)

    @pl.when(kk == 0)
    def _():
        o_ref[...] = part

    @pl.when(kk > 0)
    def _():
        o_ref[...] += part

    @pl.when(kk == pl.num_programs(1) - 1)
    def _():
        x1 = _layer_norm(ALPHA * x_ref[...] + o_ref[...]) * g_ref[...] + b_ref[...]
        o_ref[...] = x1
        x_hi = x1.astype(BF16)
        x_mid = (x1 - x_hi.astype(F32)).astype(BF16)
        wr = wr_ref[...]
        both = jnp.dot(x_hi, wr, preferred_element_type=F32)
        logits = (both[:, :128] + both[:, 128:]
                  + jnp.dot(x_mid, wr[:, :128], preferred_element_type=F32) + br_ref[...])
        _route(logits, ids_ref, wts_ref)


def _out_ln(pre, w_out_bf, x_all, g, b, w_r, b_r):
    w_hi = w_r.astype(BF16)
    w_mid = (w_r - w_hi.astype(F32)).astype(BF16)
    wr = jnp.concatenate([w_hi, w_mid], axis=1)
    row_blk = pl.BlockSpec((OUT_TM, 128), lambda i, k: (i, 0))
    return pl.pallas_call(
        _out_ln_body,
        grid=(N_TOK // OUT_TM, D_MODEL // OUT_TK),
        in_specs=[
            pl.BlockSpec((OUT_TM, OUT_TK), lambda i, k: (i, k)),
            pl.BlockSpec((OUT_TK, D_MODEL), lambda i, k: (k, 0)),
            pl.BlockSpec((OUT_TM, D_MODEL), lambda i, k: (i, 0)),
            pl.BlockSpec((1, D_MODEL), lambda i, k: (0, 0)),
            pl.BlockSpec((1, D_MODEL), lambda i, k: (0, 0)),
            pl.BlockSpec((D_MODEL, 256), lambda i, k: (0, 0)),
            pl.BlockSpec((1, 128), lambda i, k: (0, 0)),
        ],
        out_specs=[pl.BlockSpec((OUT_TM, D_MODEL), lambda i, k: (i, 0)), row_blk, row_blk],
        out_shape=[jax.ShapeDtypeStruct((N_TOK, D_MODEL), F32),
                   jax.ShapeDtypeStruct((N_TOK, 128), jnp.int32),
                   jax.ShapeDtypeStruct((N_TOK, 128), F32)],
        compiler_params=_cp(("arbitrary", "arbitrary")),
        name="out_ln1_router",
    )(pre, w_out_bf, x_all, g, b, wr, b_r)


def _route(logits, ids_ref, wts_ref):
    lane = lax.broadcasted_iota(jnp.int32, logits.shape, 1)
    neg = -jnp.inf
    big = 1 << 20
    gl = jnp.where(lane < 8, logits, neg)
    gmax = jnp.max(gl, axis=1, keepdims=True)
    gidx = jnp.min(jnp.where(gl == gmax, lane, big), axis=1, keepdims=True)
    gprob = 1.0 / jnp.sum(jnp.exp(gl - gmax), axis=1, keepdims=True)
    in_group = (lane >= 8) & (lane < 8 + N_EXPERTS) & (((lane - 8) >> 3) == gidx)
    el = jnp.where(in_group, logits, neg)
    v1 = jnp.max(el, axis=1, keepdims=True)
    i1 = jnp.min(jnp.where(el == v1, lane, big), axis=1, keepdims=True)
    el2 = jnp.where(lane == i1, neg, el)
    v2 = jnp.max(el2, axis=1, keepdims=True)
    i2 = jnp.min(jnp.where(el2 == v2, lane, big), axis=1, keepdims=True)
    e2 = jnp.exp(v2 - v1)
    w1 = gprob / (1.0 + e2)
    w2 = gprob * e2 / (1.0 + e2)
    ids_ref[...] = jnp.where(lane == 0, i1 - 8, jnp.where(lane == 1, i2 - 8, 0))
    wts_ref[...] = jnp.where(lane == 0, w1, jnp.where(lane == 1, w2, 0.0))


N_ASSIGN = 2 * N_TOK
MOE_NBLOCKS = (N_ASSIGN + N_EXPERTS * (MOE_BLOCK - 1) + MOE_BLOCK - 1) // MOE_BLOCK
MOE_ROWS = MOE_NBLOCKS * MOE_BLOCK
MOE_HC = MOE_HIDDEN // MOE_HSPLIT


def _row_copy(src_hbm, row, dst, dst_row, sem):
    return pltpu.make_async_copy(src_hbm.at[pl.ds(row, 1), :], dst.at[pl.ds(dst_row, 1), :], sem)


MOE_MACRO = 4
MOE_NMACRO = N_ASSIGN // (MOE_MACRO * MOE_BLOCK) + N_EXPERTS + 1
MOE_DCOLS = 1024


def _block_copy(acc, t, o_hbm, blk, sem):
    return pltpu.make_async_copy(acc.at[pl.ds(pl.multiple_of(t * MOE_BLOCK, MOE_BLOCK), MOE_BLOCK), :],
                                 o_hbm.at[pl.ds(pl.multiple_of(blk * MOE_BLOCK, MOE_BLOCK), MOE_BLOCK), :], sem)


def _moe_body(mexp_ref, mstart_ref, mnsub_ref, tok_ref, x_hbm, wg_ref, wu_ref, wd_ref, o_hbm,
              stage, xb, acc, gsem, osem):
    del mexp_ref
    m = pl.program_id(0)
    h = pl.program_id(1)
    nsub = mnsub_ref[m]
    start = mstart_ref[m]
    last_m = pl.num_programs(0) - 1

    def wait_blocks(n):
        def done(t, c):
            _block_copy(acc, t, o_hbm, 0, osem).wait()
            return c

        lax.fori_loop(0, n, done, 0)

    @pl.when((h == 0) & (m > 0))
    def _():
        wait_blocks(mnsub_ref[jnp.maximum(m - 1, 0)])

    def gather_start(t, slot):
        base = (start + t) * MOE_BLOCK
        for r in range(MOE_BLOCK):
            _row_copy(x_hbm, tok_ref[base + r], stage.at[slot], r, gsem.at[slot]).start()

    @pl.when((h == 0) & (nsub > 0))
    def _():
        gather_start(0, 0)

        def sub(t, c):
            slot = t % 2

            @pl.when(t + 1 < nsub)
            def _():
                gather_start(t + 1, 1 - slot)

            pltpu.make_async_copy(x_hbm.at[pl.ds(0, MOE_BLOCK), :], stage.at[slot], gsem.at[slot]).wait()
            xb[pl.ds(pl.multiple_of(t * MOE_BLOCK, MOE_BLOCK), MOE_BLOCK), :] = stage[slot].astype(BF16)
            return c

        lax.fori_loop(0, nsub, sub, 0)

    for ns in range(1, MOE_MACRO + 1):
        @pl.when(nsub == ns)
        def _(ns=ns):
            rows = ns * MOE_BLOCK

            @pl.when(h == 0)
            def _():
                acc[0:rows, :] = jnp.zeros((rows, D_MODEL), F32)

            x = xb[0:rows, :]
            hg = jnp.dot(x, wg_ref[0].astype(BF16), preferred_element_type=F32)
            hu = jnp.dot(x, wu_ref[0].astype(BF16), preferred_element_type=F32)
            hh = (hg * _sigmoid(hg) * hu).astype(BF16)
            wd = wd_ref[0].astype(BF16)
            for cc in range(D_MODEL // MOE_DCOLS):
                cols = slice(cc * MOE_DCOLS, (cc + 1) * MOE_DCOLS)
                acc[0:rows, cols] += jnp.dot(hh, wd[:, cols], preferred_element_type=F32)

    @pl.when((h == pl.num_programs(1) - 1) & (nsub > 0))
    def _():
        def put(t, c):
            _block_copy(acc, t, o_hbm, start + t, osem).start()
            return c

        lax.fori_loop(0, nsub, put, 0)

        @pl.when(m == last_m)
        def _():
            wait_blocks(nsub)

    @pl.when((h == 0) & (nsub == 0))
    def _():
        acc[0:MOE_BLOCK, :] = jnp.zeros((MOE_BLOCK, D_MODEL), F32)
        for t in range(MOE_MACRO):
            @pl.when(start + t < MOE_NBLOCKS)
            def _(t=t):
                cp = _block_copy(acc, 0, o_hbm, start + t, osem)
                cp.start()
                cp.wait()


def _moe_experts(x1, wg, wu, wd, mexp, mstart, mnsub, row_tok):
    grid_spec = pltpu.PrefetchScalarGridSpec(
        num_scalar_prefetch=4,
        grid=(MOE_NMACRO, MOE_HSPLIT),
        in_specs=[
            pl.BlockSpec(memory_space=pl.ANY),
            pl.BlockSpec((1, D_MODEL, MOE_HC), lambda m, h, me, ms, mn, rt: (me[m], 0, h)),
            pl.BlockSpec((1, D_MODEL, MOE_HC), lambda m, h, me, ms, mn, rt: (me[m], 0, h)),
            pl.BlockSpec((1, MOE_HC, D_MODEL), lambda m, h, me, ms, mn, rt: (me[m], h, 0)),
        ],
        out_specs=pl.BlockSpec(memory_space=pl.ANY),
        scratch_shapes=[pltpu.VMEM((2, MOE_BLOCK, D_MODEL), F32),
                        pltpu.VMEM((MOE_MACRO * MOE_BLOCK, D_MODEL), BF16),
                        pltpu.VMEM((MOE_MACRO * MOE_BLOCK, D_MODEL), F32),
                        pltpu.SemaphoreType.DMA((2,)), pltpu.SemaphoreType.DMA(())],
    )
    return pl.pallas_call(
        _moe_body,
        grid_spec=grid_spec,
        out_shape=jax.ShapeDtypeStruct((MOE_ROWS, D_MODEL), F32),
        compiler_params=_cp(("arbitrary", "arbitrary")),
        name="moe_experts",
    )(mexp, mstart, mnsub, row_tok, x1, wg, wu, wd)


CMB_TM = 128
CMB_PROMPT_TILES = N_PROMPT // CMB_TM


def _combine_body(pos_ref, eo_hbm, wts_ref, x1_ref, g_ref, b_ref, yp_ref, ys_ref, buf, sem):
    i = pl.program_id(0)
    slot = i % 2

    def fetch(tile, s):
        for r in range(CMB_TM):
            a = 2 * (tile * CMB_TM + r)
            _row_copy(eo_hbm, pos_ref[a], buf.at[s, 0], r, sem.at[s]).start()
            _row_copy(eo_hbm, pos_ref[a + 1], buf.at[s, 1], r, sem.at[s]).start()

    @pl.when(i == 0)
    def _():
        fetch(0, 0)

    @pl.when(i + 1 < pl.num_programs(0))
    def _():
        fetch(i + 1, 1 - slot)

    for k in range(2):
        pltpu.make_async_copy(eo_hbm.at[pl.ds(0, CMB_TM), :], buf.at[slot, k], sem.at[slot]).wait()
    w = wts_ref[...]
    y = w[:, 0:1] * buf[slot, 0] + w[:, 1:2] * buf[slot, 1]
    x2 = _layer_norm(ALPHA * x1_ref[...] + y) * g_ref[...] + b_ref[...]

    @pl.when(i < CMB_PROMPT_TILES)
    def _():
        yp_ref[...] = x2

    @pl.when(i >= CMB_PROMPT_TILES)
    def _():
        ys_ref[...] = x2


def _combine(pos, eo, wts, x1, g, b):
    grid_spec = pltpu.PrefetchScalarGridSpec(
        num_scalar_prefetch=1,
        grid=(N_TOK // CMB_TM,),
        in_specs=[
            pl.BlockSpec(memory_space=pl.ANY),
            pl.BlockSpec((CMB_TM, 128), lambda i, p: (i, 0)),
            pl.BlockSpec((CMB_TM, D_MODEL), lambda i, p: (i, 0)),
            pl.BlockSpec((1, D_MODEL), lambda i, p: (0, 0)),
            pl.BlockSpec((1, D_MODEL), lambda i, p: (0, 0)),
        ],
        out_specs=[
            pl.BlockSpec((CMB_TM, D_MODEL), lambda i, p: (jnp.minimum(i, CMB_PROMPT_TILES - 1), 0)),
            pl.BlockSpec((CMB_TM, D_MODEL), lambda i, p: (0, 0)),
        ],
        scratch_shapes=[pltpu.VMEM((2, 2, CMB_TM, D_MODEL), F32), pltpu.SemaphoreType.DMA((2,))],
    )
    return pl.pallas_call(
        _combine_body,
        grid_spec=grid_spec,
        out_shape=[jax.ShapeDtypeStruct((N_PROMPT, D_MODEL), F32),
                   jax.ShapeDtypeStruct((N_SAMPLE, D_MODEL), F32)],
        compiler_params=_cp(("arbitrary",)),
        name="combine_ln2",
    )(pos, eo, wts, x1, g, b)


def _route_positions(ids):
    eid = ids[:, :2].reshape(-1)
    onehot = (eid[:, None] == jnp.arange(N_EXPERTS, dtype=jnp.int32)[None, :]).astype(jnp.int32)
    csum = jnp.cumsum(onehot, axis=0)
    rank = jnp.take_along_axis(csum, eid[:, None], axis=1)[:, 0] - 1
    counts = csum[-1]
    nblk_e = (counts + MOE_BLOCK - 1) // MOE_BLOCK
    bend = jnp.cumsum(nblk_e)
    bstart = bend - nblk_e
    pos = bstart[eid] * MOE_BLOCK + rank
    row_tok = jnp.zeros((MOE_ROWS,), jnp.int32).at[pos].set(jnp.arange(N_ASSIGN, dtype=jnp.int32) // 2)
    nstep_e = (nblk_e + MOE_MACRO - 1) // MOE_MACRO
    send = jnp.cumsum(nstep_e)
    n_steps, n_blocks = send[-1], bend[-1]
    m = jnp.arange(MOE_NMACRO, dtype=jnp.int32)
    e_of_m = jnp.minimum(jnp.searchsorted(send, m, side='right'), N_EXPERTS - 1).astype(jnp.int32)
    local = m - (send - nstep_e)[e_of_m]
    valid = m < n_steps
    mexp = jnp.where(valid, e_of_m, e_of_m[jnp.maximum(n_steps - 1, 0)])
    mstart = jnp.where(valid, bstart[e_of_m] + MOE_MACRO * local, n_blocks + MOE_MACRO * (m - n_steps))
    mnsub = jnp.where(valid, jnp.clip(nblk_e[e_of_m] - MOE_MACRO * local, 0, MOE_MACRO), 0)
    return (pos.astype(jnp.int32), row_tok, mexp.astype(jnp.int32), mstart.astype(jnp.int32),
            mnsub.astype(jnp.int32))


def kernel(x_prompt, x_sample, state_gla, state_s5_re, state_s5_im, w_in, w_gla_gate_up, b_gla_gate_up, w_gla_norm, s5_a_re, s5_a_im, s5_b_re, s5_b_im, s5_c_re, s5_c_im, s5_d, s5_log_dt, w_s5_glu, b_s5_glu, w_branch, w_out, ln1_g, ln1_b, w_router_group, b_router_group, w_router_expert, b_router_expert, w_moe_gate, w_moe_up, w_moe_down, ln2_g, ln2_b):
    x_all = jnp.concatenate([x_prompt.reshape(N_PROMPT, D_MODEL), x_sample.reshape(N_SAMPLE, D_MODEL)], axis=0)
    x_bf = x_all.astype(BF16)

    w_in_t = w_in.T
    qkvr = _mm_t(x_bf, w_in_t, row0=0, ncols=COL_A, tn=TN, out_dtype=F32, name="proj_qkvr")
    a_low = _mm_t(x_bf, w_in_t, row0=COL_A, ncols=128, tn=128, out_dtype=F32, name="proj_a")
    u = _mm_t(x_bf, w_in_t, row0=COL_U, ncols=BRANCH, tn=TN, out_dtype=F32, name="proj_u")
    gates = _mm_t(x_bf, w_in_t, row0=COL_G, ncols=2 * D_MODEL, tn=TN, out_dtype=F32, name="proj_gates",
                  epilogue=_sigmoid)

    wgu = jnp.pad(w_gla_gate_up, ((0, 128 - GLA_RANK), (0, 0)))
    bgu = b_gla_gate_up.reshape(1, GLA_DK)
    wn = w_gla_norm.reshape(1, BRANCH)
    o_p, gla_p_t = _gla_prompt(qkvr, a_low, wgu, bgu, wn)
    o_s, gla_s = _gla_sample(qkvr, a_low, wgu, bgu, wn, state_gla)
    o_all = jnp.concatenate([o_p, o_s], axis=0)
    gla_p = jnp.swapaxes(gla_p_t, 2, 3)

    s5w = _s5_weights(s5_a_re, s5_a_im, s5_b_re, s5_b_im, s5_c_re, s5_c_im, s5_d, s5_log_dt)
    y, s5_re_p, s5_im_p, s5_re_s, s5_im_s = _s5(
        u, s5w, state_s5_re.reshape(N_SAMPLE, -1), state_s5_im.reshape(N_SAMPLE, -1))
    z = _mm(y, w_s5_glu, col0=0, ncols=BRANCH, tn=TN, out_dtype=BF16, name="s5_glu",
            epilogue=lambda acc, yt, bt: yt * _sigmoid(acc + bt),
            extra=(y, b_s5_glu.reshape(1, BRANCH)),
            extra_specs=(pl.BlockSpec((TM, TN), lambda i, j: (i, j)), pl.BlockSpec((1, TN), lambda i, j: (0, j))))

    pre = _merge(o_all, z, w_branch, gates)
    w_r = jnp.concatenate([w_router_group,
                           jnp.moveaxis(w_router_expert, 0, 1).reshape(D_MODEL, N_EXPERTS),
                           jnp.zeros((D_MODEL, 128 - 8 - N_EXPERTS), F32)], axis=1)
    b_r = jnp.concatenate([b_router_group, b_router_expert.reshape(-1),
                           jnp.zeros((128 - 8 - N_EXPERTS,), F32)]).reshape(1, 128)
    x1, ids, wts = _out_ln(pre, w_out.astype(BF16), x_all, ln1_g.reshape(1, D_MODEL), ln1_b.reshape(1, D_MODEL),
                           w_r, b_r)

    pos, row_tok, mexp, mstart, mnsub = _route_positions(ids)
    eo = _moe_experts(x1,
                      w_moe_gate.reshape(N_EXPERTS, D_MODEL, MOE_HIDDEN),
                      w_moe_up.reshape(N_EXPERTS, D_MODEL, MOE_HIDDEN),
                      w_moe_down.reshape(N_EXPERTS, MOE_HIDDEN, D_MODEL),
                      mexp, mstart, mnsub, row_tok)
    y_p, y_s = _combine(pos, eo, wts, x1, ln2_g.reshape(1, D_MODEL), ln2_b.reshape(1, D_MODEL))

    return (y_p.reshape(N_PROMPT_SEQ, SEQ, D_MODEL), y_s.reshape(N_SAMPLE, 1, D_MODEL),
            gla_p,
            s5_re_p.reshape(N_PROMPT_SEQ, S5_GROUPS, S5_STATE), s5_im_p.reshape(N_PROMPT_SEQ, S5_GROUPS, S5_STATE),
            gla_s,
            s5_re_s.reshape(N_SAMPLE, S5_GROUPS, S5_STATE), s5_im_s.reshape(N_SAMPLE, S5_GROUPS, S5_STATE))
```

```python
import functools
import math

import jax
import jax.numpy as jnp
import numpy as np
from jax import lax
from jax.experimental import pallas as pl
from jax.experimental.pallas import tpu as pltpu

F32 = jnp.float32
BF16 = jnp.bfloat16
HIGHEST = lax.Precision.HIGHEST

D_MODEL = 4096
N_PROMPT_SEQ = 4
SEQ = 2048
N_SAMPLE = 128
N_PROMPT = N_PROMPT_SEQ * SEQ
N_TOK = N_PROMPT + N_SAMPLE
BRANCH = D_MODEL // 2
GLA_HEADS = 4
GLA_DK = D_MODEL // 4
GLA_HDK = GLA_DK // GLA_HEADS
GLA_HDV = BRANCH // GLA_HEADS
GLA_RANK = 16
GLA_TAU = 16.0
GLA_CHUNK = 64
GLA_LEVELS = 6
S5_GROUP = 16
S5_GROUPS = BRANCH // S5_GROUP
S5_STATE = 64
S5_CHUNK = 8
S5_LANE_GROUPS = 8
S5_BLOCKS = S5_GROUPS // S5_LANE_GROUPS
N_EXPERTS = 64
MOE_HIDDEN = D_MODEL // 8
MOE_BLOCK = 128
MOE_HSPLIT = 2
LN_EPS = 1e-5
ALPHA = 2.0 ** 0.25
COL_A = 6144
COL_U = 6160
COL_G = 8208
IN_COLS = 16400

VMEM_LIMIT = 56 * 1024 * 1024
TM = 1040
TN = 512


_NT = (((1,), (1,)), ((), ()))
_TN = (((0,), (0,)), ((), ()))


def _cp(sem):
    return pltpu.CompilerParams(dimension_semantics=sem, vmem_limit_bytes=VMEM_LIMIT)


def _sigmoid(x):
    return 1.0 / (1.0 + jnp.exp(-x))


def _log_sigmoid(z):
    return jnp.minimum(z, 0.0) - jnp.log1p(jnp.exp(-jnp.abs(z)))


def _gelu_tanh(x):
    return 0.5 * x * (1.0 + jnp.tanh(math.sqrt(2.0 / math.pi) * (x + 0.044715 * (x * x * x))))


def _layer_norm(x):
    mu = jnp.mean(x, axis=-1, keepdims=True)
    xc = x - mu
    var = jnp.mean(xc * xc, axis=-1, keepdims=True)
    return xc * lax.rsqrt(var + LN_EPS)


def _mm_body(a_ref, w_ref, *rest, epilogue):
    *extra, o_ref = rest
    acc = jnp.dot(a_ref[...].astype(BF16), w_ref[...].astype(BF16), preferred_element_type=F32)
    o_ref[...] = epilogue(acc, *[e[...] for e in extra]).astype(o_ref.dtype)


def _mm(a, w, *, col0, ncols, tn, out_dtype, name, epilogue=lambda acc: acc, extra=(), extra_specs=()):
    m, k = a.shape
    cb0 = col0 // tn
    return pl.pallas_call(
        functools.partial(_mm_body, epilogue=epilogue),
        grid=(m // TM, ncols // tn),
        in_specs=[pl.BlockSpec((TM, k), lambda i, j: (i, 0)),
                  pl.BlockSpec((k, tn), lambda i, j: (0, cb0 + j)),
                  *extra_specs],
        out_specs=pl.BlockSpec((TM, tn), lambda i, j: (i, j)),
        out_shape=jax.ShapeDtypeStruct((m, ncols), out_dtype),
        compiler_params=_cp(("arbitrary", "arbitrary")),
        name=name,
    )(a, w, *extra)


def _mm_t_body(a_ref, wt_ref, o_ref, *, epilogue):
    acc = lax.dot_general(a_ref[...], wt_ref[...].astype(BF16), _NT, preferred_element_type=F32)
    o_ref[...] = epilogue(acc).astype(o_ref.dtype)


def _mm_t(a, wt, *, row0, ncols, tn, out_dtype, name, epilogue=lambda acc: acc):
    m, k = a.shape
    return pl.pallas_call(
        functools.partial(_mm_t_body, epilogue=epilogue),
        grid=(m // TM, ncols // tn),
        in_specs=[pl.BlockSpec((TM, k), lambda i, j: (i, 0)),
                  pl.BlockSpec((pl.Element(tn), pl.Element(k)),
                               lambda i, j: (pl.multiple_of(row0 + j * tn, 8), 0))],
        out_specs=pl.BlockSpec((TM, tn), lambda i, j: (i, j)),
        out_shape=jax.ShapeDtypeStruct((m, ncols), out_dtype),
        compiler_params=_cp(("arbitrary", "arbitrary")),
        name=name,
    )(a, wt)


def _gla_coeff_matrix():
    c = GLA_CHUNK
    t = np.arange(c)[:, None]
    u = np.arange(c)[None, :]
    blocks = [(u <= t), (u > t)]
    for lvl in range(1, GLA_LEVELS + 1):
        m = 1 << lvl
        half = m // 2
        mid = (t // m) * m + half - 1
        lower = (t % m) >= half
        blocks.append(np.where(lower, (u > mid) & (u <= t), (u > t) & (u <= mid)))
    return np.concatenate(blocks, axis=0).astype(np.float32)


def _gla_level_masks():
    c = GLA_CHUNK
    t = lax.broadcasted_iota(jnp.int32, (c, c), 0)
    s = lax.broadcasted_iota(jnp.int32, (c, c), 1)
    masks = []
    for lvl in range(1, GLA_LEVELS + 1):
        m = 1 << lvl
        half = m // 2
        masks.append(((t >> lvl) == (s >> lvl)) & ((t & (m - 1)) >= half) & ((s & (m - 1)) < half))
    return masks


def _gla_out_norm(o, r, wn):
    return _layer_norm(o) * wn * (r * _sigmoid(r))


def _gla_prompt_body(q_ref, k_ref, v_ref, r_ref, a_ref, wgu_ref, bgu_ref, wn_ref, cm_ref,
                     o_ref, st_ref, s_scr, *, n_sub):
    c = GLA_CHUNK

    @pl.when(pl.program_id(2) == 0)
    def _():
        s_scr[...] = jnp.zeros_like(s_scr)

    masks = _gla_level_masks()
    wgu = wgu_ref[...]
    bgu = bgu_ref[...]
    wn = wn_ref[...]
    cm = cm_ref[...]
    nt, tn = _NT, _TN

    def chunk(ci, carry):
        rows = pl.ds(pl.multiple_of(ci * c, c), c)
        q = q_ref[rows, :] * (GLA_HDK ** -0.5)
        k = k_ref[rows, :]
        v = v_ref[rows, :]
        z = jnp.dot(a_ref[rows, :], wgu, preferred_element_type=F32, precision=HIGHEST) + bgu
        g = _log_sigmoid(z) * (1.0 / GLA_TAU)
        g_hi = g.astype(BF16)
        r1 = g - g_hi.astype(F32)
        g_mid = r1.astype(BF16)
        g_lo = (r1 - g_mid.astype(F32)).astype(BF16)
        g3 = jnp.concatenate([g_hi, g_mid, g_lo], axis=0)
        f = jnp.exp(jnp.dot(cm, g3, preferred_element_type=F32))
        st = s_scr[...]
        vb = v.astype(BF16)
        o = lax.dot_general((q * f[0:c]).astype(BF16), st.astype(BF16), nt, preferred_element_type=F32)
        scores = jnp.zeros((c, c), F32)
        for lvl in range(GLA_LEVELS):
            fl = f[(2 + lvl) * c:(3 + lvl) * c]
            p = lax.dot_general((q * fl).astype(BF16), (k * fl).astype(BF16), nt, preferred_element_type=F32)
            scores = scores + jnp.where(masks[lvl], p, 0.0)
        diag = jnp.sum(q * k, axis=1, keepdims=True)
        o = o + jnp.dot(scores.astype(BF16), vb, preferred_element_type=F32) + diag * v
        kd = (k * f[c:2 * c]).astype(BF16)
        s_scr[...] = st * f[c - 1:c, :] + lax.dot_general(vb, kd, tn, preferred_element_type=F32)
        o_ref[rows, :] = _gla_out_norm(o, r_ref[rows, :], wn).astype(o_ref.dtype)
        return carry

    lax.fori_loop(0, n_sub, chunk, 0, unroll=True)

    @pl.when(pl.program_id(2) == pl.num_programs(2) - 1)
    def _():
        st_ref[0, 0] = s_scr[...]


def _gla_prompt(qkvr, a_low, wgu, bgu, wn):
    n_sub = 4
    tt = GLA_CHUNK * n_sub
    nt_steps = SEQ // tt
    cm = jnp.asarray(np.tile(_gla_coeff_matrix(), (1, 3)), dtype=BF16)
    rows = lambda b, h, c: b * nt_steps + c
    o, st = pl.pallas_call(
        functools.partial(_gla_prompt_body, n_sub=n_sub),
        grid=(N_PROMPT_SEQ, GLA_HEADS, nt_steps),
        in_specs=[
            pl.BlockSpec((tt, GLA_HDK), lambda b, h, c: (rows(b, h, c), h)),
            pl.BlockSpec((tt, GLA_HDK), lambda b, h, c: (rows(b, h, c), GLA_HEADS + h)),
            pl.BlockSpec((tt, GLA_HDV), lambda b, h, c: (rows(b, h, c), GLA_HEADS + h)),
            pl.BlockSpec((tt, GLA_HDV), lambda b, h, c: (rows(b, h, c), 2 * GLA_HEADS + h)),
            pl.BlockSpec((tt, 128), lambda b, h, c: (rows(b, h, c), 0)),
            pl.BlockSpec((128, GLA_HDK), lambda b, h, c: (0, h)),
            pl.BlockSpec((1, GLA_HDK), lambda b, h, c: (0, h)),
            pl.BlockSpec((1, GLA_HDV), lambda b, h, c: (0, h)),
            pl.BlockSpec(cm.shape, lambda b, h, c: (0, 0)),
        ],
        out_specs=[
            pl.BlockSpec((tt, GLA_HDV), lambda b, h, c: (rows(b, h, c), h)),
            pl.BlockSpec((1, 1, GLA_HDV, GLA_HDK), lambda b, h, c: (b, h, 0, 0)),
        ],
        out_shape=[
            jax.ShapeDtypeStruct((N_PROMPT, BRANCH), BF16),
            jax.ShapeDtypeStruct((N_PROMPT_SEQ, GLA_HEADS, GLA_HDV, GLA_HDK), F32),
        ],
        scratch_shapes=[pltpu.VMEM((GLA_HDV, GLA_HDK), F32)],
        compiler_params=_cp(("arbitrary", "arbitrary", "arbitrary")),
        name="gla_prompt",
    )(qkvr, qkvr, qkvr, qkvr, a_low, wgu, bgu, wn, cm)
    return o, st


GLA_SB = 16


def _gla_sample_body(q_ref, k_ref, v_ref, r_ref, a_ref, wgu_ref, bgu_ref, wn_ref, s_ref, o_ref, so_ref):
    q = q_ref[...] * (GLA_HDK ** -0.5)
    k = k_ref[...]
    v = v_ref[...]
    z = jnp.dot(a_ref[...], wgu_ref[...], preferred_element_type=F32, precision=HIGHEST) + bgu_ref[...]
    eg = jnp.exp(_log_sigmoid(z) * (1.0 / GLA_TAU))
    qe = (q * eg).astype(BF16)
    eg_t = eg.T
    k_t = k.T
    rows = []
    for n in range(GLA_SB):
        s0 = s_ref[n, 0]
        rows.append(jnp.dot(qe, s0.astype(BF16), preferred_element_type=F32)[n:n + 1])
        so_ref[n, 0] = s0 * eg_t[:, n:n + 1] + k_t[:, n:n + 1] * v[n:n + 1, :]
    o = jnp.concatenate(rows, axis=0) + jnp.sum(q * k, axis=1, keepdims=True) * v
    o_ref[...] = _gla_out_norm(o, r_ref[...], wn_ref[...]).astype(o_ref.dtype)


def _gla_sample(qkvr, a_low, wgu, bgu, wn, state):
    r0 = N_PROMPT // GLA_SB
    o, st = pl.pallas_call(
        _gla_sample_body,
        grid=(GLA_HEADS, N_SAMPLE // GLA_SB),
        in_specs=[
            pl.BlockSpec((GLA_SB, GLA_HDK), lambda h, i: (r0 + i, h)),
            pl.BlockSpec((GLA_SB, GLA_HDK), lambda h, i: (r0 + i, GLA_HEADS + h)),
            pl.BlockSpec((GLA_SB, GLA_HDV), lambda h, i: (r0 + i, GLA_HEADS + h)),
            pl.BlockSpec((GLA_SB, GLA_HDV), lambda h, i: (r0 + i, 2 * GLA_HEADS + h)),
            pl.BlockSpec((GLA_SB, 128), lambda h, i: (r0 + i, 0)),
            pl.BlockSpec((128, GLA_HDK), lambda h, i: (0, h)),
            pl.BlockSpec((1, GLA_HDK), lambda h, i: (0, h)),
            pl.BlockSpec((1, GLA_HDV), lambda h, i: (0, h)),
            pl.BlockSpec((GLA_SB, 1, GLA_HDK, GLA_HDV), lambda h, i: (i, h, 0, 0)),
        ],
        out_specs=[
            pl.BlockSpec((GLA_SB, GLA_HDV), lambda h, i: (i, h)),
            pl.BlockSpec((GLA_SB, 1, GLA_HDK, GLA_HDV), lambda h, i: (i, h, 0, 0)),
        ],
        out_shape=[
            jax.ShapeDtypeStruct((N_SAMPLE, BRANCH), BF16),
            jax.ShapeDtypeStruct(state.shape, F32),
        ],
        compiler_params=_cp(("arbitrary", "arbitrary")),
        name="gla_sample",
    )(qkvr, qkvr, qkvr, qkvr, a_low, wgu, bgu, wn, state)
    return o, st


def _s5_weights(a_re, a_im, b_re, b_im, c_re, c_im, d, log_dt):
    L = S5_CHUNK
    lam_re = jnp.minimum(a_re, -1e-4)
    lam_im = a_im
    dt = jnp.exp(log_dt)[:, None]
    kk = jnp.arange(L + 1, dtype=F32)[:, None, None]
    pow_re = jnp.exp(lam_re * dt * kk) * jnp.cos(lam_im * dt * kk)
    pow_im = jnp.exp(lam_re * dt * kk) * jnp.sin(lam_im * dt * kk)
    lbar_re, lbar_im = pow_re[1], pow_im[1]
    den = lam_re * lam_re + lam_im * lam_im
    f_re = ((lbar_re - 1.0) * lam_re + lbar_im * lam_im) / den
    f_im = (lbar_im * lam_re - (lbar_re - 1.0) * lam_im) / den
    bb_re = f_re[..., None] * b_re - f_im[..., None] * b_im
    bb_im = f_re[..., None] * b_im + f_im[..., None] * b_re
    nb, ng = S5_BLOCKS, S5_LANE_GROUPS
    bb_re_t = jnp.swapaxes(bb_re, 1, 2)
    bb_im_t = jnp.swapaxes(bb_im, 1, 2)
    lb_re = (pow_re[:L, :, None, :] * bb_re_t - pow_im[:L, :, None, :] * bb_im_t).reshape(L, nb, 128, S5_STATE)
    lb_im = (pow_re[:L, :, None, :] * bb_im_t + pow_im[:L, :, None, :] * bb_re_t).reshape(L, nb, 128, S5_STATE)
    cl_re = (c_re[None] * pow_re[:, :, None, :] - c_im[None] * pow_im[:, :, None, :]).reshape(L + 1, nb, 128, S5_STATE)
    cl_im = (c_re[None] * pow_im[:, :, None, :] + c_im[None] * pow_re[:, :, None, :]).reshape(L + 1, nb, 128, S5_STATE)
    lb = jnp.concatenate([lb_re, lb_im], axis=-1)
    cl = jnp.concatenate([cl_re, -cl_im], axis=-1)
    n_steps = int(math.log2(SEQ // L))
    mult = (L * (2.0 ** jnp.arange(n_steps, dtype=F32)))[:, None, None]
    sc_re = jnp.exp(lam_re * dt * mult) * jnp.cos(lam_im * dt * mult)
    sc_im = jnp.exp(lam_re * dt * mult) * jnp.sin(lam_im * dt * mult)

    def state_lanes(x):
        lead = x.shape[:-2]
        return jnp.moveaxis(x.reshape(lead + (nb, ng * S5_STATE)), -2, 0)

    scan_mult = jnp.concatenate([state_lanes(sc_re), state_lanes(sc_im)], axis=-1)
    lbar1 = jnp.concatenate([state_lanes(lbar_re[None]), state_lanes(lbar_im[None])], axis=-1)
    dvec = jnp.tile(d.reshape(nb, 1, 128), (1, 1, L))
    return lb, cl, scan_mult, lbar1, dvec


def _s5_expand(src):
    ng, p = S5_LANE_GROUPS, S5_STATE
    lane = lax.broadcasted_iota(jnp.int32, src.shape, 1)
    other = pltpu.roll(src, p, 1)
    re2 = jnp.where(lane < p, src, other)
    im2 = jnp.where(lane < p, other, src)
    full = jnp.concatenate([re2] * (ng // 2) + [im2] * (ng // 2), axis=1)
    row = lax.broadcasted_iota(jnp.int32, full.shape, 0)
    col = lax.broadcasted_iota(jnp.int32, full.shape, 1)
    same_group = (row >> 4) == ((col >> 6) & (ng - 1))
    return jnp.where(same_group, full, 0.0)


def _s5_lag_kernel(zpow, cpow0):
    return lax.dot_general(zpow, cpow0, _NT, preferred_element_type=F32, precision=HIGHEST)


def _s5_build_weights(lb_ref, cl_ref, t_scr, wz_scr, wc_scr):
    L = S5_CHUNK
    cpow0 = _s5_expand(cl_ref[0, 0])
    t_scr[...] = jnp.zeros_like(t_scr)
    for k in range(L):
        zpow = _s5_expand(lb_ref[k, 0])
        wz_scr[(L - 1 - k) * 128:(L - k) * 128, :] = zpow.astype(BF16)
        wc_scr[k * 128:(k + 1) * 128, :] = _s5_expand(cl_ref[k + 1, 0]).astype(BF16)
        bd = _s5_lag_kernel(zpow, cpow0).astype(BF16)
        for sp in range(L - k):
            t_scr[sp * 128:(sp + 1) * 128, (sp + k) * 128:(sp + k + 1) * 128] = bd


def _s5_prompt_body(u_ref, lb_ref, cl_ref, sm_ref, d_ref, y_ref, fre_ref, fim_ref, t_ref, wz_ref, wc_ref,
                    *, n_steps):
    L = S5_CHUNK
    n_rows = SEQ // L
    ns = S5_LANE_GROUPS * S5_STATE

    @pl.when(pl.program_id(1) == 0)
    def _():
        _s5_build_weights(lb_ref, cl_ref, t_ref, wz_ref, wc_ref)

    v = jnp.concatenate([u_ref[pl.ds(s, n_rows, stride=L), :] for s in range(L)], axis=1)
    vb = v.astype(BF16)
    z = jnp.dot(vb, wz_ref[...], preferred_element_type=F32)
    hr, hi = z[:, :ns], z[:, ns:]
    pos = lax.broadcasted_iota(jnp.int32, (n_rows, ns), 0)
    sm = sm_ref[0]
    for d in range(n_steps):
        sh = 1 << d
        ar, ai = sm[d:d + 1, :ns], sm[d:d + 1, ns:]
        keep = pos >= sh
        pr = jnp.where(keep, pltpu.roll(hr, sh, 0), 0.0)
        pi = jnp.where(keep, pltpu.roll(hi, sh, 0), 0.0)
        hr, hi = hr + ar * pr - ai * pi, hi + ar * pi + ai * pr
    fre_ref[0] = hr[n_rows - 1:n_rows]
    fim_ref[0] = hi[n_rows - 1:n_rows]
    first = pos >= 1
    h_prev = jnp.concatenate([jnp.where(first, pltpu.roll(hr, 1, 0), 0.0),
                              jnp.where(first, pltpu.roll(hi, 1, 0), 0.0)], axis=1)
    y = (jnp.dot(vb, t_ref[...], preferred_element_type=F32)
         + lax.dot_general(h_prev.astype(BF16), wc_ref[...], _NT, preferred_element_type=F32)
         + d_ref[0] * v)
    y = _gelu_tanh(y)
    for s in range(L):
        y_ref[pl.ds(s, n_rows, stride=L), :] = y[:, s * 128:(s + 1) * 128]


def _s5_sample_body(u_ref, lb_ref, cl_ref, l1_ref, d_ref, hre_ref, him_ref, y_ref, sre_ref, sim_ref):
    ns = S5_LANE_GROUPS * S5_STATE
    bbar = _s5_expand(lb_ref[0, 0])
    k0 = _s5_lag_kernel(bbar, _s5_expand(cl_ref[0, 0])).astype(BF16)
    wc0 = _s5_expand(cl_ref[1, 0]).astype(BF16)
    us = u_ref[...]
    usb = us.astype(BF16)
    h0r, h0i = hre_ref[...], him_ref[...]
    l1 = l1_ref[0]
    bu = jnp.dot(usb, bbar.astype(BF16), preferred_element_type=F32)
    sre_ref[...] = l1[:, :ns] * h0r - l1[:, ns:] * h0i + bu[:, :ns]
    sim_ref[...] = l1[:, :ns] * h0i + l1[:, ns:] * h0r + bu[:, ns:]
    h0 = jnp.concatenate([h0r, h0i], axis=1).astype(BF16)
    ys = (jnp.dot(usb, k0, preferred_element_type=F32)
          + lax.dot_general(h0, wc0, _NT, preferred_element_type=F32)
          + d_ref[0] * us)
    y_ref[...] = _gelu_tanh(ys)


def _s5(u, weights, st_re, st_im):
    lb, cl, scan_mult, lbar1, dvec = weights
    L = S5_CHUNK
    ns = S5_LANE_GROUPS * S5_STATE
    n_steps = scan_mult.shape[1]
    wblk = lambda a: pl.BlockSpec((1,) + a.shape[1:], lambda j, b: (j, 0, 0))
    pblk = lambda a: pl.BlockSpec((a.shape[0], 1, 128, 128), lambda j, b: (0, j, 0, 0))
    mat = pltpu.VMEM((L * 128, L * 128), BF16)
    y, f_re, f_im = pl.pallas_call(
        functools.partial(_s5_prompt_body, n_steps=n_steps),
        grid=(S5_BLOCKS, N_PROMPT_SEQ),
        in_specs=[
            pl.BlockSpec((SEQ, 128), lambda j, b: (b, j)),
            pblk(lb), pblk(cl), wblk(scan_mult), wblk(dvec),
        ],
        out_specs=[
            pl.BlockSpec((SEQ, 128), lambda j, b: (b, j)),
            pl.BlockSpec((1, 1, ns), lambda j, b: (b, 0, j)),
            pl.BlockSpec((1, 1, ns), lambda j, b: (b, 0, j)),
        ],
        out_shape=[
            jax.ShapeDtypeStruct((N_PROMPT, BRANCH), F32),
            jax.ShapeDtypeStruct((N_PROMPT_SEQ, 1, S5_GROUPS * S5_STATE), F32),
            jax.ShapeDtypeStruct((N_PROMPT_SEQ, 1, S5_GROUPS * S5_STATE), F32),
        ],
        scratch_shapes=[mat, mat, mat],
        compiler_params=_cp(("arbitrary", "arbitrary")),
        name="s5_prompt",
    )(u, lb, cl, scan_mult, dvec)
    d0 = dvec[:, :, 0:128]
    sblk = lambda a: pl.BlockSpec((1,) + a.shape[1:], lambda j: (j, 0, 0))
    spblk = lambda a: pl.BlockSpec((2, 1, 128, 128), lambda j: (0, j, 0, 0))
    r0 = N_PROMPT // N_SAMPLE
    y_s, s_re, s_im = pl.pallas_call(
        _s5_sample_body,
        grid=(S5_BLOCKS,),
        in_specs=[
            pl.BlockSpec((N_SAMPLE, 128), lambda j: (r0, j)),
            spblk(lb), spblk(cl), sblk(lbar1), sblk(d0),
            pl.BlockSpec((N_SAMPLE, ns), lambda j: (0, j)),
            pl.BlockSpec((N_SAMPLE, ns), lambda j: (0, j)),
        ],
        out_specs=[
            pl.BlockSpec((N_SAMPLE, 128), lambda j: (0, j)),
            pl.BlockSpec((N_SAMPLE, ns), lambda j: (0, j)),
            pl.BlockSpec((N_SAMPLE, ns), lambda j: (0, j)),
        ],
        out_shape=[
            jax.ShapeDtypeStruct((N_SAMPLE, BRANCH), F32),
            jax.ShapeDtypeStruct((N_SAMPLE, S5_GROUPS * S5_STATE), F32),
            jax.ShapeDtypeStruct((N_SAMPLE, S5_GROUPS * S5_STATE), F32),
        ],
        compiler_params=_cp(("arbitrary",)),
        name="s5_sample",
    )(u, lb, cl, lbar1, d0, st_re, st_im)
    return jnp.concatenate([y, y_s], axis=0), f_re, f_im, s_re, s_im


def _merge_body(o_ref, z_ref, w0_ref, w1_ref, g0_ref, g1_ref, out_ref):
    p0 = jnp.dot(o_ref[...], w0_ref[0].astype(BF16), preferred_element_type=F32)
    p1 = jnp.dot(z_ref[...], w1_ref[0].astype(BF16), preferred_element_type=F32)
    out_ref[...] = (g0_ref[...] * p0 + g1_ref[...] * p1).astype(out_ref.dtype)


def _merge(o, z, w_branch, gates):
    ncb = D_MODEL // TN
    return pl.pallas_call(
        _merge_body,
        grid=(N_TOK // TM, ncb),
        in_specs=[
            pl.BlockSpec((TM, BRANCH), lambda i, j: (i, 0)),
            pl.BlockSpec((TM, BRANCH), lambda i, j: (i, 0)),
            pl.BlockSpec((1, BRANCH, TN), lambda i, j: (0, 0, j)),
            pl.BlockSpec((1, BRANCH, TN), lambda i, j: (1, 0, j)),
            pl.BlockSpec((TM, TN), lambda i, j: (i, j)),
            pl.BlockSpec((TM, TN), lambda i, j: (i, ncb + j)),
        ],
        out_specs=pl.BlockSpec((TM, TN), lambda i, j: (i, j)),
        out_shape=jax.ShapeDtypeStruct((N_TOK, D_MODEL), BF16),
        compiler_params=_cp(("arbitrary", "arbitrary")),
        name="merge",
    )(o, z, w_branch, w_branch, gates, gates)


OUT_TM = 416
OUT_TK = 512


def _out_ln_body(a_ref, w_ref, x_ref, g_ref, b_ref, wr_ref, br_ref, o_ref, ids_ref, wts_ref):
    kk = pl.program_id(1)
    part = jnp.dot(a_ref[...], w_ref[...], preferred_element_type=F32)

    @pl.when(kk == 0)
    def _():
        o_ref[...] = part

    @pl.when(kk > 0)
    def _():
        o_ref[...] += part

    @pl.when(kk == pl.num_programs(1) - 1)
    def _():
        x1 = _layer_norm(ALPHA * x_ref[...] + o_ref[...]) * g_ref[...] + b_ref[...]
        o_ref[...] = x1
        x_hi = x1.astype(BF16)
        x_mid = (x1 - x_hi.astype(F32)).astype(BF16)
        wr = wr_ref[...]
        both = jnp.dot(x_hi, wr, preferred_element_type=F32)
        logits = (both[:, :128] + both[:, 128:]
                  + jnp.dot(x_mid, wr[:, :128], preferred_element_type=F32) + br_ref[...])
        _route(logits, ids_ref, wts_ref)


def _out_ln(pre, w_out_bf, x_all, g, b, w_r, b_r):
    w_hi = w_r.astype(BF16)
    w_mid = (w_r - w_hi.astype(F32)).astype(BF16)
    wr = jnp.concatenate([w_hi, w_mid], axis=1)
    row_blk = pl.BlockSpec((OUT_TM, 128), lambda i, k: (i, 0))
    return pl.pallas_call(
        _out_ln_body,
        grid=(N_TOK // OUT_TM, D_MODEL // OUT_TK),
        in_specs=[
            pl.BlockSpec((OUT_TM, OUT_TK), lambda i, k: (i, k)),
            pl.BlockSpec((OUT_TK, D_MODEL), lambda i, k: (k, 0)),
            pl.BlockSpec((OUT_TM, D_MODEL), lambda i, k: (i, 0)),
            pl.BlockSpec((1, D_MODEL), lambda i, k: (0, 0)),
            pl.BlockSpec((1, D_MODEL), lambda i, k: (0, 0)),
            pl.BlockSpec((D_MODEL, 256), lambda i, k: (0, 0)),
            pl.BlockSpec((1, 128), lambda i, k: (0, 0)),
        ],
        out_specs=[pl.BlockSpec((OUT_TM, D_MODEL), lambda i, k: (i, 0)), row_blk, row_blk],
        out_shape=[jax.ShapeDtypeStruct((N_TOK, D_MODEL), F32),
                   jax.ShapeDtypeStruct((N_TOK, 128), jnp.int32),
                   jax.ShapeDtypeStruct((N_TOK, 128), F32)],
        compiler_params=_cp(("arbitrary", "arbitrary")),
        name="out_ln1_router",
    )(pre, w_out_bf, x_all, g, b, wr, b_r)


def _route(logits, ids_ref, wts_ref):
    lane = lax.broadcasted_iota(jnp.int32, logits.shape, 1)
    neg = -jnp.inf
    big = 1 << 20
    gl = jnp.where(lane < 8, logits, neg)
    gmax = jnp.max(gl, axis=1, keepdims=True)
    gidx = jnp.min(jnp.where(gl == gmax, lane, big), axis=1, keepdims=True)
    gprob = 1.0 / jnp.sum(jnp.exp(gl - gmax), axis=1, keepdims=True)
    in_group = (lane >= 8) & (lane < 8 + N_EXPERTS) & (((lane - 8) >> 3) == gidx)
    el = jnp.where(in_group, logits, neg)
    v1 = jnp.max(el, axis=1, keepdims=True)
    i1 = jnp.min(jnp.where(el == v1, lane, big), axis=1, keepdims=True)
    el2 = jnp.where(lane == i1, neg, el)
    v2 = jnp.max(el2, axis=1, keepdims=True)
    i2 = jnp.min(jnp.where(el2 == v2, lane, big), axis=1, keepdims=True)
    e2 = jnp.exp(v2 - v1)
    w1 = gprob / (1.0 + e2)
    w2 = gprob * e2 / (1.0 + e2)
    ids_ref[...] = jnp.where(lane == 0, i1 - 8, jnp.where(lane == 1, i2 - 8, 0))
    wts_ref[...] = jnp.where(lane == 0, w1, jnp.where(lane == 1, w2, 0.0))


N_ASSIGN = 2 * N_TOK
MOE_NBLOCKS = (N_ASSIGN + N_EXPERTS * (MOE_BLOCK - 1) + MOE_BLOCK - 1) // MOE_BLOCK
MOE_ROWS = MOE_NBLOCKS * MOE_BLOCK
MOE_HC = MOE_HIDDEN // MOE_HSPLIT


def _row_copy(src_hbm, row, dst, dst_row, sem):
    return pltpu.make_async_copy(src_hbm.at[pl.ds(row, 1), :], dst.at[pl.ds(dst_row, 1), :], sem)


MOE_MACRO = 4
MOE_NMACRO = N_ASSIGN // (MOE_MACRO * MOE_BLOCK) + N_EXPERTS + 1
MOE_DCOLS = 1024


def _block_copy(acc, t, o_hbm, blk, sem):
    return pltpu.make_async_copy(acc.at[pl.ds(pl.multiple_of(t * MOE_BLOCK, MOE_BLOCK), MOE_BLOCK), :],
                                 o_hbm.at[pl.ds(pl.multiple_of(blk * MOE_BLOCK, MOE_BLOCK), MOE_BLOCK), :], sem)


def _moe_body(mexp_ref, mstart_ref, mnsub_ref, tok_ref, x_hbm, wg_ref, wu_ref, wd_ref, o_hbm,
              stage, xb, acc, gsem, osem):
    del mexp_ref
    m = pl.program_id(0)
    h = pl.program_id(1)
    nsub = mnsub_ref[m]
    start = mstart_ref[m]
    last_m = pl.num_programs(0) - 1

    def wait_blocks(n):
        def done(t, c):
            _block_copy(acc, t, o_hbm, 0, osem).wait()
            return c

        lax.fori_loop(0, n, done, 0)

    @pl.when((h == 0) & (m > 0))
    def _():
        wait_blocks(mnsub_ref[jnp.maximum(m - 1, 0)])

    def gather_start(t, slot):
        base = (start + t) * MOE_BLOCK
        for r in range(MOE_BLOCK):
            _row_copy(x_hbm, tok_ref[base + r], stage.at[slot], r, gsem.at[slot]).start(priority=1)

    @pl.when((h == 0) & (nsub > 0))
    def _():
        gather_start(0, 0)

        def sub(t, c):
            slot = t % 2

            @pl.when(t + 1 < nsub)
            def _():
                gather_start(t + 1, 1 - slot)

            pltpu.make_async_copy(x_hbm.at[pl.ds(0, MOE_BLOCK), :], stage.at[slot], gsem.at[slot]).wait()
            xb[pl.ds(pl.multiple_of(t * MOE_BLOCK, MOE_BLOCK), MOE_BLOCK), :] = stage[slot].astype(BF16)
            return c

        lax.fori_loop(0, nsub, sub, 0)

    for ns in range(1, MOE_MACRO + 1):
        @pl.when(nsub == ns)
        def _(ns=ns):
            rows = ns * MOE_BLOCK

            @pl.when(h == 0)
            def _():
                acc[0:rows, :] = jnp.zeros((rows, D_MODEL), F32)

            x = xb[0:rows, :]
            hg = jnp.dot(x, wg_ref[0].astype(BF16), preferred_element_type=F32)
            hu = jnp.dot(x, wu_ref[0].astype(BF16), preferred_element_type=F32)
            hh = (hg * _sigmoid(hg) * hu).astype(BF16)
            wd = wd_ref[0].astype(BF16)
            for cc in range(D_MODEL // MOE_DCOLS):
                cols = slice(cc * MOE_DCOLS, (cc + 1) * MOE_DCOLS)
                acc[0:rows, cols] += jnp.dot(hh, wd[:, cols], preferred_element_type=F32)

    @pl.when((h == pl.num_programs(1) - 1) & (nsub > 0))
    def _():
        def put(t, c):
            _block_copy(acc, t, o_hbm, start + t, osem).start()
            return c

        lax.fori_loop(0, nsub, put, 0)

        @pl.when(m == last_m)
        def _():
            wait_blocks(nsub)

    @pl.when((h == 0) & (nsub == 0))
    def _():
        acc[0:MOE_BLOCK, :] = jnp.zeros((MOE_BLOCK, D_MODEL), F32)
        for t in range(MOE_MACRO):
            @pl.when(start + t < MOE_NBLOCKS)
            def _(t=t):
                cp = _block_copy(acc, 0, o_hbm, start + t, osem)
                cp.start()
                cp.wait()


def _moe_experts(x1, wg, wu, wd, mexp, mstart, mnsub, row_tok):
    grid_spec = pltpu.PrefetchScalarGridSpec(
        num_scalar_prefetch=4,
        grid=(MOE_NMACRO, MOE_HSPLIT),
        in_specs=[
            pl.BlockSpec(memory_space=pl.ANY),
            pl.BlockSpec((1, D_MODEL, MOE_HC), lambda m, h, me, ms, mn, rt: (me[m], 0, h)),
            pl.BlockSpec((1, D_MODEL, MOE_HC), lambda m, h, me, ms, mn, rt: (me[m], 0, h)),
            pl.BlockSpec((1, MOE_HC, D_MODEL), lambda m, h, me, ms, mn, rt: (me[m], h, 0)),
        ],
        out_specs=pl.BlockSpec(memory_space=pl.ANY),
        scratch_shapes=[pltpu.VMEM((2, MOE_BLOCK, D_MODEL), F32),
                        pltpu.VMEM((MOE_MACRO * MOE_BLOCK, D_MODEL), BF16),
                        pltpu.VMEM((MOE_MACRO * MOE_BLOCK, D_MODEL), F32),
                        pltpu.SemaphoreType.DMA((2,)), pltpu.SemaphoreType.DMA(())],
    )
    return pl.pallas_call(
        _moe_body,
        grid_spec=grid_spec,
        out_shape=jax.ShapeDtypeStruct((MOE_ROWS, D_MODEL), F32),
        compiler_params=_cp(("arbitrary", "arbitrary")),
        name="moe_experts",
    )(mexp, mstart, mnsub, row_tok, x1, wg, wu, wd)


CMB_TM = 128
CMB_PROMPT_TILES = N_PROMPT // CMB_TM


def _combine_body(pos_ref, eo_hbm, wts_ref, x1_ref, g_ref, b_ref, yp_ref, ys_ref, buf, sem):
    i = pl.program_id(0)
    slot = i % 2

    def fetch(tile, s):
        for r in range(CMB_TM):
            a = 2 * (tile * CMB_TM + r)
            _row_copy(eo_hbm, pos_ref[a], buf.at[s, 0], r, sem.at[s]).start(priority=0)
            _row_copy(eo_hbm, pos_ref[a + 1], buf.at[s, 1], r, sem.at[s]).start(priority=1)

    @pl.when(i == 0)
    def _():
        fetch(0, 0)

    @pl.when(i + 1 < pl.num_programs(0))
    def _():
        fetch(i + 1, 1 - slot)

    for k in range(2):
        pltpu.make_async_copy(eo_hbm.at[pl.ds(0, CMB_TM), :], buf.at[slot, k], sem.at[slot]).wait()
    w = wts_ref[...]
    y = w[:, 0:1] * buf[slot, 0] + w[:, 1:2] * buf[slot, 1]
    x2 = _layer_norm(ALPHA * x1_ref[...] + y) * g_ref[...] + b_ref[...]

    @pl.when(i < CMB_PROMPT_TILES)
    def _():
        yp_ref[...] = x2

    @pl.when(i >= CMB_PROMPT_TILES)
    def _():
        ys_ref[...] = x2


def _combine(pos, eo, wts, x1, g, b):
    grid_spec = pltpu.PrefetchScalarGridSpec(
        num_scalar_prefetch=1,
        grid=(N_TOK // CMB_TM,),
        in_specs=[
            pl.BlockSpec(memory_space=pl.ANY),
            pl.BlockSpec((CMB_TM, 128), lambda i, p: (i, 0)),
            pl.BlockSpec((CMB_TM, D_MODEL), lambda i, p: (i, 0)),
            pl.BlockSpec((1, D_MODEL), lambda i, p: (0, 0)),
            pl.BlockSpec((1, D_MODEL), lambda i, p: (0, 0)),
        ],
        out_specs=[
            pl.BlockSpec((CMB_TM, D_MODEL), lambda i, p: (jnp.minimum(i, CMB_PROMPT_TILES - 1), 0)),
            pl.BlockSpec((CMB_TM, D_MODEL), lambda i, p: (0, 0)),
        ],
        scratch_shapes=[pltpu.VMEM((2, 2, CMB_TM, D_MODEL), F32), pltpu.SemaphoreType.DMA((2,))],
    )
    return pl.pallas_call(
        _combine_body,
        grid_spec=grid_spec,
        out_shape=[jax.ShapeDtypeStruct((N_PROMPT, D_MODEL), F32),
                   jax.ShapeDtypeStruct((N_SAMPLE, D_MODEL), F32)],
        compiler_params=_cp(("arbitrary",)),
        name="combine_ln2",
    )(pos, eo, wts, x1, g, b)


def _route_positions(ids):
    eid = ids[:, :2].reshape(-1)
    onehot = (eid[:, None] == jnp.arange(N_EXPERTS, dtype=jnp.int32)[None, :]).astype(jnp.int32)
    csum = jnp.cumsum(onehot, axis=0)
    rank = jnp.take_along_axis(csum, eid[:, None], axis=1)[:, 0] - 1
    counts = csum[-1]
    nblk_e = (counts + MOE_BLOCK - 1) // MOE_BLOCK
    bend = jnp.cumsum(nblk_e)
    bstart = bend - nblk_e
    pos = bstart[eid] * MOE_BLOCK + rank
    row_tok = jnp.zeros((MOE_ROWS,), jnp.int32).at[pos].set(jnp.arange(N_ASSIGN, dtype=jnp.int32) // 2)
    nstep_e = (nblk_e + MOE_MACRO - 1) // MOE_MACRO
    send = jnp.cumsum(nstep_e)
    n_steps, n_blocks = send[-1], bend[-1]
    m = jnp.arange(MOE_NMACRO, dtype=jnp.int32)
    e_of_m = jnp.minimum(jnp.searchsorted(send, m, side='right'), N_EXPERTS - 1).astype(jnp.int32)
    local = m - (send - nstep_e)[e_of_m]
    valid = m < n_steps
    mexp = jnp.where(valid, e_of_m, e_of_m[jnp.maximum(n_steps - 1, 0)])
    mstart = jnp.where(valid, bstart[e_of_m] + MOE_MACRO * local, n_blocks + MOE_MACRO * (m - n_steps))
    mnsub = jnp.where(valid, jnp.clip(nblk_e[e_of_m] - MOE_MACRO * local, 0, MOE_MACRO), 0)
    return (pos.astype(jnp.int32), row_tok, mexp.astype(jnp.int32), mstart.astype(jnp.int32),
            mnsub.astype(jnp.int32))


def kernel(x_prompt, x_sample, state_gla, state_s5_re, state_s5_im, w_in, w_gla_gate_up, b_gla_gate_up, w_gla_norm, s5_a_re, s5_a_im, s5_b_re, s5_b_im, s5_c_re, s5_c_im, s5_d, s5_log_dt, w_s5_glu, b_s5_glu, w_branch, w_out, ln1_g, ln1_b, w_router_group, b_router_group, w_router_expert, b_router_expert, w_moe_gate, w_moe_up, w_moe_down, ln2_g, ln2_b):
    x_all = jnp.concatenate([x_prompt.reshape(N_PROMPT, D_MODEL), x_sample.reshape(N_SAMPLE, D_MODEL)], axis=0)
    x_bf = x_all.astype(BF16)

    w_in_t = w_in.T
    qkvr = _mm_t(x_bf, w_in_t, row0=0, ncols=COL_A, tn=TN, out_dtype=F32, name="proj_qkvr")
    a_low = _mm_t(x_bf, w_in_t, row0=COL_A, ncols=128, tn=128, out_dtype=F32, name="proj_a")
    u = _mm_t(x_bf, w_in_t, row0=COL_U, ncols=BRANCH, tn=TN, out_dtype=F32, name="proj_u")
    gates = _mm_t(x_bf, w_in_t, row0=COL_G, ncols=2 * D_MODEL, tn=TN, out_dtype=F32, name="proj_gates",
                  epilogue=_sigmoid)

    wgu = jnp.pad(w_gla_gate_up, ((0, 128 - GLA_RANK), (0, 0)))
    bgu = b_gla_gate_up.reshape(1, GLA_DK)
    wn = w_gla_norm.reshape(1, BRANCH)
    o_p, gla_p_t = _gla_prompt(qkvr, a_low, wgu, bgu, wn)
    o_s, gla_s = _gla_sample(qkvr, a_low, wgu, bgu, wn, state_gla)
    o_all = jnp.concatenate([o_p, o_s], axis=0)
    gla_p = jnp.swapaxes(gla_p_t, 2, 3)

    s5w = _s5_weights(s5_a_re, s5_a_im, s5_b_re, s5_b_im, s5_c_re, s5_c_im, s5_d, s5_log_dt)
    y, s5_re_p, s5_im_p, s5_re_s, s5_im_s = _s5(
        u, s5w, state_s5_re.reshape(N_SAMPLE, -1), state_s5_im.reshape(N_SAMPLE, -1))
    z = _mm(y, w_s5_glu, col0=0, ncols=BRANCH, tn=TN, out_dtype=BF16, name="s5_glu",
            epilogue=lambda acc, yt, bt: yt * _sigmoid(acc + bt),
            extra=(y, b_s5_glu.reshape(1, BRANCH)),
            extra_specs=(pl.BlockSpec((TM, TN), lambda i, j: (i, j)), pl.BlockSpec((1, TN), lambda i, j: (0, j))))

    pre = _merge(o_all, z, w_branch, gates)
    w_r = jnp.concatenate([w_router_group,
                           jnp.moveaxis(w_router_expert, 0, 1).reshape(D_MODEL, N_EXPERTS),
                           jnp.zeros((D_MODEL, 128 - 8 - N_EXPERTS), F32)], axis=1)
    b_r = jnp.concatenate([b_router_group, b_router_expert.reshape(-1),
                           jnp.zeros((128 - 8 - N_EXPERTS,), F32)]).reshape(1, 128)
    x1, ids, wts = _out_ln(pre, w_out.astype(BF16), x_all, ln1_g.reshape(1, D_MODEL), ln1_b.reshape(1, D_MODEL),
                           w_r, b_r)

    pos, row_tok, mexp, mstart, mnsub = _route_positions(ids)
    eo = _moe_experts(x1,
                      w_moe_gate.reshape(N_EXPERTS, D_MODEL, MOE_HIDDEN),
                      w_moe_up.reshape(N_EXPERTS, D_MODEL, MOE_HIDDEN),
                      w_moe_down.reshape(N_EXPERTS, MOE_HIDDEN, D_MODEL),
                      mexp, mstart, mnsub, row_tok)
    y_p, y_s = _combine(pos, eo, wts, x1, ln2_g.reshape(1, D_MODEL), ln2_b.reshape(1, D_MODEL))

    return (y_p.reshape(N_PROMPT_SEQ, SEQ, D_MODEL), y_s.reshape(N_SAMPLE, 1, D_MODEL),
            gla_p,
            s5_re_p.reshape(N_PROMPT_SEQ, S5_GROUPS, S5_STATE), s5_im_p.reshape(N_PROMPT_SEQ, S5_GROUPS, S5_STATE),
            gla_s,
            s5_re_s.reshape(N_SAMPLE, S5_GROUPS, S5_STATE), s5_im_s.reshape(N_SAMPLE, S5_GROUPS, S5_STATE))
```

```python
import functools
import math

import jax
import jax.numpy as jnp
import numpy as np
from jax import lax
from jax.experimental import pallas as pl
from jax.experimental.pallas import tpu as pltpu

F32 = jnp.float32
BF16 = jnp.bfloat16
HIGHEST = lax.Precision.HIGHEST

D_MODEL = 4096
N_PROMPT_SEQ = 4
SEQ = 2048
N_SAMPLE = 128
N_PROMPT = N_PROMPT_SEQ * SEQ
N_TOK = N_PROMPT + N_SAMPLE
BRANCH = D_MODEL // 2
GLA_HEADS = 4
GLA_DK = D_MODEL // 4
GLA_HDK = GLA_DK // GLA_HEADS
GLA_HDV = BRANCH // GLA_HEADS
GLA_RANK = 16
GLA_TAU = 16.0
GLA_CHUNK = 64
GLA_LEVELS = 6
S5_GROUP = 16
S5_GROUPS = BRANCH // S5_GROUP
S5_STATE = 64
S5_CHUNK = 8
S5_LANE_GROUPS = 8
S5_BLOCKS = S5_GROUPS // S5_LANE_GROUPS
N_EXPERTS = 64
MOE_HIDDEN = D_MODEL // 8
MOE_BLOCK = 128
MOE_HSPLIT = 2
LN_EPS = 1e-5
ALPHA = 2.0 ** 0.25
COL_A = 6144
COL_U = 6160
COL_G = 8208
IN_COLS = 16400

VMEM_LIMIT = 56 * 1024 * 1024
TM = 1040
TN = 512


_NT = (((1,), (1,)), ((), ()))
_TN = (((0,), (0,)), ((), ()))


def _cp(sem):
    return pltpu.CompilerParams(dimension_semantics=sem, vmem_limit_bytes=VMEM_LIMIT)


def _sigmoid(x):
    return 1.0 / (1.0 + jnp.exp(-x))


def _log_sigmoid(z):
    return jnp.minimum(z, 0.0) - jnp.log1p(jnp.exp(-jnp.abs(z)))


def _gelu_tanh(x):
    return 0.5 * x * (1.0 + jnp.tanh(math.sqrt(2.0 / math.pi) * (x + 0.044715 * (x * x * x))))


def _layer_norm(x):
    mu = jnp.mean(x, axis=-1, keepdims=True)
    xc = x - mu
    var = jnp.mean(xc * xc, axis=-1, keepdims=True)
    return xc * lax.rsqrt(var + LN_EPS)


def _mm_body(a_ref, w_ref, *rest, epilogue):
    *extra, o_ref = rest
    acc = jnp.dot(a_ref[...].astype(BF16), w_ref[...].astype(BF16), preferred_element_type=F32)
    o_ref[...] = epilogue(acc, *[e[...] for e in extra]).astype(o_ref.dtype)


def _mm(a, w, *, col0, ncols, tn, out_dtype, name, epilogue=lambda acc: acc, extra=(), extra_specs=()):
    m, k = a.shape
    cb0 = col0 // tn
    return pl.pallas_call(
        functools.partial(_mm_body, epilogue=epilogue),
        grid=(m // TM, ncols // tn),
        in_specs=[pl.BlockSpec((TM, k), lambda i, j: (i, 0)),
                  pl.BlockSpec((k, tn), lambda i, j: (0, cb0 + j)),
                  *extra_specs],
        out_specs=pl.BlockSpec((TM, tn), lambda i, j: (i, j)),
        out_shape=jax.ShapeDtypeStruct((m, ncols), out_dtype),
        compiler_params=_cp(("arbitrary", "arbitrary")),
        name=name,
    )(a, w, *extra)


def _mm_t_body(a_ref, wt_ref, o_ref, *, epilogue):
    acc = lax.dot_general(a_ref[...], wt_ref[...].astype(BF16), _NT, preferred_element_type=F32)
    o_ref[...] = epilogue(acc).astype(o_ref.dtype)


def _mm_t(a, wt, *, row0, ncols, tn, out_dtype, name, epilogue=lambda acc: acc):
    m, k = a.shape
    return pl.pallas_call(
        functools.partial(_mm_t_body, epilogue=epilogue),
        grid=(m // TM, ncols // tn),
        in_specs=[pl.BlockSpec((TM, k), lambda i, j: (i, 0)),
                  pl.BlockSpec((pl.Element(tn), pl.Element(k)),
                               lambda i, j: (pl.multiple_of(row0 + j * tn, 8), 0))],
        out_specs=pl.BlockSpec((TM, tn), lambda i, j: (i, j)),
        out_shape=jax.ShapeDtypeStruct((m, ncols), out_dtype),
        compiler_params=_cp(("arbitrary", "arbitrary")),
        name=name,
    )(a, wt)


def _gla_coeff_matrix():
    c = GLA_CHUNK
    t = np.arange(c)[:, None]
    u = np.arange(c)[None, :]
    blocks = [(u <= t), (u > t)]
    for lvl in range(1, GLA_LEVELS + 1):
        m = 1 << lvl
        half = m // 2
        mid = (t // m) * m + half - 1
        lower = (t % m) >= half
        blocks.append(np.where(lower, (u > mid) & (u <= t), (u > t) & (u <= mid)))
    return np.concatenate(blocks, axis=0).astype(np.float32)


def _gla_level_masks():
    c = GLA_CHUNK
    t = lax.broadcasted_iota(jnp.int32, (c, c), 0)
    s = lax.broadcasted_iota(jnp.int32, (c, c), 1)
    masks = []
    for lvl in range(1, GLA_LEVELS + 1):
        m = 1 << lvl
        half = m // 2
        masks.append(((t >> lvl) == (s >> lvl)) & ((t & (m - 1)) >= half) & ((s & (m - 1)) < half))
    return masks


def _gla_out_norm(o, r, wn):
    return _layer_norm(o) * wn * (r * _sigmoid(r))


def _gla_prompt_body(q_ref, k_ref, v_ref, r_ref, a_ref, wgu_ref, bgu_ref, wn_ref, cm_ref,
                     o_ref, st_ref, s_scr, *, n_sub):
    c = GLA_CHUNK

    @pl.when(pl.program_id(2) == 0)
    def _():
        s_scr[...] = jnp.zeros_like(s_scr)

    masks = _gla_level_masks()
    wgu = wgu_ref[...]
    bgu = bgu_ref[...]
    wn = wn_ref[...]
    cm = cm_ref[...]
    nt, tn = _NT, _TN

    def chunk(ci, carry):
        rows = pl.ds(pl.multiple_of(ci * c, c), c)
        q = q_ref[rows, :] * (GLA_HDK ** -0.5)
        k = k_ref[rows, :]
        v = v_ref[rows, :]
        z = jnp.dot(a_ref[rows, :], wgu, preferred_element_type=F32, precision=HIGHEST) + bgu
        g = _log_sigmoid(z) * (1.0 / GLA_TAU)
        g_hi = g.astype(BF16)
        r1 = g - g_hi.astype(F32)
        g_mid = r1.astype(BF16)
        g_lo = (r1 - g_mid.astype(F32)).astype(BF16)
        g3 = jnp.concatenate([g_hi, g_mid, g_lo], axis=0)
        f = jnp.exp(jnp.dot(cm, g3, preferred_element_type=F32))
        st = s_scr[...]
        vb = v.astype(BF16)
        o = lax.dot_general((q * f[0:c]).astype(BF16), st.astype(BF16), nt, preferred_element_type=F32)
        scores = jnp.zeros((c, c), F32)
        for lvl in range(GLA_LEVELS):
            fl = f[(2 + lvl) * c:(3 + lvl) * c]
            p = lax.dot_general((q * fl).astype(BF16), (k * fl).astype(BF16), nt, preferred_element_type=F32)
            scores = scores + jnp.where(masks[lvl], p, 0.0)
        diag = jnp.sum(q * k, axis=1, keepdims=True)
        o = o + jnp.dot(scores.astype(BF16), vb, preferred_element_type=F32) + diag * v
        kd = (k * f[c:2 * c]).astype(BF16)
        s_scr[...] = st * f[c - 1:c, :] + lax.dot_general(vb, kd, tn, preferred_element_type=F32)
        o_ref[rows, :] = _gla_out_norm(o, r_ref[rows, :], wn).astype(o_ref.dtype)
        return carry

    lax.fori_loop(0, n_sub, chunk, 0, unroll=True)

    @pl.when(pl.program_id(2) == pl.num_programs(2) - 1)
    def _():
        st_ref[0, 0] = s_scr[...]


def _gla_prompt(qkvr, a_low, wgu, bgu, wn):
    n_sub = 4
    tt = GLA_CHUNK * n_sub
    nt_steps = SEQ // tt
    cm = jnp.asarray(np.tile(_gla_coeff_matrix(), (1, 3)), dtype=BF16)
    rows = lambda b, h, c: b * nt_steps + c
    o, st = pl.pallas_call(
        functools.partial(_gla_prompt_body, n_sub=n_sub),
        grid=(N_PROMPT_SEQ, GLA_HEADS, nt_steps),
        in_specs=[
            pl.BlockSpec((tt, GLA_HDK), lambda b, h, c: (rows(b, h, c), h)),
            pl.BlockSpec((tt, GLA_HDK), lambda b, h, c: (rows(b, h, c), GLA_HEADS + h)),
            pl.BlockSpec((tt, GLA_HDV), lambda b, h, c: (rows(b, h, c), GLA_HEADS + h)),
            pl.BlockSpec((tt, GLA_HDV), lambda b, h, c: (rows(b, h, c), 2 * GLA_HEADS + h)),
            pl.BlockSpec((tt, 128), lambda b, h, c: (rows(b, h, c), 0)),
            pl.BlockSpec((128, GLA_HDK), lambda b, h, c: (0, h)),
            pl.BlockSpec((1, GLA_HDK), lambda b, h, c: (0, h)),
            pl.BlockSpec((1, GLA_HDV), lambda b, h, c: (0, h)),
            pl.BlockSpec(cm.shape, lambda b, h, c: (0, 0)),
        ],
        out_specs=[
            pl.BlockSpec((tt, GLA_HDV), lambda b, h, c: (rows(b, h, c), h)),
            pl.BlockSpec((1, 1, GLA_HDV, GLA_HDK), lambda b, h, c: (b, h, 0, 0)),
        ],
        out_shape=[
            jax.ShapeDtypeStruct((N_PROMPT, BRANCH), BF16),
            jax.ShapeDtypeStruct((N_PROMPT_SEQ, GLA_HEADS, GLA_HDV, GLA_HDK), F32),
        ],
        scratch_shapes=[pltpu.VMEM((GLA_HDV, GLA_HDK), F32)],
        compiler_params=_cp(("arbitrary", "arbitrary", "arbitrary")),
        name="gla_prompt",
    )(qkvr, qkvr, qkvr, qkvr, a_low, wgu, bgu, wn, cm)
    return o, st


GLA_SB = 16


def _gla_sample_body(q_ref, k_ref, v_ref, r_ref, a_ref, wgu_ref, bgu_ref, wn_ref, s_ref, o_ref, so_ref):
    q = q_ref[...] * (GLA_HDK ** -0.5)
    k = k_ref[...]
    v = v_ref[...]
    z = jnp.dot(a_ref[...], wgu_ref[...], preferred_element_type=F32, precision=HIGHEST) + bgu_ref[...]
    eg = jnp.exp(_log_sigmoid(z) * (1.0 / GLA_TAU))
    qe = (q * eg).astype(BF16)
    eg_t = eg.T
    k_t = k.T
    rows = []
    for n in range(GLA_SB):
        s0 = s_ref[n, 0]
        rows.append(jnp.dot(qe, s0.astype(BF16), preferred_element_type=F32)[n:n + 1])
        so_ref[n, 0] = s0 * eg_t[:, n:n + 1] + k_t[:, n:n + 1] * v[n:n + 1, :]
    o = jnp.concatenate(rows, axis=0) + jnp.sum(q * k, axis=1, keepdims=True) * v
    o_ref[...] = _gla_out_norm(o, r_ref[...], wn_ref[...]).astype(o_ref.dtype)


def _gla_sample(qkvr, a_low, wgu, bgu, wn, state):
    r0 = N_PROMPT // GLA_SB
    o, st = pl.pallas_call(
        _gla_sample_body,
        grid=(GLA_HEADS, N_SAMPLE // GLA_SB),
        in_specs=[
            pl.BlockSpec((GLA_SB, GLA_HDK), lambda h, i: (r0 + i, h)),
            pl.BlockSpec((GLA_SB, GLA_HDK), lambda h, i: (r0 + i, GLA_HEADS + h)),
            pl.BlockSpec((GLA_SB, GLA_HDV), lambda h, i: (r0 + i, GLA_HEADS + h)),
            pl.BlockSpec((GLA_SB, GLA_HDV), lambda h, i: (r0 + i, 2 * GLA_HEADS + h)),
            pl.BlockSpec((GLA_SB, 128), lambda h, i: (r0 + i, 0)),
            pl.BlockSpec((128, GLA_HDK), lambda h, i: (0, h)),
            pl.BlockSpec((1, GLA_HDK), lambda h, i: (0, h)),
            pl.BlockSpec((1, GLA_HDV), lambda h, i: (0, h)),
            pl.BlockSpec((GLA_SB, 1, GLA_HDK, GLA_HDV), lambda h, i: (i, h, 0, 0)),
        ],
        out_specs=[
            pl.BlockSpec((GLA_SB, GLA_HDV), lambda h, i: (i, h)),
            pl.BlockSpec((GLA_SB, 1, GLA_HDK, GLA_HDV), lambda h, i: (i, h, 0, 0)),
        ],
        out_shape=[
            jax.ShapeDtypeStruct((N_SAMPLE, BRANCH), BF16),
            jax.ShapeDtypeStruct(state.shape, F32),
        ],
        compiler_params=_cp(("arbitrary", "arbitrary")),
        name="gla_sample",
    )(qkvr, qkvr, qkvr, qkvr, a_low, wgu, bgu, wn, state)
    return o, st


def _s5_weights(a_re, a_im, b_re, b_im, c_re, c_im, d, log_dt):
    L = S5_CHUNK
    lam_re = jnp.minimum(a_re, -1e-4)
    lam_im = a_im
    dt = jnp.exp(log_dt)[:, None]
    kk = jnp.arange(L + 1, dtype=F32)[:, None, None]
    pow_re = jnp.exp(lam_re * dt * kk) * jnp.cos(lam_im * dt * kk)
    pow_im = jnp.exp(lam_re * dt * kk) * jnp.sin(lam_im * dt * kk)
    lbar_re, lbar_im = pow_re[1], pow_im[1]
    den = lam_re * lam_re + lam_im * lam_im
    f_re = ((lbar_re - 1.0) * lam_re + lbar_im * lam_im) / den
    f_im = (lbar_im * lam_re - (lbar_re - 1.0) * lam_im) / den
    bb_re = f_re[..., None] * b_re - f_im[..., None] * b_im
    bb_im = f_re[..., None] * b_im + f_im[..., None] * b_re
    nb, ng = S5_BLOCKS, S5_LANE_GROUPS
    bb_re_t = jnp.swapaxes(bb_re, 1, 2)
    bb_im_t = jnp.swapaxes(bb_im, 1, 2)
    lb_re = (pow_re[:L, :, None, :] * bb_re_t - pow_im[:L, :, None, :] * bb_im_t).reshape(L, nb, 128, S5_STATE)
    lb_im = (pow_re[:L, :, None, :] * bb_im_t + pow_im[:L, :, None, :] * bb_re_t).reshape(L, nb, 128, S5_STATE)
    cl_re = (c_re[None] * pow_re[:, :, None, :] - c_im[None] * pow_im[:, :, None, :]).reshape(L + 1, nb, 128, S5_STATE)
    cl_im = (c_re[None] * pow_im[:, :, None, :] + c_im[None] * pow_re[:, :, None, :]).reshape(L + 1, nb, 128, S5_STATE)
    lb = jnp.concatenate([lb_re, lb_im], axis=-1)
    cl = jnp.concatenate([cl_re, -cl_im], axis=-1)
    n_steps = int(math.log2(SEQ // L))
    mult = (L * (2.0 ** jnp.arange(n_steps, dtype=F32)))[:, None, None]
    sc_re = jnp.exp(lam_re * dt * mult) * jnp.cos(lam_im * dt * mult)
    sc_im = jnp.exp(lam_re * dt * mult) * jnp.sin(lam_im * dt * mult)

    def state_lanes(x):
        lead = x.shape[:-2]
        return jnp.moveaxis(x.reshape(lead + (nb, ng * S5_STATE)), -2, 0)

    scan_mult = jnp.concatenate([state_lanes(sc_re), state_lanes(sc_im)], axis=-1)
    lbar1 = jnp.concatenate([state_lanes(lbar_re[None]), state_lanes(lbar_im[None])], axis=-1)
    dvec = jnp.tile(d.reshape(nb, 1, 128), (1, 1, L))
    return lb, cl, scan_mult, lbar1, dvec


def _s5_expand(src):
    ng, p = S5_LANE_GROUPS, S5_STATE
    lane = lax.broadcasted_iota(jnp.int32, src.shape, 1)
    other = pltpu.roll(src, p, 1)
    re2 = jnp.where(lane < p, src, other)
    im2 = jnp.where(lane < p, other, src)
    full = jnp.concatenate([re2] * (ng // 2) + [im2] * (ng // 2), axis=1)
    row = lax.broadcasted_iota(jnp.int32, full.shape, 0)
    col = lax.broadcasted_iota(jnp.int32, full.shape, 1)
    same_group = (row >> 4) == ((col >> 6) & (ng - 1))
    return jnp.where(same_group, full, 0.0)


def _s5_lag_kernel(zpow, cpow0):
    return lax.dot_general(zpow, cpow0, _NT, preferred_element_type=F32, precision=HIGHEST)


def _s5_build_weights(lb_ref, cl_ref, t_scr, wz_scr, wc_scr):
    L = S5_CHUNK
    cpow0 = _s5_expand(cl_ref[0, 0])
    t_scr[...] = jnp.zeros_like(t_scr)
    for k in range(L):
        zpow = _s5_expand(lb_ref[k, 0])
        wz_scr[(L - 1 - k) * 128:(L - k) * 128, :] = zpow.astype(BF16)
        wc_scr[k * 128:(k + 1) * 128, :] = _s5_expand(cl_ref[k + 1, 0]).astype(BF16)
        bd = _s5_lag_kernel(zpow, cpow0).astype(BF16)
        for sp in range(L - k):
            t_scr[sp * 128:(sp + 1) * 128, (sp + k) * 128:(sp + k + 1) * 128] = bd


def _s5_prompt_body(u_ref, lb_ref, cl_ref, sm_ref, d_ref, y_ref, fre_ref, fim_ref, t_ref, wz_ref, wc_ref,
                    *, n_steps):
    L = S5_CHUNK
    n_rows = SEQ // L
    ns = S5_LANE_GROUPS * S5_STATE

    @pl.when(pl.program_id(1) == 0)
    def _():
        _s5_build_weights(lb_ref, cl_ref, t_ref, wz_ref, wc_ref)

    v = jnp.concatenate([u_ref[pl.ds(s, n_rows, stride=L), :] for s in range(L)], axis=1)
    vb = v.astype(BF16)
    z = jnp.dot(vb, wz_ref[...], preferred_element_type=F32)
    hr, hi = z[:, :ns], z[:, ns:]
    pos = lax.broadcasted_iota(jnp.int32, (n_rows, ns), 0)
    sm = sm_ref[0]
    for d in range(n_steps):
        sh = 1 << d
        ar, ai = sm[d:d + 1, :ns], sm[d:d + 1, ns:]
        keep = pos >= sh
        pr = jnp.where(keep, pltpu.roll(hr, sh, 0), 0.0)
        pi = jnp.where(keep, pltpu.roll(hi, sh, 0), 0.0)
        hr, hi = hr + ar * pr - ai * pi, hi + ar * pi + ai * pr
    fre_ref[0] = hr[n_rows - 1:n_rows]
    fim_ref[0] = hi[n_rows - 1:n_rows]
    first = pos >= 1
    h_prev = jnp.concatenate([jnp.where(first, pltpu.roll(hr, 1, 0), 0.0),
                              jnp.where(first, pltpu.roll(hi, 1, 0), 0.0)], axis=1)
    y = (jnp.dot(vb, t_ref[...], preferred_element_type=F32)
         + lax.dot_general(h_prev.astype(BF16), wc_ref[...], _NT, preferred_element_type=F32)
         + d_ref[0] * v)
    y = _gelu_tanh(y)
    for s in range(L):
        y_ref[pl.ds(s, n_rows, stride=L), :] = y[:, s * 128:(s + 1) * 128]


def _s5_sample_body(u_ref, lb_ref, cl_ref, l1_ref, d_ref, hre_ref, him_ref, y_ref, sre_ref, sim_ref):
    ns = S5_LANE_GROUPS * S5_STATE
    bbar = _s5_expand(lb_ref[0, 0])
    k0 = _s5_lag_kernel(bbar, _s5_expand(cl_ref[0, 0])).astype(BF16)
    wc0 = _s5_expand(cl_ref[1, 0]).astype(BF16)
    us = u_ref[...]
    usb = us.astype(BF16)
    h0r, h0i = hre_ref[...], him_ref[...]
    l1 = l1_ref[0]
    bu = jnp.dot(usb, bbar.astype(BF16), preferred_element_type=F32)
    sre_ref[...] = l1[:, :ns] * h0r - l1[:, ns:] * h0i + bu[:, :ns]
    sim_ref[...] = l1[:, :ns] * h0i + l1[:, ns:] * h0r + bu[:, ns:]
    h0 = jnp.concatenate([h0r, h0i], axis=1).astype(BF16)
    ys = (jnp.dot(usb, k0, preferred_element_type=F32)
          + lax.dot_general(h0, wc0, _NT, preferred_element_type=F32)
          + d_ref[0] * us)
    y_ref[...] = _gelu_tanh(ys)


def _s5(u, weights, st_re, st_im):
    lb, cl, scan_mult, lbar1, dvec = weights
    L = S5_CHUNK
    ns = S5_LANE_GROUPS * S5_STATE
    n_steps = scan_mult.shape[1]
    wblk = lambda a: pl.BlockSpec((1,) + a.shape[1:], lambda j, b: (j, 0, 0))
    pblk = lambda a: pl.BlockSpec((a.shape[0], 1, 128, 128), lambda j, b: (0, j, 0, 0))
    mat = pltpu.VMEM((L * 128, L * 128), BF16)
    y, f_re, f_im = pl.pallas_call(
        functools.partial(_s5_prompt_body, n_steps=n_steps),
        grid=(S5_BLOCKS, N_PROMPT_SEQ),
        in_specs=[
            pl.BlockSpec((SEQ, 128), lambda j, b: (b, j)),
            pblk(lb), pblk(cl), wblk(scan_mult), wblk(dvec),
        ],
        out_specs=[
            pl.BlockSpec((SEQ, 128), lambda j, b: (b, j)),
            pl.BlockSpec((1, 1, ns), lambda j, b: (b, 0, j)),
            pl.BlockSpec((1, 1, ns), lambda j, b: (b, 0, j)),
        ],
        out_shape=[
            jax.ShapeDtypeStruct((N_PROMPT, BRANCH), F32),
            jax.ShapeDtypeStruct((N_PROMPT_SEQ, 1, S5_GROUPS * S5_STATE), F32),
            jax.ShapeDtypeStruct((N_PROMPT_SEQ, 1, S5_GROUPS * S5_STATE), F32),
        ],
        scratch_shapes=[mat, mat, mat],
        compiler_params=_cp(("arbitrary", "arbitrary")),
        name="s5_prompt",
    )(u, lb, cl, scan_mult, dvec)
    d0 = dvec[:, :, 0:128]
    sblk = lambda a: pl.BlockSpec((1,) + a.shape[1:], lambda j: (j, 0, 0))
    spblk = lambda a: pl.BlockSpec((2, 1, 128, 128), lambda j: (0, j, 0, 0))
    r0 = N_PROMPT // N_SAMPLE
    y_s, s_re, s_im = pl.pallas_call(
        _s5_sample_body,
        grid=(S5_BLOCKS,),
        in_specs=[
            pl.BlockSpec((N_SAMPLE, 128), lambda j: (r0, j)),
            spblk(lb), spblk(cl), sblk(lbar1), sblk(d0),
            pl.BlockSpec((N_SAMPLE, ns), lambda j: (0, j)),
            pl.BlockSpec((N_SAMPLE, ns), lambda j: (0, j)),
        ],
        out_specs=[
            pl.BlockSpec((N_SAMPLE, 128), lambda j: (0, j)),
            pl.BlockSpec((N_SAMPLE, ns), lambda j: (0, j)),
            pl.BlockSpec((N_SAMPLE, ns), lambda j: (0, j)),
        ],
        out_shape=[
            jax.ShapeDtypeStruct((N_SAMPLE, BRANCH), F32),
            jax.ShapeDtypeStruct((N_SAMPLE, S5_GROUPS * S5_STATE), F32),
            jax.ShapeDtypeStruct((N_SAMPLE, S5_GROUPS * S5_STATE), F32),
        ],
        compiler_params=_cp(("arbitrary",)),
        name="s5_sample",
    )(u, lb, cl, lbar1, d0, st_re, st_im)
    return jnp.concatenate([y, y_s], axis=0), f_re, f_im, s_re, s_im


def _merge_body(o_ref, z_ref, w0_ref, w1_ref, g0_ref, g1_ref, out_ref):
    p0 = jnp.dot(o_ref[...], w0_ref[0].astype(BF16), preferred_element_type=F32)
    p1 = jnp.dot(z_ref[...], w1_ref[0].astype(BF16), preferred_element_type=F32)
    out_ref[...] = (g0_ref[...] * p0 + g1_ref[...] * p1).astype(out_ref.dtype)


def _merge(o, z, w_branch, gates):
    ncb = D_MODEL // TN
    return pl.pallas_call(
        _merge_body,
        grid=(N_TOK // TM, ncb),
        in_specs=[
            pl.BlockSpec((TM, BRANCH), lambda i, j: (i, 0)),
            pl.BlockSpec((TM, BRANCH), lambda i, j: (i, 0)),
            pl.BlockSpec((1, BRANCH, TN), lambda i, j: (0, 0, j)),
            pl.BlockSpec((1, BRANCH, TN), lambda i, j: (1, 0, j)),
            pl.BlockSpec((TM, TN), lambda i, j: (i, j)),
            pl.BlockSpec((TM, TN), lambda i, j: (i, ncb + j)),
        ],
        out_specs=pl.BlockSpec((TM, TN), lambda i, j: (i, j)),
        out_shape=jax.ShapeDtypeStruct((N_TOK, D_MODEL), BF16),
        compiler_params=_cp(("arbitrary", "arbitrary")),
        name="merge",
    )(o, z, w_branch, w_branch, gates, gates)


LN_TM = 320


def _ln_router_body(r_ref, g_ref, b_ref, wr_ref, br_ref, o_ref, ids_ref, wts_ref):
    x1 = _layer_norm(r_ref[...]) * g_ref[...] + b_ref[...]
    o_ref[...] = x1
    x_hi = x1.astype(BF16)
    x_mid = (x1 - x_hi.astype(F32)).astype(BF16)
    wr = wr_ref[...]
    both = jnp.dot(x_hi, wr, preferred_element_type=F32)
    logits = (both[:, :128] + both[:, 128:]
              + jnp.dot(x_mid, wr[:, :128], preferred_element_type=F32) + br_ref[...])
    _route(logits, ids_ref, wts_ref)


def _ln_router(resid, g, b, w_r, b_r):
    w_hi = w_r.astype(BF16)
    w_mid = (w_r - w_hi.astype(F32)).astype(BF16)
    wr = jnp.concatenate([w_hi, w_mid], axis=1)
    row_blk = pl.BlockSpec((LN_TM, 128), lambda i: (i, 0))
    return pl.pallas_call(
        _ln_router_body,
        grid=(N_TOK // LN_TM,),
        in_specs=[
            pl.BlockSpec((LN_TM, D_MODEL), lambda i: (i, 0)),
            pl.BlockSpec((1, D_MODEL), lambda i: (0, 0)),
            pl.BlockSpec((1, D_MODEL), lambda i: (0, 0)),
            pl.BlockSpec((D_MODEL, 256), lambda i: (0, 0)),
            pl.BlockSpec((1, 128), lambda i: (0, 0)),
        ],
        out_specs=[pl.BlockSpec((LN_TM, D_MODEL), lambda i: (i, 0)), row_blk, row_blk],
        out_shape=[jax.ShapeDtypeStruct((N_TOK, D_MODEL), F32),
                   jax.ShapeDtypeStruct((N_TOK, 128), jnp.int32),
                   jax.ShapeDtypeStruct((N_TOK, 128), F32)],
        compiler_params=_cp(("arbitrary",)),
        name="ln1_router",
    )(resid, g, b, wr, b_r)


def _route(logits, ids_ref, wts_ref):
    lane = lax.broadcasted_iota(jnp.int32, logits.shape, 1)
    neg = -jnp.inf
    big = 1 << 20
    gl = jnp.where(lane < 8, logits, neg)
    gmax = jnp.max(gl, axis=1, keepdims=True)
    gidx = jnp.min(jnp.where(gl == gmax, lane, big), axis=1, keepdims=True)
    gprob = 1.0 / jnp.sum(jnp.exp(gl - gmax), axis=1, keepdims=True)
    in_group = (lane >= 8) & (lane < 8 + N_EXPERTS) & (((lane - 8) >> 3) == gidx)
    el = jnp.where(in_group, logits, neg)
    v1 = jnp.max(el, axis=1, keepdims=True)
    i1 = jnp.min(jnp.where(el == v1, lane, big), axis=1, keepdims=True)
    el2 = jnp.where(lane == i1, neg, el)
    v2 = jnp.max(el2, axis=1, keepdims=True)
    i2 = jnp.min(jnp.where(el2 == v2, lane, big), axis=1, keepdims=True)
    e2 = jnp.exp(v2 - v1)
    w1 = gprob / (1.0 + e2)
    w2 = gprob * e2 / (1.0 + e2)
    ids_ref[...] = jnp.where(lane == 0, i1 - 8, jnp.where(lane == 1, i2 - 8, 0))
    wts_ref[...] = jnp.where(lane == 0, w1, jnp.where(lane == 1, w2, 0.0))


N_ASSIGN = 2 * N_TOK
MOE_NBLOCKS = (N_ASSIGN + N_EXPERTS * (MOE_BLOCK - 1) + MOE_BLOCK - 1) // MOE_BLOCK
MOE_ROWS = MOE_NBLOCKS * MOE_BLOCK
MOE_HC = MOE_HIDDEN // MOE_HSPLIT


def _row_copy(src_hbm, row, dst, dst_row, sem):
    return pltpu.make_async_copy(src_hbm.at[pl.ds(row, 1), :], dst.at[pl.ds(dst_row, 1), :], sem)


MOE_MACRO = 4
MOE_NMACRO = N_ASSIGN // (MOE_MACRO * MOE_BLOCK) + N_EXPERTS + 1
MOE_DCOLS = 1024


def _block_copy(acc, t, o_hbm, blk, sem):
    return pltpu.make_async_copy(acc.at[pl.ds(pl.multiple_of(t * MOE_BLOCK, MOE_BLOCK), MOE_BLOCK), :],
                                 o_hbm.at[pl.ds(pl.multiple_of(blk * MOE_BLOCK, MOE_BLOCK), MOE_BLOCK), :], sem)


def _moe_body(mexp_ref, mstart_ref, mnsub_ref, tok_ref, x_hbm, wg_ref, wu_ref, wd_ref, o_hbm,
              stage, xb, acc, gsem, osem):
    del mexp_ref
    m = pl.program_id(0)
    h = pl.program_id(1)
    nsub = mnsub_ref[m]
    start = mstart_ref[m]
    last_m = pl.num_programs(0) - 1

    def wait_blocks(n):
        def done(t, c):
            _block_copy(acc, t, o_hbm, 0, osem).wait()
            return c

        lax.fori_loop(0, n, done, 0)

    @pl.when((h == 0) & (m > 0))
    def _():
        wait_blocks(mnsub_ref[jnp.maximum(m - 1, 0)])

    def gather_start(step):
        def sub(t, c):
            base = (mstart_ref[step] + t) * MOE_BLOCK
            off = pl.multiple_of(t * MOE_BLOCK, MOE_BLOCK)
            for r in range(MOE_BLOCK):
                _row_copy(x_hbm, tok_ref[base + r], stage, off + r, gsem).start()
            return c

        lax.fori_loop(0, mnsub_ref[step], sub, 0)

    @pl.when((h == 0) & (m == 0))
    def _():
        gather_start(0)

    @pl.when((h == 0) & (nsub > 0))
    def _():
        def landed(t, c):
            rows = pl.ds(pl.multiple_of(t * MOE_BLOCK, MOE_BLOCK), MOE_BLOCK)
            pltpu.make_async_copy(x_hbm.at[pl.ds(0, MOE_BLOCK), :], stage.at[rows, :], gsem).wait()
            return c

        lax.fori_loop(0, nsub, landed, 0)

        def sub(t, c):
            rows = pl.ds(pl.multiple_of(t * MOE_BLOCK, MOE_BLOCK), MOE_BLOCK)
            xb[rows, :] = stage[rows, :].astype(BF16)
            return c

        lax.fori_loop(0, nsub, sub, 0)

    @pl.when((h == pl.num_programs(1) - 1) & (m < last_m))
    def _():
        gather_start(jnp.minimum(m + 1, last_m))

    for ns in range(1, MOE_MACRO + 1):
        @pl.when(nsub == ns)
        def _(ns=ns):
            rows = ns * MOE_BLOCK

            @pl.when(h == 0)
            def _():
                acc[0:rows, :] = jnp.zeros((rows, D_MODEL), F32)

            x = xb[0:rows, :]
            hg = jnp.dot(x, wg_ref[0].astype(BF16), preferred_element_type=F32)
            hu = jnp.dot(x, wu_ref[0].astype(BF16), preferred_element_type=F32)
            hh = (hg * _sigmoid(hg) * hu).astype(BF16)
            wd = wd_ref[0].astype(BF16)
            for cc in range(D_MODEL // MOE_DCOLS):
                cols = slice(cc * MOE_DCOLS, (cc + 1) * MOE_DCOLS)
                acc[0:rows, cols] += jnp.dot(hh, wd[:, cols], preferred_element_type=F32)

    @pl.when((h == pl.num_programs(1) - 1) & (nsub > 0))
    def _():
        def put(t, c):
            _block_copy(acc, t, o_hbm, start + t, osem).start()
            return c

        lax.fori_loop(0, nsub, put, 0)

        @pl.when(m == last_m)
        def _():
            wait_blocks(nsub)

    @pl.when((h == 0) & (nsub == 0))
    def _():
        acc[0:MOE_BLOCK, :] = jnp.zeros((MOE_BLOCK, D_MODEL), F32)
        for t in range(MOE_MACRO):
            @pl.when(start + t < MOE_NBLOCKS)
            def _(t=t):
                cp = _block_copy(acc, 0, o_hbm, start + t, osem)
                cp.start()
                cp.wait()


def _moe_experts(x1, wg, wu, wd, mexp, mstart, mnsub, row_tok):
    grid_spec = pltpu.PrefetchScalarGridSpec(
        num_scalar_prefetch=4,
        grid=(MOE_NMACRO, MOE_HSPLIT),
        in_specs=[
            pl.BlockSpec(memory_space=pl.ANY),
            pl.BlockSpec((1, D_MODEL, MOE_HC), lambda m, h, me, ms, mn, rt: (me[m], 0, h)),
            pl.BlockSpec((1, D_MODEL, MOE_HC), lambda m, h, me, ms, mn, rt: (me[m], 0, h)),
            pl.BlockSpec((1, MOE_HC, D_MODEL), lambda m, h, me, ms, mn, rt: (me[m], h, 0)),
        ],
        out_specs=pl.BlockSpec(memory_space=pl.ANY),
        scratch_shapes=[pltpu.VMEM((MOE_MACRO * MOE_BLOCK, D_MODEL), F32),
                        pltpu.VMEM((MOE_MACRO * MOE_BLOCK, D_MODEL), BF16),
                        pltpu.VMEM((MOE_MACRO * MOE_BLOCK, D_MODEL), F32),
                        pltpu.SemaphoreType.DMA(()), pltpu.SemaphoreType.DMA(())],
    )
    return pl.pallas_call(
        _moe_body,
        grid_spec=grid_spec,
        out_shape=jax.ShapeDtypeStruct((MOE_ROWS, D_MODEL), F32),
        compiler_params=_cp(("arbitrary", "arbitrary")),
        name="moe_experts",
    )(mexp, mstart, mnsub, row_tok, x1, wg, wu, wd)


CMB_TM = 128
CMB_PROMPT_TILES = N_PROMPT // CMB_TM


def _combine_body(pos_ref, eo_hbm, wts_ref, x1_ref, g_ref, b_ref, yp_ref, ys_ref, buf, sem):
    i = pl.program_id(0)
    slot = i % 2

    def fetch(tile, s):
        for r in range(CMB_TM):
            a = 2 * (tile * CMB_TM + r)
            _row_copy(eo_hbm, pos_ref[a], buf.at[s, 0], r, sem.at[s]).start(priority=0)
            _row_copy(eo_hbm, pos_ref[a + 1], buf.at[s, 1], r, sem.at[s]).start(priority=1)

    @pl.when(i == 0)
    def _():
        fetch(0, 0)

    @pl.when(i + 1 < pl.num_programs(0))
    def _():
        fetch(i + 1, 1 - slot)

    for k in range(2):
        pltpu.make_async_copy(eo_hbm.at[pl.ds(0, CMB_TM), :], buf.at[slot, k], sem.at[slot]).wait()
    w = wts_ref[...]
    y = w[:, 0:1] * buf[slot, 0] + w[:, 1:2] * buf[slot, 1]
    x2 = _layer_norm(ALPHA * x1_ref[...] + y) * g_ref[...] + b_ref[...]

    @pl.when(i < CMB_PROMPT_TILES)
    def _():
        yp_ref[...] = x2

    @pl.when(i >= CMB_PROMPT_TILES)
    def _():
        ys_ref[...] = x2


def _combine(pos, eo, wts, x1, g, b):
    grid_spec = pltpu.PrefetchScalarGridSpec(
        num_scalar_prefetch=1,
        grid=(N_TOK // CMB_TM,),
        in_specs=[
            pl.BlockSpec(memory_space=pl.ANY),
            pl.BlockSpec((CMB_TM, 128), lambda i, p: (i, 0)),
            pl.BlockSpec((CMB_TM, D_MODEL), lambda i, p: (i, 0)),
            pl.BlockSpec((1, D_MODEL), lambda i, p: (0, 0)),
            pl.BlockSpec((1, D_MODEL), lambda i, p: (0, 0)),
        ],
        out_specs=[
            pl.BlockSpec((CMB_TM, D_MODEL), lambda i, p: (jnp.minimum(i, CMB_PROMPT_TILES - 1), 0)),
            pl.BlockSpec((CMB_TM, D_MODEL), lambda i, p: (0, 0)),
        ],
        scratch_shapes=[pltpu.VMEM((2, 2, CMB_TM, D_MODEL), F32), pltpu.SemaphoreType.DMA((2,))],
    )
    return pl.pallas_call(
        _combine_body,
        grid_spec=grid_spec,
        out_shape=[jax.ShapeDtypeStruct((N_PROMPT, D_MODEL), F32),
                   jax.ShapeDtypeStruct((N_SAMPLE, D_MODEL), F32)],
        compiler_params=_cp(("arbitrary",)),
        name="combine_ln2",
    )(pos, eo, wts, x1, g, b)


def _route_positions(ids):
    eid = ids[:, :2].reshape(-1)
    onehot = (eid[:, None] == jnp.arange(N_EXPERTS, dtype=jnp.int32)[None, :]).astype(jnp.int32)
    csum = jnp.cumsum(onehot, axis=0)
    rank = jnp.take_along_axis(csum, eid[:, None], axis=1)[:, 0] - 1
    counts = csum[-1]
    nblk_e = (counts + MOE_BLOCK - 1) // MOE_BLOCK
    bend = jnp.cumsum(nblk_e)
    bstart = bend - nblk_e
    pos = bstart[eid] * MOE_BLOCK + rank
    row_tok = jnp.zeros((MOE_ROWS,), jnp.int32).at[pos].set(jnp.arange(N_ASSIGN, dtype=jnp.int32) // 2)
    nstep_e = (nblk_e + MOE_MACRO - 1) // MOE_MACRO
    send = jnp.cumsum(nstep_e)
    n_steps, n_blocks = send[-1], bend[-1]
    m = jnp.arange(MOE_NMACRO, dtype=jnp.int32)
    e_of_m = jnp.minimum(jnp.searchsorted(send, m, side='right'), N_EXPERTS - 1).astype(jnp.int32)
    local = m - (send - nstep_e)[e_of_m]
    valid = m < n_steps
    mexp = jnp.where(valid, e_of_m, e_of_m[jnp.maximum(n_steps - 1, 0)])
    mstart = jnp.where(valid, bstart[e_of_m] + MOE_MACRO * local, n_blocks + MOE_MACRO * (m - n_steps))
    mnsub = jnp.where(valid, jnp.clip(nblk_e[e_of_m] - MOE_MACRO * local, 0, MOE_MACRO), 0)
    return (pos.astype(jnp.int32), row_tok, mexp.astype(jnp.int32), mstart.astype(jnp.int32),
            mnsub.astype(jnp.int32))


def kernel(x_prompt, x_sample, state_gla, state_s5_re, state_s5_im, w_in, w_gla_gate_up, b_gla_gate_up, w_gla_norm, s5_a_re, s5_a_im, s5_b_re, s5_b_im, s5_c_re, s5_c_im, s5_d, s5_log_dt, w_s5_glu, b_s5_glu, w_branch, w_out, ln1_g, ln1_b, w_router_group, b_router_group, w_router_expert, b_router_expert, w_moe_gate, w_moe_up, w_moe_down, ln2_g, ln2_b):
    x_all = jnp.concatenate([x_prompt.reshape(N_PROMPT, D_MODEL), x_sample.reshape(N_SAMPLE, D_MODEL)], axis=0)
    x_bf = x_all.astype(BF16)

    w_in_t = w_in.T
    qkvr = _mm_t(x_bf, w_in_t, row0=0, ncols=COL_A, tn=TN, out_dtype=F32, name="proj_qkvr")
    a_low = _mm_t(x_bf, w_in_t, row0=COL_A, ncols=128, tn=128, out_dtype=F32, name="proj_a")
    u = _mm_t(x_bf, w_in_t, row0=COL_U, ncols=BRANCH, tn=TN, out_dtype=F32, name="proj_u")
    gates = _mm_t(x_bf, w_in_t, row0=COL_G, ncols=2 * D_MODEL, tn=TN, out_dtype=F32, name="proj_gates",
                  epilogue=_sigmoid)

    wgu = jnp.pad(w_gla_gate_up, ((0, 128 - GLA_RANK), (0, 0)))
    bgu = b_gla_gate_up.reshape(1, GLA_DK)
    wn = w_gla_norm.reshape(1, BRANCH)
    o_p, gla_p_t = _gla_prompt(qkvr, a_low, wgu, bgu, wn)
    o_s, gla_s = _gla_sample(qkvr, a_low, wgu, bgu, wn, state_gla)
    o_all = jnp.concatenate([o_p, o_s], axis=0)
    gla_p = jnp.swapaxes(gla_p_t, 2, 3)

    s5w = _s5_weights(s5_a_re, s5_a_im, s5_b_re, s5_b_im, s5_c_re, s5_c_im, s5_d, s5_log_dt)
    y, s5_re_p, s5_im_p, s5_re_s, s5_im_s = _s5(
        u, s5w, state_s5_re.reshape(N_SAMPLE, -1), state_s5_im.reshape(N_SAMPLE, -1))
    z = _mm(y, w_s5_glu, col0=0, ncols=BRANCH, tn=TN, out_dtype=BF16, name="s5_glu",
            epilogue=lambda acc, yt, bt: yt * _sigmoid(acc + bt),
            extra=(y, b_s5_glu.reshape(1, BRANCH)),
            extra_specs=(pl.BlockSpec((TM, TN), lambda i, j: (i, j)), pl.BlockSpec((1, TN), lambda i, j: (0, j))))

    pre = _merge(o_all, z, w_branch, gates)
    w_r = jnp.concatenate([w_router_group,
                           jnp.moveaxis(w_router_expert, 0, 1).reshape(D_MODEL, N_EXPERTS),
                           jnp.zeros((D_MODEL, 128 - 8 - N_EXPERTS), F32)], axis=1)
    b_r = jnp.concatenate([b_router_group, b_router_expert.reshape(-1),
                           jnp.zeros((128 - 8 - N_EXPERTS,), F32)]).reshape(1, 128)
    resid = _mm(pre, w_out, col0=0, ncols=D_MODEL, tn=TN, out_dtype=F32, name="out_proj",
                epilogue=lambda acc, xt: ALPHA * xt + acc, extra=(x_all,),
                extra_specs=(pl.BlockSpec((TM, TN), lambda i, j: (i, j)),))
    x1, ids, wts = _ln_router(resid, ln1_g.reshape(1, D_MODEL), ln1_b.reshape(1, D_MODEL), w_r, b_r)

    pos, row_tok, mexp, mstart, mnsub = _route_positions(ids)
    eo = _moe_experts(x1,
                      w_moe_gate.reshape(N_EXPERTS, D_MODEL, MOE_HIDDEN),
                      w_moe_up.reshape(N_EXPERTS, D_MODEL, MOE_HIDDEN),
                      w_moe_down.reshape(N_EXPERTS, MOE_HIDDEN, D_MODEL),
                      mexp, mstart, mnsub, row_tok)
    y_p, y_s = _combine(pos, eo, wts, x1, ln2_g.reshape(1, D_MODEL), ln2_b.reshape(1, D_MODEL))

    return (y_p.reshape(N_PROMPT_SEQ, SEQ, D_MODEL), y_s.reshape(N_SAMPLE, 1, D_MODEL),
            gla_p,
            s5_re_p.reshape(N_PROMPT_SEQ, S5_GROUPS, S5_STATE), s5_im_p.reshape(N_PROMPT_SEQ, S5_GROUPS, S5_STATE),
            gla_s,
            s5_re_s.reshape(N_SAMPLE, S5_GROUPS, S5_STATE), s5_im_s.reshape(N_SAMPLE, S5_GROUPS, S5_STATE))
```

```python
import functools
import math

import jax
import jax.numpy as jnp
import numpy as np
from jax import lax
from jax.experimental import pallas as pl
from jax.experimental.pallas import tpu as pltpu

F32 = jnp.float32
BF16 = jnp.bfloat16
HIGHEST = lax.Precision.HIGHEST

D_MODEL = 4096
N_PROMPT_SEQ = 4
SEQ = 2048
N_SAMPLE = 128
N_PROMPT = N_PROMPT_SEQ * SEQ
N_TOK = N_PROMPT + N_SAMPLE
BRANCH = D_MODEL // 2
GLA_HEADS = 4
GLA_DK = D_MODEL // 4
GLA_HDK = GLA_DK // GLA_HEADS
GLA_HDV = BRANCH // GLA_HEADS
GLA_RANK = 16
GLA_TAU = 16.0
GLA_CHUNK = 64
GLA_LEVELS = 6
S5_GROUP = 16
S5_GROUPS = BRANCH // S5_GROUP
S5_STATE = 64
S5_CHUNK = 8
S5_LANE_GROUPS = 8
S5_BLOCKS = S5_GROUPS // S5_LANE_GROUPS
N_EXPERTS = 64
MOE_HIDDEN = D_MODEL // 8
MOE_BLOCK = 128
MOE_HSPLIT = 2
LN_EPS = 1e-5
ALPHA = 2.0 ** 0.25
COL_A = 6144
COL_U = 6160
COL_G = 8208
IN_COLS = 16400

VMEM_LIMIT = 56 * 1024 * 1024
TM = 1040
TN = 512


_NT = (((1,), (1,)), ((), ()))
_TN = (((0,), (0,)), ((), ()))


def _cp(sem):
    return pltpu.CompilerParams(dimension_semantics=sem, vmem_limit_bytes=VMEM_LIMIT)


def _sigmoid(x):
    return 1.0 / (1.0 + jnp.exp(-x))


def _log_sigmoid(z):
    return jnp.minimum(z, 0.0) - jnp.log1p(jnp.exp(-jnp.abs(z)))


def _gelu_tanh(x):
    return 0.5 * x * (1.0 + jnp.tanh(math.sqrt(2.0 / math.pi) * (x + 0.044715 * (x * x * x))))


def _layer_norm(x):
    mu = jnp.mean(x, axis=-1, keepdims=True)
    xc = x - mu
    var = jnp.mean(xc * xc, axis=-1, keepdims=True)
    return xc * lax.rsqrt(var + LN_EPS)


def _mm_body(a_ref, w_ref, *rest, epilogue):
    *extra, o_ref = rest
    acc = jnp.dot(a_ref[...].astype(BF16), w_ref[...].astype(BF16), preferred_element_type=F32)
    o_ref[...] = epilogue(acc, *[e[...] for e in extra]).astype(o_ref.dtype)


def _mm(a, w, *, col0, ncols, tn, out_dtype, name, epilogue=lambda acc: acc, extra=(), extra_specs=()):
    m, k = a.shape
    cb0 = col0 // tn
    return pl.pallas_call(
        functools.partial(_mm_body, epilogue=epilogue),
        grid=(m // TM, ncols // tn),
        in_specs=[pl.BlockSpec((TM, k), lambda i, j: (i, 0)),
                  pl.BlockSpec((k, tn), lambda i, j: (0, cb0 + j)),
                  *extra_specs],
        out_specs=pl.BlockSpec((TM, tn), lambda i, j: (i, j)),
        out_shape=jax.ShapeDtypeStruct((m, ncols), out_dtype),
        compiler_params=_cp(("arbitrary", "arbitrary")),
        name=name,
    )(a, w, *extra)


def _mm_t_body(a_ref, wt_ref, o_ref, *, epilogue):
    acc = lax.dot_general(a_ref[...], wt_ref[...].astype(BF16), _NT, preferred_element_type=F32)
    o_ref[...] = epilogue(acc).astype(o_ref.dtype)


def _mm_t(a, wt, *, row0, ncols, tn, out_dtype, name, epilogue=lambda acc: acc):
    m, k = a.shape
    return pl.pallas_call(
        functools.partial(_mm_t_body, epilogue=epilogue),
        grid=(m // TM, ncols // tn),
        in_specs=[pl.BlockSpec((TM, k), lambda i, j: (i, 0)),
                  pl.BlockSpec((pl.Element(tn), pl.Element(k)),
                               lambda i, j: (pl.multiple_of(row0 + j * tn, 8), 0))],
        out_specs=pl.BlockSpec((TM, tn), lambda i, j: (i, j)),
        out_shape=jax.ShapeDtypeStruct((m, ncols), out_dtype),
        compiler_params=_cp(("arbitrary", "arbitrary")),
        name=name,
    )(a, wt)


def _gla_coeff_matrix():
    c = GLA_CHUNK
    t = np.arange(c)[:, None]
    u = np.arange(c)[None, :]
    blocks = [(u <= t), (u > t)]
    for lvl in range(1, GLA_LEVELS + 1):
        m = 1 << lvl
        half = m // 2
        mid = (t // m) * m + half - 1
        lower = (t % m) >= half
        blocks.append(np.where(lower, (u > mid) & (u <= t), (u > t) & (u <= mid)))
    return np.concatenate(blocks, axis=0).astype(np.float32)


def _gla_level_masks():
    c = GLA_CHUNK
    t = lax.broadcasted_iota(jnp.int32, (c, c), 0)
    s = lax.broadcasted_iota(jnp.int32, (c, c), 1)
    masks = []
    for lvl in range(1, GLA_LEVELS + 1):
        m = 1 << lvl
        half = m // 2
        masks.append(((t >> lvl) == (s >> lvl)) & ((t & (m - 1)) >= half) & ((s & (m - 1)) < half))
    return masks


def _gla_out_norm(o, r, wn):
    return _layer_norm(o) * wn * (r * _sigmoid(r))


def _gla_prompt_body(q_ref, k_ref, v_ref, r_ref, a_ref, wgu_ref, bgu_ref, wn_ref, cm_ref,
                     o_ref, st_ref, s_scr, *, n_sub):
    c = GLA_CHUNK

    @pl.when(pl.program_id(2) == 0)
    def _():
        s_scr[...] = jnp.zeros_like(s_scr)

    masks = _gla_level_masks()
    wgu = wgu_ref[...]
    bgu = bgu_ref[...]
    wn = wn_ref[...]
    cm = cm_ref[...]
    nt, tn = _NT, _TN

    def chunk(ci, carry):
        rows = pl.ds(pl.multiple_of(ci * c, c), c)
        q = q_ref[rows, :] * (GLA_HDK ** -0.5)
        k = k_ref[rows, :]
        v = v_ref[rows, :]
        z = jnp.dot(a_ref[rows, :], wgu, preferred_element_type=F32, precision=HIGHEST) + bgu
        g = _log_sigmoid(z) * (1.0 / GLA_TAU)
        g_hi = g.astype(BF16)
        r1 = g - g_hi.astype(F32)
        g_mid = r1.astype(BF16)
        g_lo = (r1 - g_mid.astype(F32)).astype(BF16)
        g3 = jnp.concatenate([g_hi, g_mid, g_lo], axis=0)
        f = jnp.exp(jnp.dot(cm, g3, preferred_element_type=F32))
        st = s_scr[...]
        vb = v.astype(BF16)
        o = lax.dot_general((q * f[0:c]).astype(BF16), st.astype(BF16), nt, preferred_element_type=F32)
        scores = jnp.zeros((c, c), F32)
        for lvl in range(GLA_LEVELS):
            fl = f[(2 + lvl) * c:(3 + lvl) * c]
            p = lax.dot_general((q * fl).astype(BF16), (k * fl).astype(BF16), nt, preferred_element_type=F32)
            scores = scores + jnp.where(masks[lvl], p, 0.0)
        diag = jnp.sum(q * k, axis=1, keepdims=True)
        o = o + jnp.dot(scores.astype(BF16), vb, preferred_element_type=F32) + diag * v
        kd = (k * f[c:2 * c]).astype(BF16)
        s_scr[...] = st * f[c - 1:c, :] + lax.dot_general(vb, kd, tn, preferred_element_type=F32)
        o_ref[rows, :] = _gla_out_norm(o, r_ref[rows, :], wn).astype(o_ref.dtype)
        return carry

    lax.fori_loop(0, n_sub, chunk, 0, unroll=True)

    @pl.when(pl.program_id(2) == pl.num_programs(2) - 1)
    def _():
        st_ref[0, 0] = s_scr[...]


def _gla_prompt(qkvr, a_low, wgu, bgu, wn):
    n_sub = 4
    tt = GLA_CHUNK * n_sub
    nt_steps = SEQ // tt
    cm = jnp.asarray(np.tile(_gla_coeff_matrix(), (1, 3)), dtype=BF16)
    rows = lambda b, h, c: b * nt_steps + c
    o, st = pl.pallas_call(
        functools.partial(_gla_prompt_body, n_sub=n_sub),
        grid=(N_PROMPT_SEQ, GLA_HEADS, nt_steps),
        in_specs=[
            pl.BlockSpec((tt, GLA_HDK), lambda b, h, c: (rows(b, h, c), h)),
            pl.BlockSpec((tt, GLA_HDK), lambda b, h, c: (rows(b, h, c), GLA_HEADS + h)),
            pl.BlockSpec((tt, GLA_HDV), lambda b, h, c: (rows(b, h, c), GLA_HEADS + h)),
            pl.BlockSpec((tt, GLA_HDV), lambda b, h, c: (rows(b, h, c), 2 * GLA_HEADS + h)),
            pl.BlockSpec((tt, 128), lambda b, h, c: (rows(b, h, c), 0)),
            pl.BlockSpec((128, GLA_HDK), lambda b, h, c: (0, h)),
            pl.BlockSpec((1, GLA_HDK), lambda b, h, c: (0, h)),
            pl.BlockSpec((1, GLA_HDV), lambda b, h, c: (0, h)),
            pl.BlockSpec(cm.shape, lambda b, h, c: (0, 0)),
        ],
        out_specs=[
            pl.BlockSpec((tt, GLA_HDV), lambda b, h, c: (rows(b, h, c), h)),
            pl.BlockSpec((1, 1, GLA_HDV, GLA_HDK), lambda b, h, c: (b, h, 0, 0)),
        ],
        out_shape=[
            jax.ShapeDtypeStruct((N_PROMPT, BRANCH), BF16),
            jax.ShapeDtypeStruct((N_PROMPT_SEQ, GLA_HEADS, GLA_HDV, GLA_HDK), F32),
        ],
        scratch_shapes=[pltpu.VMEM((GLA_HDV, GLA_HDK), F32)],
        compiler_params=_cp(("arbitrary", "arbitrary", "arbitrary")),
        name="gla_prompt",
    )(qkvr, qkvr, qkvr, qkvr, a_low, wgu, bgu, wn, cm)
    return o, st


GLA_SB = 16


def _gla_sample_body(q_ref, k_ref, v_ref, r_ref, a_ref, wgu_ref, bgu_ref, wn_ref, s_ref, o_ref, so_ref):
    q = q_ref[...] * (GLA_HDK ** -0.5)
    k = k_ref[...]
    v = v_ref[...]
    z = jnp.dot(a_ref[...], wgu_ref[...], preferred_element_type=F32, precision=HIGHEST) + bgu_ref[...]
    eg = jnp.exp(_log_sigmoid(z) * (1.0 / GLA_TAU))
    qe = (q * eg).astype(BF16)
    eg_t = eg.T
    k_t = k.T
    rows = []
    for n in range(GLA_SB):
        s0 = s_ref[n, 0]
        rows.append(jnp.dot(qe, s0.astype(BF16), preferred_element_type=F32)[n:n + 1])
        so_ref[n, 0] = s0 * eg_t[:, n:n + 1] + k_t[:, n:n + 1] * v[n:n + 1, :]
    o = jnp.concatenate(rows, axis=0) + jnp.sum(q * k, axis=1, keepdims=True) * v
    o_ref[...] = _gla_out_norm(o, r_ref[...], wn_ref[...]).astype(o_ref.dtype)


def _gla_sample(qkvr, a_low, wgu, bgu, wn, state):
    r0 = N_PROMPT // GLA_SB
    o, st = pl.pallas_call(
        _gla_sample_body,
        grid=(GLA_HEADS, N_SAMPLE // GLA_SB),
        in_specs=[
            pl.BlockSpec((GLA_SB, GLA_HDK), lambda h, i: (r0 + i, h)),
            pl.BlockSpec((GLA_SB, GLA_HDK), lambda h, i: (r0 + i, GLA_HEADS + h)),
            pl.BlockSpec((GLA_SB, GLA_HDV), lambda h, i: (r0 + i, GLA_HEADS + h)),
            pl.BlockSpec((GLA_SB, GLA_HDV), lambda h, i: (r0 + i, 2 * GLA_HEADS + h)),
            pl.BlockSpec((GLA_SB, 128), lambda h, i: (r0 + i, 0)),
            pl.BlockSpec((128, GLA_HDK), lambda h, i: (0, h)),
            pl.BlockSpec((1, GLA_HDK), lambda h, i: (0, h)),
            pl.BlockSpec((1, GLA_HDV), lambda h, i: (0, h)),
            pl.BlockSpec((GLA_SB, 1, GLA_HDK, GLA_HDV), lambda h, i: (i, h, 0, 0)),
        ],
        out_specs=[
            pl.BlockSpec((GLA_SB, GLA_HDV), lambda h, i: (i, h)),
            pl.BlockSpec((GLA_SB, 1, GLA_HDK, GLA_HDV), lambda h, i: (i, h, 0, 0)),
        ],
        out_shape=[
            jax.ShapeDtypeStruct((N_SAMPLE, BRANCH), BF16),
            jax.ShapeDtypeStruct(state.shape, F32),
        ],
        compiler_params=_cp(("arbitrary", "arbitrary")),
        name="gla_sample",
    )(qkvr, qkvr, qkvr, qkvr, a_low, wgu, bgu, wn, state)
    return o, st


def _s5_weights(a_re, a_im, b_re, b_im, c_re, c_im, d, log_dt):
    L = S5_CHUNK
    lam_re = jnp.minimum(a_re, -1e-4)
    lam_im = a_im
    dt = jnp.exp(log_dt)[:, None]
    kk = jnp.arange(L + 1, dtype=F32)[:, None, None]
    pow_re = jnp.exp(lam_re * dt * kk) * jnp.cos(lam_im * dt * kk)
    pow_im = jnp.exp(lam_re * dt * kk) * jnp.sin(lam_im * dt * kk)
    lbar_re, lbar_im = pow_re[1], pow_im[1]
    den = lam_re * lam_re + lam_im * lam_im
    f_re = ((lbar_re - 1.0) * lam_re + lbar_im * lam_im) / den
    f_im = (lbar_im * lam_re - (lbar_re - 1.0) * lam_im) / den
    bb_re = f_re[..., None] * b_re - f_im[..., None] * b_im
    bb_im = f_re[..., None] * b_im + f_im[..., None] * b_re
    nb, ng = S5_BLOCKS, S5_LANE_GROUPS
    bb_re_t = jnp.swapaxes(bb_re, 1, 2)
    bb_im_t = jnp.swapaxes(bb_im, 1, 2)
    lb_re = (pow_re[:L, :, None, :] * bb_re_t - pow_im[:L, :, None, :] * bb_im_t).reshape(L, nb, 128, S5_STATE)
    lb_im = (pow_re[:L, :, None, :] * bb_im_t + pow_im[:L, :, None, :] * bb_re_t).reshape(L, nb, 128, S5_STATE)
    cl_re = (c_re[None] * pow_re[:, :, None, :] - c_im[None] * pow_im[:, :, None, :]).reshape(L + 1, nb, 128, S5_STATE)
    cl_im = (c_re[None] * pow_im[:, :, None, :] + c_im[None] * pow_re[:, :, None, :]).reshape(L + 1, nb, 128, S5_STATE)
    lb = jnp.concatenate([lb_re, lb_im], axis=-1)
    cl = jnp.concatenate([cl_re, -cl_im], axis=-1)
    n_steps = int(math.log2(SEQ // L))
    mult = (L * (2.0 ** jnp.arange(n_steps, dtype=F32)))[:, None, None]
    sc_re = jnp.exp(lam_re * dt * mult) * jnp.cos(lam_im * dt * mult)
    sc_im = jnp.exp(lam_re * dt * mult) * jnp.sin(lam_im * dt * mult)

    def state_lanes(x):
        lead = x.shape[:-2]
        return jnp.moveaxis(x.reshape(lead + (nb, ng * S5_STATE)), -2, 0)

    scan_mult = jnp.concatenate([state_lanes(sc_re), state_lanes(sc_im)], axis=-1)
    lbar1 = jnp.concatenate([state_lanes(lbar_re[None]), state_lanes(lbar_im[None])], axis=-1)
    dvec = jnp.tile(d.reshape(nb, 1, 128), (1, 1, L))
    return lb, cl, scan_mult, lbar1, dvec


def _s5_expand(src):
    ng, p = S5_LANE_GROUPS, S5_STATE
    lane = lax.broadcasted_iota(jnp.int32, src.shape, 1)
    other = pltpu.roll(src, p, 1)
    re2 = jnp.where(lane < p, src, other)
    im2 = jnp.where(lane < p, other, src)
    full = jnp.concatenate([re2] * (ng // 2) + [im2] * (ng // 2), axis=1)
    row = lax.broadcasted_iota(jnp.int32, full.shape, 0)
    col = lax.broadcasted_iota(jnp.int32, full.shape, 1)
    same_group = (row >> 4) == ((col >> 6) & (ng - 1))
    return jnp.where(same_group, full, 0.0)


def _s5_lag_kernel(zpow, cpow0):
    return lax.dot_general(zpow, cpow0, _NT, preferred_element_type=F32, precision=HIGHEST)


def _s5_build_weights(lb_ref, cl_ref, t_scr, wz_scr, wc_scr):
    L = S5_CHUNK
    cpow0 = _s5_expand(cl_ref[0, 0])
    t_scr[...] = jnp.zeros_like(t_scr)
    for k in range(L):
        zpow = _s5_expand(lb_ref[k, 0])
        wz_scr[(L - 1 - k) * 128:(L - k) * 128, :] = zpow.astype(BF16)
        wc_scr[k * 128:(k + 1) * 128, :] = _s5_expand(cl_ref[k + 1, 0]).astype(BF16)
        bd = _s5_lag_kernel(zpow, cpow0).astype(BF16)
        for sp in range(L - k):
            t_scr[sp * 128:(sp + 1) * 128, (sp + k) * 128:(sp + k + 1) * 128] = bd


def _s5_prompt_body(u_ref, lb_ref, cl_ref, sm_ref, d_ref, y_ref, fre_ref, fim_ref, t_ref, wz_ref, wc_ref,
                    *, n_steps):
    L = S5_CHUNK
    n_rows = SEQ // L
    ns = S5_LANE_GROUPS * S5_STATE

    @pl.when(pl.program_id(1) == 0)
    def _():
        _s5_build_weights(lb_ref, cl_ref, t_ref, wz_ref, wc_ref)

    v = jnp.concatenate([u_ref[pl.ds(s, n_rows, stride=L), :] for s in range(L)], axis=1)
    vb = v.astype(BF16)
    z = jnp.dot(vb, wz_ref[...], preferred_element_type=F32)
    hr, hi = z[:, :ns], z[:, ns:]
    pos = lax.broadcasted_iota(jnp.int32, (n_rows, ns), 0)
    sm = sm_ref[0]
    for d in range(n_steps):
        sh = 1 << d
        ar, ai = sm[d:d + 1, :ns], sm[d:d + 1, ns:]
        keep = pos >= sh
        pr = jnp.where(keep, pltpu.roll(hr, sh, 0), 0.0)
        pi = jnp.where(keep, pltpu.roll(hi, sh, 0), 0.0)
        hr, hi = hr + ar * pr - ai * pi, hi + ar * pi + ai * pr
    fre_ref[0] = hr[n_rows - 1:n_rows]
    fim_ref[0] = hi[n_rows - 1:n_rows]
    first = pos >= 1
    h_prev = jnp.concatenate([jnp.where(first, pltpu.roll(hr, 1, 0), 0.0),
                              jnp.where(first, pltpu.roll(hi, 1, 0), 0.0)], axis=1)
    y = (jnp.dot(vb, t_ref[...], preferred_element_type=F32)
         + lax.dot_general(h_prev.astype(BF16), wc_ref[...], _NT, preferred_element_type=F32)
         + d_ref[0] * v)
    y = _gelu_tanh(y)
    for s in range(L):
        y_ref[pl.ds(s, n_rows, stride=L), :] = y[:, s * 128:(s + 1) * 128]


def _s5_sample_body(u_ref, lb_ref, cl_ref, l1_ref, d_ref, hre_ref, him_ref, y_ref, sre_ref, sim_ref):
    ns = S5_LANE_GROUPS * S5_STATE
    bbar = _s5_expand(lb_ref[0, 0])
    k0 = _s5_lag_kernel(bbar, _s5_expand(cl_ref[0, 0])).astype(BF16)
    wc0 = _s5_expand(cl_ref[1, 0]).astype(BF16)
    us = u_ref[...]
    usb = us.astype(BF16)
    h0r, h0i = hre_ref[...], him_ref[...]
    l1 = l1_ref[0]
    bu = jnp.dot(usb, bbar.astype(BF16), preferred_element_type=F32)
    sre_ref[...] = l1[:, :ns] * h0r - l1[:, ns:] * h0i + bu[:, :ns]
    sim_ref[...] = l1[:, :ns] * h0i + l1[:, ns:] * h0r + bu[:, ns:]
    h0 = jnp.concatenate([h0r, h0i], axis=1).astype(BF16)
    ys = (jnp.dot(usb, k0, preferred_element_type=F32)
          + lax.dot_general(h0, wc0, _NT, preferred_element_type=F32)
          + d_ref[0] * us)
    y_ref[...] = _gelu_tanh(ys)


def _s5(u, weights, st_re, st_im):
    lb, cl, scan_mult, lbar1, dvec = weights
    L = S5_CHUNK
    ns = S5_LANE_GROUPS * S5_STATE
    n_steps = scan_mult.shape[1]
    wblk = lambda a: pl.BlockSpec((1,) + a.shape[1:], lambda j, b: (j, 0, 0))
    pblk = lambda a: pl.BlockSpec((a.shape[0], 1, 128, 128), lambda j, b: (0, j, 0, 0))
    mat = pltpu.VMEM((L * 128, L * 128), BF16)
    y, f_re, f_im = pl.pallas_call(
        functools.partial(_s5_prompt_body, n_steps=n_steps),
        grid=(S5_BLOCKS, N_PROMPT_SEQ),
        in_specs=[
            pl.BlockSpec((SEQ, 128), lambda j, b: (b, j)),
            pblk(lb), pblk(cl), wblk(scan_mult), wblk(dvec),
        ],
        out_specs=[
            pl.BlockSpec((SEQ, 128), lambda j, b: (b, j)),
            pl.BlockSpec((1, 1, ns), lambda j, b: (b, 0, j)),
            pl.BlockSpec((1, 1, ns), lambda j, b: (b, 0, j)),
        ],
        out_shape=[
            jax.ShapeDtypeStruct((N_PROMPT, BRANCH), F32),
            jax.ShapeDtypeStruct((N_PROMPT_SEQ, 1, S5_GROUPS * S5_STATE), F32),
            jax.ShapeDtypeStruct((N_PROMPT_SEQ, 1, S5_GROUPS * S5_STATE), F32),
        ],
        scratch_shapes=[mat, mat, mat],
        compiler_params=_cp(("arbitrary", "arbitrary")),
        name="s5_prompt",
    )(u, lb, cl, scan_mult, dvec)
    d0 = dvec[:, :, 0:128]
    sblk = lambda a: pl.BlockSpec((1,) + a.shape[1:], lambda j: (j, 0, 0))
    spblk = lambda a: pl.BlockSpec((2, 1, 128, 128), lambda j: (0, j, 0, 0))
    r0 = N_PROMPT // N_SAMPLE
    y_s, s_re, s_im = pl.pallas_call(
        _s5_sample_body,
        grid=(S5_BLOCKS,),
        in_specs=[
            pl.BlockSpec((N_SAMPLE, 128), lambda j: (r0, j)),
            spblk(lb), spblk(cl), sblk(lbar1), sblk(d0),
            pl.BlockSpec((N_SAMPLE, ns), lambda j: (0, j)),
            pl.BlockSpec((N_SAMPLE, ns), lambda j: (0, j)),
        ],
        out_specs=[
            pl.BlockSpec((N_SAMPLE, 128), lambda j: (0, j)),
            pl.BlockSpec((N_SAMPLE, ns), lambda j: (0, j)),
            pl.BlockSpec((N_SAMPLE, ns), lambda j: (0, j)),
        ],
        out_shape=[
            jax.ShapeDtypeStruct((N_SAMPLE, BRANCH), F32),
            jax.ShapeDtypeStruct((N_SAMPLE, S5_GROUPS * S5_STATE), F32),
            jax.ShapeDtypeStruct((N_SAMPLE, S5_GROUPS * S5_STATE), F32),
        ],
        compiler_params=_cp(("arbitrary",)),
        name="s5_sample",
    )(u, lb, cl, lbar1, d0, st_re, st_im)
    return jnp.concatenate([y, y_s], axis=0), f_re, f_im, s_re, s_im


def _merge_body(o_ref, z_ref, w0_ref, w1_ref, g0_ref, g1_ref, out_ref):
    p0 = jnp.dot(o_ref[...], w0_ref[0].astype(BF16), preferred_element_type=F32)
    p1 = jnp.dot(z_ref[...], w1_ref[0].astype(BF16), preferred_element_type=F32)
    out_ref[...] = (g0_ref[...] * p0 + g1_ref[...] * p1).astype(out_ref.dtype)


def _merge(o, z, w_branch, gates):
    ncb = D_MODEL // TN
    return pl.pallas_call(
        _merge_body,
        grid=(N_TOK // TM, ncb),
        in_specs=[
            pl.BlockSpec((TM, BRANCH), lambda i, j: (i, 0)),
            pl.BlockSpec((TM, BRANCH), lambda i, j: (i, 0)),
            pl.BlockSpec((1, BRANCH, TN), lambda i, j: (0, 0, j)),
            pl.BlockSpec((1, BRANCH, TN), lambda i, j: (1, 0, j)),
            pl.BlockSpec((TM, TN), lambda i, j: (i, j)),
            pl.BlockSpec((TM, TN), lambda i, j: (i, ncb + j)),
        ],
        out_specs=pl.BlockSpec((TM, TN), lambda i, j: (i, j)),
        out_shape=jax.ShapeDtypeStruct((N_TOK, D_MODEL), BF16),
        compiler_params=_cp(("arbitrary", "arbitrary")),
        name="merge",
    )(o, z, w_branch, w_branch, gates, gates)


LN_TM = 128
LN_PROMPT_TILES = N_PROMPT // LN_TM
HALF = D_MODEL // 2


def _pack_bf16_pair(x):
    bits = pltpu.bitcast(x.astype(BF16).astype(F32), jnp.uint32)
    return (bits[:, :HALF] & jnp.uint32(0xFFFF0000)) | (bits[:, HALF:] >> 16)


def _unpack_bf16_pair(w):
    hi = pltpu.bitcast(w & jnp.uint32(0xFFFF0000), F32)
    lo = pltpu.bitcast(w << 16, F32)
    return hi, lo


def _ln_router_body(m_ref, xp_ref, xs_ref, g_ref, b_ref, wr_ref, br_ref, o_ref, op_ref, ids_ref, wts_ref):
    x = jnp.where(pl.program_id(0) < LN_PROMPT_TILES, xp_ref[...], xs_ref[...])
    x1 = _layer_norm(ALPHA * x + m_ref[...]) * g_ref[...] + b_ref[...]
    o_ref[...] = x1
    op_ref[...] = _pack_bf16_pair(x1)
    x_hi = x1.astype(BF16)
    x_mid = (x1 - x_hi.astype(F32)).astype(BF16)
    wr = wr_ref[...]
    both = jnp.dot(x_hi, wr, preferred_element_type=F32)
    logits = (both[:, :128] + both[:, 128:]
              + jnp.dot(x_mid, wr[:, :128], preferred_element_type=F32) + br_ref[...])
    _route(logits, ids_ref, wts_ref)


def _ln_router(mixed, x_p, x_s, g, b, w_r, b_r):
    w_hi = w_r.astype(BF16)
    w_mid = (w_r - w_hi.astype(F32)).astype(BF16)
    wr = jnp.concatenate([w_hi, w_mid], axis=1)
    row_blk = pl.BlockSpec((LN_TM, 128), lambda i: (i, 0))
    return pl.pallas_call(
        _ln_router_body,
        grid=(N_TOK // LN_TM,),
        in_specs=[
            pl.BlockSpec((LN_TM, D_MODEL), lambda i: (i, 0)),
            pl.BlockSpec((LN_TM, D_MODEL), lambda i: (jnp.minimum(i, LN_PROMPT_TILES - 1), 0)),
            pl.BlockSpec((LN_TM, D_MODEL), lambda i: (jnp.maximum(i - LN_PROMPT_TILES, 0), 0)),
            pl.BlockSpec((1, D_MODEL), lambda i: (0, 0)),
            pl.BlockSpec((1, D_MODEL), lambda i: (0, 0)),
            pl.BlockSpec((D_MODEL, 256), lambda i: (0, 0)),
            pl.BlockSpec((1, 128), lambda i: (0, 0)),
        ],
        out_specs=[pl.BlockSpec((LN_TM, D_MODEL), lambda i: (i, 0)),
                   pl.BlockSpec((LN_TM, HALF), lambda i: (i, 0)), row_blk, row_blk],
        out_shape=[jax.ShapeDtypeStruct((N_TOK, D_MODEL), F32),
                   jax.ShapeDtypeStruct((N_TOK, HALF), jnp.uint32),
                   jax.ShapeDtypeStruct((N_TOK, 128), jnp.int32),
                   jax.ShapeDtypeStruct((N_TOK, 128), F32)],
        compiler_params=_cp(("arbitrary",)),
        name="ln1_router",
    )(mixed, x_p, x_s, g, b, wr, b_r)


def _route(logits, ids_ref, wts_ref):
    lane = lax.broadcasted_iota(jnp.int32, logits.shape, 1)
    neg = -jnp.inf
    big = 1 << 20
    gl = jnp.where(lane < 8, logits, neg)
    gmax = jnp.max(gl, axis=1, keepdims=True)
    gidx = jnp.min(jnp.where(gl == gmax, lane, big), axis=1, keepdims=True)
    gprob = 1.0 / jnp.sum(jnp.exp(gl - gmax), axis=1, keepdims=True)
    in_group = (lane >= 8) & (lane < 8 + N_EXPERTS) & (((lane - 8) >> 3) == gidx)
    el = jnp.where(in_group, logits, neg)
    v1 = jnp.max(el, axis=1, keepdims=True)
    i1 = jnp.min(jnp.where(el == v1, lane, big), axis=1, keepdims=True)
    el2 = jnp.where(lane == i1, neg, el)
    v2 = jnp.max(el2, axis=1, keepdims=True)
    i2 = jnp.min(jnp.where(el2 == v2, lane, big), axis=1, keepdims=True)
    e2 = jnp.exp(v2 - v1)
    w1 = gprob / (1.0 + e2)
    w2 = gprob * e2 / (1.0 + e2)
    ids_ref[...] = jnp.where(lane == 0, i1 - 8, jnp.where(lane == 1, i2 - 8, 0))
    wts_ref[...] = jnp.where(lane == 0, w1, jnp.where(lane == 1, w2, 0.0))


N_ASSIGN = 2 * N_TOK
MOE_NBLOCKS = (N_ASSIGN + N_EXPERTS * (MOE_BLOCK - 1) + MOE_BLOCK - 1) // MOE_BLOCK
MOE_ROWS = MOE_NBLOCKS * MOE_BLOCK
MOE_HC = MOE_HIDDEN // MOE_HSPLIT


def _row_copy(src_hbm, row, dst, dst_row, sem):
    return pltpu.make_async_copy(src_hbm.at[pl.ds(row, 1), :], dst.at[pl.ds(dst_row, 1), :], sem)


MOE_MACRO = 4
MOE_NMACRO = N_ASSIGN // (MOE_MACRO * MOE_BLOCK) + N_EXPERTS + 1
MOE_DCOLS = 1024


def _block_copy(acc, t, o_hbm, blk, sem):
    return pltpu.make_async_copy(acc.at[pl.ds(pl.multiple_of(t * MOE_BLOCK, MOE_BLOCK), MOE_BLOCK), :],
                                 o_hbm.at[pl.ds(pl.multiple_of(blk * MOE_BLOCK, MOE_BLOCK), MOE_BLOCK), :], sem)


def _moe_body(mexp_ref, mstart_ref, mnsub_ref, tok_ref, x_hbm, wg_ref, wu_ref, wd_ref, o_hbm,
              stage, xb, acc, ostage, gsem, osem):
    del mexp_ref
    m = pl.program_id(0)
    h = pl.program_id(1)
    nsub = mnsub_ref[m]
    start = mstart_ref[m]
    last_m = pl.num_programs(0) - 1
    last_h = pl.num_programs(1) - 1

    def wait_blocks(n):
        def done(t, c):
            _block_copy(ostage, t, o_hbm, 0, osem).wait()
            return c

        lax.fori_loop(0, n, done, 0)

    def gather_start(step):
        def sub(t, c):
            base = (mstart_ref[step] + t) * MOE_BLOCK
            off = pl.multiple_of(t * MOE_BLOCK, MOE_BLOCK)
            for r in range(MOE_BLOCK):
                _row_copy(x_hbm, tok_ref[base + r], stage, off + r, gsem).start()
            return c

        lax.fori_loop(0, mnsub_ref[step], sub, 0)

    @pl.when((h == 0) & (m == 0))
    def _():
        gather_start(0)

    @pl.when((h == 0) & (nsub > 0))
    def _():
        def landed(t, c):
            rows = pl.ds(pl.multiple_of(t * MOE_BLOCK, MOE_BLOCK), MOE_BLOCK)
            pltpu.make_async_copy(x_hbm.at[pl.ds(0, MOE_BLOCK), :], stage.at[rows, :], gsem).wait()
            return c

        lax.fori_loop(0, nsub, landed, 0)

        def sub(t, c):
            rows = pl.ds(pl.multiple_of(t * MOE_BLOCK, MOE_BLOCK), MOE_BLOCK)
            hi, lo = _unpack_bf16_pair(stage[rows, :])
            xb[rows, 0:HALF] = hi.astype(BF16)
            xb[rows, HALF:D_MODEL] = lo.astype(BF16)
            return c

        lax.fori_loop(0, nsub, sub, 0)

    @pl.when((h == last_h) & (m < last_m))
    def _():
        gather_start(jnp.minimum(m + 1, last_m))

    for ns in range(1, MOE_MACRO + 1):
        @pl.when(nsub == ns)
        def _(ns=ns):
            rows = ns * MOE_BLOCK

            @pl.when(h == 0)
            def _():
                acc[0:rows, :] = jnp.zeros((rows, D_MODEL), F32)

            x = xb[0:rows, :]
            hg = jnp.dot(x, wg_ref[0].astype(BF16), preferred_element_type=F32)
            hu = jnp.dot(x, wu_ref[0].astype(BF16), preferred_element_type=F32)
            hh = (hg * _sigmoid(hg) * hu).astype(BF16)
            wd = wd_ref[0].astype(BF16)
            for cc in range(D_MODEL // MOE_DCOLS):
                cols = slice(cc * MOE_DCOLS, (cc + 1) * MOE_DCOLS)
                acc[0:rows, cols] += jnp.dot(hh, wd[:, cols], preferred_element_type=F32)

    @pl.when((h == last_h) & (m > 0))
    def _():
        wait_blocks(mnsub_ref[jnp.maximum(m - 1, 0)])

    @pl.when((h == last_h) & (nsub > 0))
    def _():
        def put(t, c):
            rows = pl.ds(pl.multiple_of(t * MOE_BLOCK, MOE_BLOCK), MOE_BLOCK)
            ostage[rows, :] = _pack_bf16_pair(acc[rows, :])
            _block_copy(ostage, t, o_hbm, start + t, osem).start()
            return c

        lax.fori_loop(0, nsub, put, 0)

        @pl.when(m == last_m)
        def _():
            wait_blocks(nsub)

    @pl.when((h == last_h) & (nsub == 0))
    def _():
        ostage[0:MOE_BLOCK, :] = jnp.zeros((MOE_BLOCK, HALF), jnp.uint32)
        for t in range(MOE_MACRO):
            @pl.when(start + t < MOE_NBLOCKS)
            def _(t=t):
                cp = _block_copy(ostage, 0, o_hbm, start + t, osem)
                cp.start()
                cp.wait()


def _moe_experts(x1p, wg, wu, wd, mexp, mstart, mnsub, row_tok):
    grid_spec = pltpu.PrefetchScalarGridSpec(
        num_scalar_prefetch=4,
        grid=(MOE_NMACRO, MOE_HSPLIT),
        in_specs=[
            pl.BlockSpec(memory_space=pl.ANY),
            pl.BlockSpec((1, D_MODEL, MOE_HC), lambda m, h, me, ms, mn, rt: (me[m], 0, h)),
            pl.BlockSpec((1, D_MODEL, MOE_HC), lambda m, h, me, ms, mn, rt: (me[m], 0, h)),
            pl.BlockSpec((1, MOE_HC, D_MODEL), lambda m, h, me, ms, mn, rt: (me[m], h, 0)),
        ],
        out_specs=pl.BlockSpec(memory_space=pl.ANY),
        scratch_shapes=[pltpu.VMEM((MOE_MACRO * MOE_BLOCK, HALF), jnp.uint32),
                        pltpu.VMEM((MOE_MACRO * MOE_BLOCK, D_MODEL), BF16),
                        pltpu.VMEM((MOE_MACRO * MOE_BLOCK, D_MODEL), F32),
                        pltpu.VMEM((MOE_MACRO * MOE_BLOCK, HALF), jnp.uint32),
                        pltpu.SemaphoreType.DMA(()), pltpu.SemaphoreType.DMA(())],
    )
    return pl.pallas_call(
        _moe_body,
        grid_spec=grid_spec,
        out_shape=jax.ShapeDtypeStruct((MOE_ROWS, HALF), jnp.uint32),
        compiler_params=_cp(("arbitrary", "arbitrary")),
        name="moe_experts",
    )(mexp, mstart, mnsub, row_tok, x1p, wg, wu, wd)


CMB_TM = 128
CMB_PROMPT_TILES = N_PROMPT // CMB_TM


def _combine_body(pos_ref, eo_hbm, wts_ref, x1_ref, g_ref, b_ref, yp_ref, ys_ref, buf, sem):
    i = pl.program_id(0)
    slot = i % 2

    def fetch(tile, s):
        base = 2 * tile * CMB_TM
        for r in range(CMB_TM):
            _row_copy(eo_hbm, pos_ref[base + 2 * r], buf.at[s, 0], r, sem.at[s]).start()
            _row_copy(eo_hbm, pos_ref[base + 2 * r + 1], buf.at[s, 1], r, sem.at[s]).start()

    @pl.when(i == 0)
    def _():
        fetch(0, 0)

    for s in range(2):
        @pl.when((i + 1 < pl.num_programs(0)) & (slot == s))
        def _(s=s):
            fetch(i + 1, 1 - s)

    for k in range(2):
        pltpu.make_async_copy(eo_hbm.at[pl.ds(0, CMB_TM), :], buf.at[slot, k], sem.at[slot]).wait()
    w = wts_ref[...]
    hi0, lo0 = _unpack_bf16_pair(buf[slot, 0])
    hi1, lo1 = _unpack_bf16_pair(buf[slot, 1])
    w0, w1 = w[:, 0:1], w[:, 1:2]
    y = jnp.concatenate([w0 * hi0 + w1 * hi1, w0 * lo0 + w1 * lo1], axis=1)
    x2 = _layer_norm(ALPHA * x1_ref[...] + y) * g_ref[...] + b_ref[...]

    @pl.when(i < CMB_PROMPT_TILES)
    def _():
        yp_ref[...] = x2

    @pl.when(i >= CMB_PROMPT_TILES)
    def _():
        ys_ref[...] = x2


def _combine(pos, eo, wts, x1, g, b):
    grid_spec = pltpu.PrefetchScalarGridSpec(
        num_scalar_prefetch=1,
        grid=(N_TOK // CMB_TM,),
        in_specs=[
            pl.BlockSpec(memory_space=pl.ANY),
            pl.BlockSpec((CMB_TM, 128), lambda i, p: (i, 0)),
            pl.BlockSpec((CMB_TM, D_MODEL), lambda i, p: (i, 0)),
            pl.BlockSpec((1, D_MODEL), lambda i, p: (0, 0)),
            pl.BlockSpec((1, D_MODEL), lambda i, p: (0, 0)),
        ],
        out_specs=[
            pl.BlockSpec((CMB_TM, D_MODEL), lambda i, p: (jnp.minimum(i, CMB_PROMPT_TILES - 1), 0)),
            pl.BlockSpec((CMB_TM, D_MODEL), lambda i, p: (0, 0)),
        ],
        scratch_shapes=[pltpu.VMEM((2, 2, CMB_TM, HALF), jnp.uint32), pltpu.SemaphoreType.DMA((2,))],
    )
    return pl.pallas_call(
        _combine_body,
        grid_spec=grid_spec,
        out_shape=[jax.ShapeDtypeStruct((N_PROMPT, D_MODEL), F32),
                   jax.ShapeDtypeStruct((N_SAMPLE, D_MODEL), F32)],
        compiler_params=_cp(("arbitrary",)),
        name="combine_ln2",
    )(pos, eo, wts, x1, g, b)


def _route_positions(ids):
    eid = ids[:, :2].reshape(-1)
    onehot = (eid[:, None] == jnp.arange(N_EXPERTS, dtype=jnp.int32)[None, :]).astype(jnp.int32)
    csum = jnp.cumsum(onehot, axis=0)
    rank = jnp.take_along_axis(csum, eid[:, None], axis=1)[:, 0] - 1
    counts = csum[-1]
    nblk_e = (counts + MOE_BLOCK - 1) // MOE_BLOCK
    bend = jnp.cumsum(nblk_e)
    bstart = bend - nblk_e
    pos = bstart[eid] * MOE_BLOCK + rank
    row_tok = jnp.zeros((MOE_ROWS,), jnp.int32).at[pos].set(jnp.arange(N_ASSIGN, dtype=jnp.int32) // 2)
    nstep_e = (nblk_e + MOE_MACRO - 1) // MOE_MACRO
    send = jnp.cumsum(nstep_e)
    n_steps, n_blocks = send[-1], bend[-1]
    m = jnp.arange(MOE_NMACRO, dtype=jnp.int32)
    e_of_m = jnp.minimum(jnp.searchsorted(send, m, side='right'), N_EXPERTS - 1).astype(jnp.int32)
    local = m - (send - nstep_e)[e_of_m]
    valid = m < n_steps
    mexp = jnp.where(valid, e_of_m, e_of_m[jnp.maximum(n_steps - 1, 0)])
    mstart = jnp.where(valid, bstart[e_of_m] + MOE_MACRO * local, n_blocks + MOE_MACRO * (m - n_steps))
    mnsub = jnp.where(valid, jnp.clip(nblk_e[e_of_m] - MOE_MACRO * local, 0, MOE_MACRO), 0)
    return (pos.astype(jnp.int32), row_tok, mexp.astype(jnp.int32), mstart.astype(jnp.int32),
            mnsub.astype(jnp.int32))


def kernel(x_prompt, x_sample, state_gla, state_s5_re, state_s5_im, w_in, w_gla_gate_up, b_gla_gate_up, w_gla_norm, s5_a_re, s5_a_im, s5_b_re, s5_b_im, s5_c_re, s5_c_im, s5_d, s5_log_dt, w_s5_glu, b_s5_glu, w_branch, w_out, ln1_g, ln1_b, w_router_group, b_router_group, w_router_expert, b_router_expert, w_moe_gate, w_moe_up, w_moe_down, ln2_g, ln2_b):
    x_p = x_prompt.reshape(N_PROMPT, D_MODEL)
    x_s = x_sample.reshape(N_SAMPLE, D_MODEL)
    x_bf = jnp.concatenate([x_p.astype(BF16), x_s.astype(BF16)], axis=0)

    w_in_t = w_in.T
    qkvr = _mm_t(x_bf, w_in_t, row0=0, ncols=COL_A, tn=TN, out_dtype=F32, name="proj_qkvr")
    a_low = _mm_t(x_bf, w_in_t, row0=COL_A, ncols=128, tn=128, out_dtype=F32, name="proj_a")
    u = _mm_t(x_bf, w_in_t, row0=COL_U, ncols=BRANCH, tn=TN, out_dtype=F32, name="proj_u")
    gates = _mm_t(x_bf, w_in_t, row0=COL_G, ncols=2 * D_MODEL, tn=TN, out_dtype=F32, name="proj_gates",
                  epilogue=_sigmoid)

    wgu = jnp.pad(w_gla_gate_up, ((0, 128 - GLA_RANK), (0, 0)))
    bgu = b_gla_gate_up.reshape(1, GLA_DK)
    wn = w_gla_norm.reshape(1, BRANCH)
    o_p, gla_p_t = _gla_prompt(qkvr, a_low, wgu, bgu, wn)
    o_s, gla_s = _gla_sample(qkvr, a_low, wgu, bgu, wn, state_gla)
    o_all = jnp.concatenate([o_p, o_s], axis=0)
    gla_p = jnp.swapaxes(gla_p_t, 2, 3)

    s5w = _s5_weights(s5_a_re, s5_a_im, s5_b_re, s5_b_im, s5_c_re, s5_c_im, s5_d, s5_log_dt)
    y, s5_re_p, s5_im_p, s5_re_s, s5_im_s = _s5(
        u, s5w, state_s5_re.reshape(N_SAMPLE, -1), state_s5_im.reshape(N_SAMPLE, -1))
    z = _mm(y, w_s5_glu, col0=0, ncols=BRANCH, tn=TN, out_dtype=BF16, name="s5_glu",
            epilogue=lambda acc, yt, bt: yt * _sigmoid(acc + bt),
            extra=(y, b_s5_glu.reshape(1, BRANCH)),
            extra_specs=(pl.BlockSpec((TM, TN), lambda i, j: (i, j)), pl.BlockSpec((1, TN), lambda i, j: (0, j))))

    pre = _merge(o_all, z, w_branch, gates)
    w_r = jnp.concatenate([w_router_group,
                           jnp.moveaxis(w_router_expert, 0, 1).reshape(D_MODEL, N_EXPERTS),
                           jnp.zeros((D_MODEL, 128 - 8 - N_EXPERTS), F32)], axis=1)
    b_r = jnp.concatenate([b_router_group, b_router_expert.reshape(-1),
                           jnp.zeros((128 - 8 - N_EXPERTS,), F32)]).reshape(1, 128)
    mixed = _mm(pre, w_out, col0=0, ncols=D_MODEL, tn=TN, out_dtype=F32, name="out_proj")
    x1, x1p, ids, wts = _ln_router(mixed, x_p, x_s, ln1_g.reshape(1, D_MODEL), ln1_b.reshape(1, D_MODEL), w_r, b_r)

    pos, row_tok, mexp, mstart, mnsub = _route_positions(ids)
    eo = _moe_experts(x1p,
                      w_moe_gate.reshape(N_EXPERTS, D_MODEL, MOE_HIDDEN),
                      w_moe_up.reshape(N_EXPERTS, D_MODEL, MOE_HIDDEN),
                      w_moe_down.reshape(N_EXPERTS, MOE_HIDDEN, D_MODEL),
                      mexp, mstart, mnsub, row_tok)
    y_p, y_s = _combine(pos, eo, wts, x1, ln2_g.reshape(1, D_MODEL), ln2_b.reshape(1, D_MODEL))

    return (y_p.reshape(N_PROMPT_SEQ, SEQ, D_MODEL), y_s.reshape(N_SAMPLE, 1, D_MODEL),
            gla_p,
            s5_re_p.reshape(N_PROMPT_SEQ, S5_GROUPS, S5_STATE), s5_im_p.reshape(N_PROMPT_SEQ, S5_GROUPS, S5_STATE),
            gla_s,
            s5_re_s.reshape(N_SAMPLE, S5_GROUPS, S5_STATE), s5_im_s.reshape(N_SAMPLE, S5_GROUPS, S5_STATE))
```

```python
import functools
import math

import jax
import jax.numpy as jnp
import numpy as np
from jax import lax
from jax.experimental import pallas as pl
from jax.experimental.pallas import tpu as pltpu

F32 = jnp.float32
BF16 = jnp.bfloat16
HIGHEST = lax.Precision.HIGHEST

D_MODEL = 4096
N_PROMPT_SEQ = 4
SEQ = 2048
N_SAMPLE = 128
N_PROMPT = N_PROMPT_SEQ * SEQ
N_TOK = N_PROMPT + N_SAMPLE
BRANCH = D_MODEL // 2
GLA_HEADS = 4
GLA_DK = D_MODEL // 4
GLA_HDK = GLA_DK // GLA_HEADS
GLA_HDV = BRANCH // GLA_HEADS
GLA_RANK = 16
GLA_TAU = 16.0
GLA_CHUNK = 64
GLA_LEVELS = 6
S5_GROUP = 16
S5_GROUPS = BRANCH // S5_GROUP
S5_STATE = 64
S5_CHUNK = 8
S5_LANE_GROUPS = 8
S5_BLOCKS = S5_GROUPS // S5_LANE_GROUPS
N_EXPERTS = 64
MOE_HIDDEN = D_MODEL // 8
MOE_BLOCK = 128
MOE_HSPLIT = 2
LN_EPS = 1e-5
ALPHA = 2.0 ** 0.25
COL_A = 6144
COL_U = 6160
COL_G = 8208
IN_COLS = 16400

VMEM_LIMIT = 56 * 1024 * 1024
TM = 1040
TN = 512


_NT = (((1,), (1,)), ((), ()))
_TN = (((0,), (0,)), ((), ()))


def _cp(sem):
    return pltpu.CompilerParams(dimension_semantics=sem, vmem_limit_bytes=VMEM_LIMIT)


def _sigmoid(x):
    return 1.0 / (1.0 + jnp.exp(-x))


def _log_sigmoid(z):
    return jnp.minimum(z, 0.0) - jnp.log1p(jnp.exp(-jnp.abs(z)))


def _gelu_tanh(x):
    return 0.5 * x * (1.0 + jnp.tanh(math.sqrt(2.0 / math.pi) * (x + 0.044715 * (x * x * x))))


def _layer_norm(x):
    mu = jnp.mean(x, axis=-1, keepdims=True)
    xc = x - mu
    var = jnp.mean(xc * xc, axis=-1, keepdims=True)
    return xc * lax.rsqrt(var + LN_EPS)


def _mm_body(a_ref, w_ref, *rest, epilogue):
    *extra, o_ref = rest
    acc = jnp.dot(a_ref[...].astype(BF16), w_ref[...].astype(BF16), preferred_element_type=F32)
    o_ref[...] = epilogue(acc, *[e[...] for e in extra]).astype(o_ref.dtype)


def _mm(a, w, *, col0, ncols, tn, out_dtype, name, epilogue=lambda acc: acc, extra=(), extra_specs=()):
    m, k = a.shape
    cb0 = col0 // tn
    return pl.pallas_call(
        functools.partial(_mm_body, epilogue=epilogue),
        grid=(m // TM, ncols // tn),
        in_specs=[pl.BlockSpec((TM, k), lambda i, j: (i, 0)),
                  pl.BlockSpec((k, tn), lambda i, j: (0, cb0 + j)),
                  *extra_specs],
        out_specs=pl.BlockSpec((TM, tn), lambda i, j: (i, j)),
        out_shape=jax.ShapeDtypeStruct((m, ncols), out_dtype),
        compiler_params=_cp(("arbitrary", "arbitrary")),
        name=name,
    )(a, w, *extra)


def _mm_t_body(a_ref, wt_ref, o_ref, *, epilogue):
    acc = lax.dot_general(a_ref[...], wt_ref[...].astype(BF16), _NT, preferred_element_type=F32)
    o_ref[...] = epilogue(acc).astype(o_ref.dtype)


def _mm_t(a, wt, *, row0, ncols, tn, out_dtype, name, epilogue=lambda acc: acc):
    m, k = a.shape
    return pl.pallas_call(
        functools.partial(_mm_t_body, epilogue=epilogue),
        grid=(m // TM, ncols // tn),
        in_specs=[pl.BlockSpec((TM, k), lambda i, j: (i, 0)),
                  pl.BlockSpec((pl.Element(tn), pl.Element(k)),
                               lambda i, j: (pl.multiple_of(row0 + j * tn, 8), 0))],
        out_specs=pl.BlockSpec((TM, tn), lambda i, j: (i, j)),
        out_shape=jax.ShapeDtypeStruct((m, ncols), out_dtype),
        compiler_params=_cp(("arbitrary", "arbitrary")),
        name=name,
    )(a, wt)


def _gla_coeff_matrix():
    c = GLA_CHUNK
    t = np.arange(c)[:, None]
    u = np.arange(c)[None, :]
    blocks = [(u <= t), (u > t)]
    for lvl in range(1, GLA_LEVELS + 1):
        m = 1 << lvl
        half = m // 2
        mid = (t // m) * m + half - 1
        lower = (t % m) >= half
        blocks.append(np.where(lower, (u > mid) & (u <= t), (u > t) & (u <= mid)))
    return np.concatenate(blocks, axis=0).astype(np.float32)


def _gla_level_masks():
    c = GLA_CHUNK
    t = lax.broadcasted_iota(jnp.int32, (c, c), 0)
    s = lax.broadcasted_iota(jnp.int32, (c, c), 1)
    masks = []
    for lvl in range(1, GLA_LEVELS + 1):
        m = 1 << lvl
        half = m // 2
        masks.append(((t >> lvl) == (s >> lvl)) & ((t & (m - 1)) >= half) & ((s & (m - 1)) < half))
    return masks


def _gla_out_norm(o, r, wn):
    return _layer_norm(o) * wn * (r * _sigmoid(r))


def _gla_prompt_body(q_ref, k_ref, v_ref, r_ref, a_ref, wgu_ref, bgu_ref, wn_ref, cm_ref,
                     o_ref, st_ref, s_scr, *, n_sub):
    c = GLA_CHUNK

    @pl.when(pl.program_id(2) == 0)
    def _():
        s_scr[...] = jnp.zeros_like(s_scr)

    masks = _gla_level_masks()
    cm = cm_ref[...]
    nt, tn = _NT, _TN

    def chunk(ci, carry):
        for hh in range(GLA_HP):
            head_chunk(ci, hh)
        return carry

    def head_chunk(ci, hh):
        rows = pl.ds(pl.multiple_of(ci * c, c), c)
        kcols = slice(hh * GLA_HDK, (hh + 1) * GLA_HDK)
        vcols = slice(hh * GLA_HDV, (hh + 1) * GLA_HDV)
        q = q_ref[rows, kcols] * (GLA_HDK ** -0.5)
        k = k_ref[rows, kcols]
        v = v_ref[rows, vcols]
        z = jnp.dot(a_ref[rows, :], wgu_ref[:, kcols], preferred_element_type=F32, precision=HIGHEST) + bgu_ref[:, kcols]
        g = _log_sigmoid(z) * (1.0 / GLA_TAU)
        g_hi = g.astype(BF16)
        r1 = g - g_hi.astype(F32)
        g_mid = r1.astype(BF16)
        g_lo = (r1 - g_mid.astype(F32)).astype(BF16)
        g3 = jnp.concatenate([g_hi, g_mid, g_lo], axis=0)
        f = jnp.exp(jnp.dot(cm, g3, preferred_element_type=F32))
        st = s_scr[hh]
        vb = v.astype(BF16)
        o = lax.dot_general((q * f[0:c]).astype(BF16), st.astype(BF16), nt, preferred_element_type=F32)
        scores = jnp.zeros((c, c), F32)
        for lvl in range(GLA_LEVELS):
            fl = f[(2 + lvl) * c:(3 + lvl) * c]
            p = lax.dot_general((q * fl).astype(BF16), (k * fl).astype(BF16), nt, preferred_element_type=F32)
            scores = scores + jnp.where(masks[lvl], p, 0.0)
        diag = jnp.sum(q * k, axis=1, keepdims=True)
        o = o + jnp.dot(scores.astype(BF16), vb, preferred_element_type=F32) + diag * v
        kd = (k * f[c:2 * c]).astype(BF16)
        s_scr[hh] = st * f[c - 1:c, :] + lax.dot_general(vb, kd, tn, preferred_element_type=F32)
        o_ref[rows, vcols] = _gla_out_norm(o, r_ref[rows, vcols], wn_ref[:, vcols]).astype(o_ref.dtype)

    lax.fori_loop(0, n_sub, chunk, 0, unroll=True)

    @pl.when(pl.program_id(2) == pl.num_programs(2) - 1)
    def _():
        st_ref[0] = s_scr[...]


GLA_HP = 4


def _gla_prompt(qkvr, a_low, wgu, bgu, wn):
    n_sub = 4
    tt = GLA_CHUNK * n_sub
    nt_steps = SEQ // tt
    cm = jnp.asarray(np.tile(_gla_coeff_matrix(), (1, 3)), dtype=BF16)
    rows = lambda b, h, c: b * nt_steps + c
    wk, wv = GLA_HP * GLA_HDK, GLA_HP * GLA_HDV
    n_groups = GLA_HEADS // GLA_HP
    o, st = pl.pallas_call(
        functools.partial(_gla_prompt_body, n_sub=n_sub),
        grid=(N_PROMPT_SEQ, n_groups, nt_steps),
        in_specs=[
            pl.BlockSpec((tt, wk), lambda b, h, c: (rows(b, h, c), h)),
            pl.BlockSpec((tt, wk), lambda b, h, c: (rows(b, h, c), n_groups + h)),
            pl.BlockSpec((tt, wv), lambda b, h, c: (rows(b, h, c), n_groups + h)),
            pl.BlockSpec((tt, wv), lambda b, h, c: (rows(b, h, c), 2 * n_groups + h)),
            pl.BlockSpec((tt, 128), lambda b, h, c: (rows(b, h, c), 0)),
            pl.BlockSpec((128, wk), lambda b, h, c: (0, h)),
            pl.BlockSpec((1, wk), lambda b, h, c: (0, h)),
            pl.BlockSpec((1, wv), lambda b, h, c: (0, h)),
            pl.BlockSpec(cm.shape, lambda b, h, c: (0, 0)),
        ],
        out_specs=[
            pl.BlockSpec((tt, wv), lambda b, h, c: (rows(b, h, c), h)),
            pl.BlockSpec((1, GLA_HP, GLA_HDV, GLA_HDK), lambda b, h, c: (b, h, 0, 0)),
        ],
        out_shape=[
            jax.ShapeDtypeStruct((N_PROMPT, BRANCH), BF16),
            jax.ShapeDtypeStruct((N_PROMPT_SEQ, GLA_HEADS, GLA_HDV, GLA_HDK), F32),
        ],
        scratch_shapes=[pltpu.VMEM((GLA_HP, GLA_HDV, GLA_HDK), F32)],
        compiler_params=_cp(("arbitrary", "arbitrary", "arbitrary")),
        name="gla_prompt",
    )(qkvr, qkvr, qkvr, qkvr, a_low, wgu, bgu, wn, cm)
    return o, st


GLA_SB = 16


def _gla_sample_body(q_ref, k_ref, v_ref, r_ref, a_ref, wgu_ref, bgu_ref, wn_ref, s_ref, o_ref, so_ref):
    q = q_ref[...] * (GLA_HDK ** -0.5)
    k = k_ref[...]
    v = v_ref[...]
    z = jnp.dot(a_ref[...], wgu_ref[...], preferred_element_type=F32, precision=HIGHEST) + bgu_ref[...]
    eg = jnp.exp(_log_sigmoid(z) * (1.0 / GLA_TAU))
    qe = (q * eg).astype(BF16)
    eg_t = eg.T
    k_t = k.T
    rows = []
    for n in range(GLA_SB):
        s0 = s_ref[n, 0]
        rows.append(jnp.dot(qe, s0.astype(BF16), preferred_element_type=F32)[n:n + 1])
        so_ref[n, 0] = s0 * eg_t[:, n:n + 1] + k_t[:, n:n + 1] * v[n:n + 1, :]
    o = jnp.concatenate(rows, axis=0) + jnp.sum(q * k, axis=1, keepdims=True) * v
    o_ref[...] = _gla_out_norm(o, r_ref[...], wn_ref[...]).astype(o_ref.dtype)


def _gla_sample(qkvr, a_low, wgu, bgu, wn, state):
    r0 = N_PROMPT // GLA_SB
    o, st = pl.pallas_call(
        _gla_sample_body,
        grid=(GLA_HEADS, N_SAMPLE // GLA_SB),
        in_specs=[
            pl.BlockSpec((GLA_SB, GLA_HDK), lambda h, i: (r0 + i, h)),
            pl.BlockSpec((GLA_SB, GLA_HDK), lambda h, i: (r0 + i, GLA_HEADS + h)),
            pl.BlockSpec((GLA_SB, GLA_HDV), lambda h, i: (r0 + i, GLA_HEADS + h)),
            pl.BlockSpec((GLA_SB, GLA_HDV), lambda h, i: (r0 + i, 2 * GLA_HEADS + h)),
            pl.BlockSpec((GLA_SB, 128), lambda h, i: (r0 + i, 0)),
            pl.BlockSpec((128, GLA_HDK), lambda h, i: (0, h)),
            pl.BlockSpec((1, GLA_HDK), lambda h, i: (0, h)),
            pl.BlockSpec((1, GLA_HDV), lambda h, i: (0, h)),
            pl.BlockSpec((GLA_SB, 1, GLA_HDK, GLA_HDV), lambda h, i: (i, h, 0, 0)),
        ],
        out_specs=[
            pl.BlockSpec((GLA_SB, GLA_HDV), lambda h, i: (i, h)),
            pl.BlockSpec((GLA_SB, 1, GLA_HDK, GLA_HDV), lambda h, i: (i, h, 0, 0)),
        ],
        out_shape=[
            jax.ShapeDtypeStruct((N_SAMPLE, BRANCH), BF16),
            jax.ShapeDtypeStruct(state.shape, F32),
        ],
        compiler_params=_cp(("arbitrary", "arbitrary")),
        name="gla_sample",
    )(qkvr, qkvr, qkvr, qkvr, a_low, wgu, bgu, wn, state)
    return o, st


def _s5_weights(a_re, a_im, b_re, b_im, c_re, c_im, d, log_dt):
    L = S5_CHUNK
    lam_re = jnp.minimum(a_re, -1e-4)
    lam_im = a_im
    dt = jnp.exp(log_dt)[:, None]
    kk = jnp.arange(L + 1, dtype=F32)[:, None, None]
    pow_re = jnp.exp(lam_re * dt * kk) * jnp.cos(lam_im * dt * kk)
    pow_im = jnp.exp(lam_re * dt * kk) * jnp.sin(lam_im * dt * kk)
    lbar_re, lbar_im = pow_re[1], pow_im[1]
    den = lam_re * lam_re + lam_im * lam_im
    f_re = ((lbar_re - 1.0) * lam_re + lbar_im * lam_im) / den
    f_im = (lbar_im * lam_re - (lbar_re - 1.0) * lam_im) / den
    bb_re = f_re[..., None] * b_re - f_im[..., None] * b_im
    bb_im = f_re[..., None] * b_im + f_im[..., None] * b_re
    nb, ng = S5_BLOCKS, S5_LANE_GROUPS
    bb_re_t = jnp.swapaxes(bb_re, 1, 2)
    bb_im_t = jnp.swapaxes(bb_im, 1, 2)
    lb_re = (pow_re[:L, :, None, :] * bb_re_t - pow_im[:L, :, None, :] * bb_im_t).reshape(L, nb, 128, S5_STATE)
    lb_im = (pow_re[:L, :, None, :] * bb_im_t + pow_im[:L, :, None, :] * bb_re_t).reshape(L, nb, 128, S5_STATE)
    cl_re = (c_re[None] * pow_re[:, :, None, :] - c_im[None] * pow_im[:, :, None, :]).reshape(L + 1, nb, 128, S5_STATE)
    cl_im = (c_re[None] * pow_im[:, :, None, :] + c_im[None] * pow_re[:, :, None, :]).reshape(L + 1, nb, 128, S5_STATE)
    lb = jnp.concatenate([lb_re, lb_im], axis=-1)
    cl = jnp.concatenate([cl_re, -cl_im], axis=-1)
    n_steps = int(math.log2(SEQ // L))
    mult = (L * (2.0 ** jnp.arange(n_steps, dtype=F32)))[:, None, None]
    sc_re = jnp.exp(lam_re * dt * mult) * jnp.cos(lam_im * dt * mult)
    sc_im = jnp.exp(lam_re * dt * mult) * jnp.sin(lam_im * dt * mult)

    def state_lanes(x):
        lead = x.shape[:-2]
        return jnp.moveaxis(x.reshape(lead + (nb, ng * S5_STATE)), -2, 0)

    scan_mult = jnp.concatenate([state_lanes(sc_re), state_lanes(sc_im)], axis=-1)
    lbar1 = jnp.concatenate([state_lanes(lbar_re[None]), state_lanes(lbar_im[None])], axis=-1)
    dvec = jnp.tile(d.reshape(nb, 1, 128), (1, 1, L))
    return lb, cl, scan_mult, lbar1, dvec


def _s5_expand(src):
    ng, p = S5_LANE_GROUPS, S5_STATE
    lane = lax.broadcasted_iota(jnp.int32, src.shape, 1)
    other = pltpu.roll(src, p, 1)
    re2 = jnp.where(lane < p, src, other)
    im2 = jnp.where(lane < p, other, src)
    full = jnp.concatenate([re2] * (ng // 2) + [im2] * (ng // 2), axis=1)
    row = lax.broadcasted_iota(jnp.int32, full.shape, 0)
    col = lax.broadcasted_iota(jnp.int32, full.shape, 1)
    same_group = (row >> 4) == ((col >> 6) & (ng - 1))
    return jnp.where(same_group, full, 0.0)


def _s5_lag_kernel(zpow, cpow0):
    return lax.dot_general(zpow, cpow0, _NT, preferred_element_type=F32, precision=HIGHEST)


def _s5_build_weights(lb_ref, cl_ref, t_scr, wz_scr, wc_scr):
    L = S5_CHUNK
    cpow0 = _s5_expand(cl_ref[0, 0])
    t_scr[...] = jnp.zeros_like(t_scr)
    for k in range(L):
        zpow = _s5_expand(lb_ref[k, 0])
        wz_scr[(L - 1 - k) * 128:(L - k) * 128, :] = zpow.astype(BF16)
        wc_scr[k * 128:(k + 1) * 128, :] = _s5_expand(cl_ref[k + 1, 0]).astype(BF16)
        bd = _s5_lag_kernel(zpow, cpow0).astype(BF16)
        for sp in range(L - k):
            t_scr[sp * 128:(sp + 1) * 128, (sp + k) * 128:(sp + k + 1) * 128] = bd


def _s5_prompt_body(u_ref, lb_ref, cl_ref, sm_ref, d_ref, y_ref, fre_ref, fim_ref, t_ref, wz_ref, wc_ref,
                    *, n_steps):
    L = S5_CHUNK
    n_rows = SEQ // L
    ns = S5_LANE_GROUPS * S5_STATE

    @pl.when(pl.program_id(1) == 0)
    def _():
        _s5_build_weights(lb_ref, cl_ref, t_ref, wz_ref, wc_ref)

    v = jnp.concatenate([u_ref[pl.ds(s, n_rows, stride=L), :] for s in range(L)], axis=1)
    vb = v.astype(BF16)
    z = jnp.dot(vb, wz_ref[...], preferred_element_type=F32)
    hr, hi = z[:, :ns], z[:, ns:]
    pos = lax.broadcasted_iota(jnp.int32, (n_rows, ns), 0)
    sm = sm_ref[0]
    for d in range(n_steps):
        sh = 1 << d
        ar, ai = sm[d:d + 1, :ns], sm[d:d + 1, ns:]
        keep = pos >= sh
        pr = jnp.where(keep, pltpu.roll(hr, sh, 0), 0.0)
        pi = jnp.where(keep, pltpu.roll(hi, sh, 0), 0.0)
        hr, hi = hr + ar * pr - ai * pi, hi + ar * pi + ai * pr
    fre_ref[0] = hr[n_rows - 1:n_rows]
    fim_ref[0] = hi[n_rows - 1:n_rows]
    first = pos >= 1
    h_prev = jnp.concatenate([jnp.where(first, pltpu.roll(hr, 1, 0), 0.0),
                              jnp.where(first, pltpu.roll(hi, 1, 0), 0.0)], axis=1)
    y = (jnp.dot(vb, t_ref[...], preferred_element_type=F32)
         + lax.dot_general(h_prev.astype(BF16), wc_ref[...], _NT, preferred_element_type=F32)
         + d_ref[0] * v)
    y = _gelu_tanh(y)
    for s in range(L):
        y_ref[pl.ds(s, n_rows, stride=L), :] = y[:, s * 128:(s + 1) * 128]


def _s5_sample_body(u_ref, lb_ref, cl_ref, l1_ref, d_ref, hre_ref, him_ref, y_ref, sre_ref, sim_ref):
    ns = S5_LANE_GROUPS * S5_STATE
    bbar = _s5_expand(lb_ref[0, 0])
    k0 = _s5_lag_kernel(bbar, _s5_expand(cl_ref[0, 0])).astype(BF16)
    wc0 = _s5_expand(cl_ref[1, 0]).astype(BF16)
    us = u_ref[...]
    usb = us.astype(BF16)
    h0r, h0i = hre_ref[...], him_ref[...]
    l1 = l1_ref[0]
    bu = jnp.dot(usb, bbar.astype(BF16), preferred_element_type=F32)
    sre_ref[...] = l1[:, :ns] * h0r - l1[:, ns:] * h0i + bu[:, :ns]
    sim_ref[...] = l1[:, :ns] * h0i + l1[:, ns:] * h0r + bu[:, ns:]
    h0 = jnp.concatenate([h0r, h0i], axis=1).astype(BF16)
    ys = (jnp.dot(usb, k0, preferred_element_type=F32)
          + lax.dot_general(h0, wc0, _NT, preferred_element_type=F32)
          + d_ref[0] * us)
    y_ref[...] = _gelu_tanh(ys)


def _s5(u, weights, st_re, st_im):
    lb, cl, scan_mult, lbar1, dvec = weights
    L = S5_CHUNK
    ns = S5_LANE_GROUPS * S5_STATE
    n_steps = scan_mult.shape[1]
    wblk = lambda a: pl.BlockSpec((1,) + a.shape[1:], lambda j, b: (j, 0, 0))
    pblk = lambda a: pl.BlockSpec((a.shape[0], 1, 128, 128), lambda j, b: (0, j, 0, 0))
    mat = pltpu.VMEM((L * 128, L * 128), BF16)
    y, f_re, f_im = pl.pallas_call(
        functools.partial(_s5_prompt_body, n_steps=n_steps),
        grid=(S5_BLOCKS, N_PROMPT_SEQ),
        in_specs=[
            pl.BlockSpec((SEQ, 128), lambda j, b: (b, j)),
            pblk(lb), pblk(cl), wblk(scan_mult), wblk(dvec),
        ],
        out_specs=[
            pl.BlockSpec((SEQ, 128), lambda j, b: (b, j)),
            pl.BlockSpec((1, 1, ns), lambda j, b: (b, 0, j)),
            pl.BlockSpec((1, 1, ns), lambda j, b: (b, 0, j)),
        ],
        out_shape=[
            jax.ShapeDtypeStruct((N_PROMPT, BRANCH), F32),
            jax.ShapeDtypeStruct((N_PROMPT_SEQ, 1, S5_GROUPS * S5_STATE), F32),
            jax.ShapeDtypeStruct((N_PROMPT_SEQ, 1, S5_GROUPS * S5_STATE), F32),
        ],
        scratch_shapes=[mat, mat, mat],
        compiler_params=_cp(("arbitrary", "arbitrary")),
        name="s5_prompt",
    )(u, lb, cl, scan_mult, dvec)
    d0 = dvec[:, :, 0:128]
    sblk = lambda a: pl.BlockSpec((1,) + a.shape[1:], lambda j: (j, 0, 0))
    spblk = lambda a: pl.BlockSpec((2, 1, 128, 128), lambda j: (0, j, 0, 0))
    r0 = N_PROMPT // N_SAMPLE
    y_s, s_re, s_im = pl.pallas_call(
        _s5_sample_body,
        grid=(S5_BLOCKS,),
        in_specs=[
            pl.BlockSpec((N_SAMPLE, 128), lambda j: (r0, j)),
            spblk(lb), spblk(cl), sblk(lbar1), sblk(d0),
            pl.BlockSpec((N_SAMPLE, ns), lambda j: (0, j)),
            pl.BlockSpec((N_SAMPLE, ns), lambda j: (0, j)),
        ],
        out_specs=[
            pl.BlockSpec((N_SAMPLE, 128), lambda j: (0, j)),
            pl.BlockSpec((N_SAMPLE, ns), lambda j: (0, j)),
            pl.BlockSpec((N_SAMPLE, ns), lambda j: (0, j)),
        ],
        out_shape=[
            jax.ShapeDtypeStruct((N_SAMPLE, BRANCH), F32),
            jax.ShapeDtypeStruct((N_SAMPLE, S5_GROUPS * S5_STATE), F32),
            jax.ShapeDtypeStruct((N_SAMPLE, S5_GROUPS * S5_STATE), F32),
        ],
        compiler_params=_cp(("arbitrary",)),
        name="s5_sample",
    )(u, lb, cl, lbar1, d0, st_re, st_im)
    return jnp.concatenate([y, y_s], axis=0), f_re, f_im, s_re, s_im


def _merge_body(o_ref, z_ref, w0_ref, w1_ref, g0_ref, g1_ref, out_ref):
    p0 = jnp.dot(o_ref[...], w0_ref[0].astype(BF16), preferred_element_type=F32)
    p1 = jnp.dot(z_ref[...], w1_ref[0].astype(BF16), preferred_element_type=F32)
    out_ref[...] = (g0_ref[...] * p0 + g1_ref[...] * p1).astype(out_ref.dtype)


def _merge(o, z, w_branch, gates):
    ncb = D_MODEL // TN
    return pl.pallas_call(
        _merge_body,
        grid=(N_TOK // TM, ncb),
        in_specs=[
            pl.BlockSpec((TM, BRANCH), lambda i, j: (i, 0)),
            pl.BlockSpec((TM, BRANCH), lambda i, j: (i, 0)),
            pl.BlockSpec((1, BRANCH, TN), lambda i, j: (0, 0, j)),
            pl.BlockSpec((1, BRANCH, TN), lambda i, j: (1, 0, j)),
            pl.BlockSpec((TM, TN), lambda i, j: (i, j)),
            pl.BlockSpec((TM, TN), lambda i, j: (i, ncb + j)),
        ],
        out_specs=pl.BlockSpec((TM, TN), lambda i, j: (i, j)),
        out_shape=jax.ShapeDtypeStruct((N_TOK, D_MODEL), BF16),
        compiler_params=_cp(("arbitrary", "arbitrary")),
        name="merge",
    )(o, z, w_branch, w_branch, gates, gates)


LN_TM = 128
LN_PROMPT_TILES = N_PROMPT // LN_TM
HALF = D_MODEL // 2


def _pack_bf16_pair(x):
    bits = pltpu.bitcast(x.astype(BF16).astype(F32), jnp.uint32)
    return (bits[:, :HALF] & jnp.uint32(0xFFFF0000)) | (bits[:, HALF:] >> 16)


SLAB = HALF // 128


def _slab_store(ref, row0, x):
    for c in range(SLAB):
        ref[pl.ds(row0 * SLAB + c, x.shape[0], stride=SLAB), :] = x[:, c * 128:(c + 1) * 128]


def _slab_load(ref, row0, n):
    return jnp.concatenate([ref[pl.ds(row0 * SLAB + c, n, stride=SLAB), :] for c in range(SLAB)], axis=1)


def _slab_rows(first, n):
    return pl.ds(pl.multiple_of(first * SLAB, SLAB), n * SLAB)


def _unpack_bf16_pair(w):
    hi = pltpu.bitcast(w & jnp.uint32(0xFFFF0000), F32)
    lo = pltpu.bitcast(w << 16, F32)
    return hi, lo


def _ln_router_body(m_ref, xp_ref, xs_ref, g_ref, b_ref, wr_ref, br_ref, o_ref, op_ref, ids_ref, wts_ref):
    x = jnp.where(pl.program_id(0) < LN_PROMPT_TILES, xp_ref[...], xs_ref[...])
    x1 = _layer_norm(ALPHA * x + m_ref[...]) * g_ref[...] + b_ref[...]
    o_ref[...] = x1
    _slab_store(op_ref, 0, _pack_bf16_pair(x1))
    x_hi = x1.astype(BF16)
    x_mid = (x1 - x_hi.astype(F32)).astype(BF16)
    wr = wr_ref[...]
    both = jnp.dot(x_hi, wr, preferred_element_type=F32)
    logits = (both[:, :128] + both[:, 128:]
              + jnp.dot(x_mid, wr[:, :128], preferred_element_type=F32) + br_ref[...])
    _route(logits, ids_ref, wts_ref)


def _ln_router(mixed, x_p, x_s, g, b, w_r, b_r):
    w_hi = w_r.astype(BF16)
    w_mid = (w_r - w_hi.astype(F32)).astype(BF16)
    wr = jnp.concatenate([w_hi, w_mid], axis=1)
    row_blk = pl.BlockSpec((LN_TM, 128), lambda i: (i, 0))
    return pl.pallas_call(
        _ln_router_body,
        grid=(N_TOK // LN_TM,),
        in_specs=[
            pl.BlockSpec((LN_TM, D_MODEL), lambda i: (i, 0)),
            pl.BlockSpec((LN_TM, D_MODEL), lambda i: (jnp.minimum(i, LN_PROMPT_TILES - 1), 0)),
            pl.BlockSpec((LN_TM, D_MODEL), lambda i: (jnp.maximum(i - LN_PROMPT_TILES, 0), 0)),
            pl.BlockSpec((1, D_MODEL), lambda i: (0, 0)),
            pl.BlockSpec((1, D_MODEL), lambda i: (0, 0)),
            pl.BlockSpec((D_MODEL, 256), lambda i: (0, 0)),
            pl.BlockSpec((1, 128), lambda i: (0, 0)),
        ],
        out_specs=[pl.BlockSpec((LN_TM, D_MODEL), lambda i: (i, 0)),
                   pl.BlockSpec((LN_TM * SLAB, 128), lambda i: (i, 0)), row_blk, row_blk],
        out_shape=[jax.ShapeDtypeStruct((N_TOK, D_MODEL), F32),
                   jax.ShapeDtypeStruct((N_TOK * SLAB, 128), jnp.uint32),
                   jax.ShapeDtypeStruct((N_TOK, 128), jnp.int32),
                   jax.ShapeDtypeStruct((N_TOK, 128), F32)],
        compiler_params=_cp(("arbitrary",)),
        name="ln1_router",
    )(mixed, x_p, x_s, g, b, wr, b_r)


def _route(logits, ids_ref, wts_ref):
    lane = lax.broadcasted_iota(jnp.int32, logits.shape, 1)
    neg = -jnp.inf
    big = 1 << 20
    gl = jnp.where(lane < 8, logits, neg)
    gmax = jnp.max(gl, axis=1, keepdims=True)
    gidx = jnp.min(jnp.where(gl == gmax, lane, big), axis=1, keepdims=True)
    gprob = 1.0 / jnp.sum(jnp.exp(gl - gmax), axis=1, keepdims=True)
    in_group = (lane >= 8) & (lane < 8 + N_EXPERTS) & (((lane - 8) >> 3) == gidx)
    el = jnp.where(in_group, logits, neg)
    v1 = jnp.max(el, axis=1, keepdims=True)
    i1 = jnp.min(jnp.where(el == v1, lane, big), axis=1, keepdims=True)
    el2 = jnp.where(lane == i1, neg, el)
    v2 = jnp.max(el2, axis=1, keepdims=True)
    i2 = jnp.min(jnp.where(el2 == v2, lane, big), axis=1, keepdims=True)
    e2 = jnp.exp(v2 - v1)
    w1 = gprob / (1.0 + e2)
    w2 = gprob * e2 / (1.0 + e2)
    ids_ref[...] = jnp.where(lane == 0, i1 - 8, jnp.where(lane == 1, i2 - 8, 0))
    wts_ref[...] = jnp.where(lane == 0, w1, jnp.where(lane == 1, w2, 0.0))


N_ASSIGN = 2 * N_TOK
MOE_NBLOCKS = (N_ASSIGN + N_EXPERTS * (MOE_BLOCK - 1) + MOE_BLOCK - 1) // MOE_BLOCK
MOE_ROWS = MOE_NBLOCKS * MOE_BLOCK
MOE_HC = MOE_HIDDEN // MOE_HSPLIT


def _row_copy(src_hbm, row, dst, dst_row, sem):
    return pltpu.make_async_copy(src_hbm.at[_slab_rows(row, 1), :], dst.at[_slab_rows(dst_row, 1), :], sem)


MOE_MACRO = 4
MOE_NMACRO = N_ASSIGN // (MOE_MACRO * MOE_BLOCK) + N_EXPERTS + 1
MOE_DCOLS = 1024


def _block_copy(src, t, o_hbm, blk, sem):
    return pltpu.make_async_copy(src.at[_slab_rows(t * MOE_BLOCK, MOE_BLOCK), :],
                                 o_hbm.at[_slab_rows(blk * MOE_BLOCK, MOE_BLOCK), :], sem)


def _moe_body(mexp_ref, mstart_ref, mnsub_ref, tok_ref, x_hbm, wg_ref, wu_ref, wd_ref, o_hbm,
              stage, xb, acc, ostage, gsem, osem):
    del mexp_ref
    m = pl.program_id(0)
    h = pl.program_id(1)
    nsub = mnsub_ref[m]
    start = mstart_ref[m]
    last_m = pl.num_programs(0) - 1
    last_h = pl.num_programs(1) - 1

    def wait_blocks(n):
        def done(t, c):
            _block_copy(ostage, t, o_hbm, 0, osem).wait()
            return c

        lax.fori_loop(0, n, done, 0)

    def gather_start(step):
        def sub(t, c):
            base = (mstart_ref[step] + t) * MOE_BLOCK
            off = pl.multiple_of(t * MOE_BLOCK, MOE_BLOCK)
            for r in range(MOE_BLOCK):
                _row_copy(x_hbm, tok_ref[base + r], stage, off + r, gsem).start()
            return c

        lax.fori_loop(0, mnsub_ref[step], sub, 0)

    @pl.when((h == 0) & (m == 0))
    def _():
        gather_start(0)

    @pl.when((h == 0) & (nsub > 0))
    def _():
        def landed(t, c):
            pltpu.make_async_copy(x_hbm.at[_slab_rows(0, MOE_BLOCK), :],
                                  stage.at[_slab_rows(t * MOE_BLOCK, MOE_BLOCK), :], gsem).wait()
            return c

        lax.fori_loop(0, nsub, landed, 0)

        def sub(t, c):
            rows = pl.ds(pl.multiple_of(t * MOE_BLOCK, MOE_BLOCK), MOE_BLOCK)
            hi, lo = _unpack_bf16_pair(_slab_load(stage, t * MOE_BLOCK, MOE_BLOCK))
            xb[rows, 0:HALF] = hi.astype(BF16)
            xb[rows, HALF:D_MODEL] = lo.astype(BF16)
            return c

        lax.fori_loop(0, nsub, sub, 0)

    @pl.when((h == last_h) & (m < last_m))
    def _():
        gather_start(jnp.minimum(m + 1, last_m))

    for ns in range(1, MOE_MACRO + 1):
        @pl.when(nsub == ns)
        def _(ns=ns):
            rows = ns * MOE_BLOCK

            @pl.when(h == 0)
            def _():
                acc[0:rows, :] = jnp.zeros((rows, D_MODEL), F32)

            x = xb[0:rows, :]
            hg = jnp.dot(x, wg_ref[0].astype(BF16), preferred_element_type=F32)
            hu = jnp.dot(x, wu_ref[0].astype(BF16), preferred_element_type=F32)
            hh = (hg * _sigmoid(hg) * hu).astype(BF16)
            wd = wd_ref[0].astype(BF16)
            for cc in range(D_MODEL // MOE_DCOLS):
                cols = slice(cc * MOE_DCOLS, (cc + 1) * MOE_DCOLS)
                acc[0:rows, cols] += jnp.dot(hh, wd[:, cols], preferred_element_type=F32)

    @pl.when((h == last_h) & (m > 0))
    def _():
        wait_blocks(mnsub_ref[jnp.maximum(m - 1, 0)])

    @pl.when((h == last_h) & (nsub > 0))
    def _():
        def put(t, c):
            rows = pl.ds(pl.multiple_of(t * MOE_BLOCK, MOE_BLOCK), MOE_BLOCK)
            _slab_store(ostage, t * MOE_BLOCK, _pack_bf16_pair(acc[rows, :]))
            _block_copy(ostage, t, o_hbm, start + t, osem).start()
            return c

        lax.fori_loop(0, nsub, put, 0)

        @pl.when(m == last_m)
        def _():
            wait_blocks(nsub)

    @pl.when((h == last_h) & (nsub == 0))
    def _():
        ostage[0:MOE_BLOCK * SLAB, :] = jnp.zeros((MOE_BLOCK * SLAB, 128), jnp.uint32)
        for t in range(MOE_MACRO):
            @pl.when(start + t < MOE_NBLOCKS)
            def _(t=t):
                cp = _block_copy(ostage, 0, o_hbm, start + t, osem)
                cp.start()
                cp.wait()


def _moe_experts(x1p, wg, wu, wd, mexp, mstart, mnsub, row_tok):
    grid_spec = pltpu.PrefetchScalarGridSpec(
        num_scalar_prefetch=4,
        grid=(MOE_NMACRO, MOE_HSPLIT),
        in_specs=[
            pl.BlockSpec(memory_space=pl.ANY),
            pl.BlockSpec((1, D_MODEL, MOE_HC), lambda m, h, me, ms, mn, rt: (me[m], 0, h)),
            pl.BlockSpec((1, D_MODEL, MOE_HC), lambda m, h, me, ms, mn, rt: (me[m], 0, h)),
            pl.BlockSpec((1, MOE_HC, D_MODEL), lambda m, h, me, ms, mn, rt: (me[m], h, 0)),
        ],
        out_specs=pl.BlockSpec(memory_space=pl.ANY),
        scratch_shapes=[pltpu.VMEM((MOE_MACRO * MOE_BLOCK * SLAB, 128), jnp.uint32),
                        pltpu.VMEM((MOE_MACRO * MOE_BLOCK, D_MODEL), BF16),
                        pltpu.VMEM((MOE_MACRO * MOE_BLOCK, D_MODEL), F32),
                        pltpu.VMEM((MOE_MACRO * MOE_BLOCK * SLAB, 128), jnp.uint32),
                        pltpu.SemaphoreType.DMA(()), pltpu.SemaphoreType.DMA(())],
    )
    return pl.pallas_call(
        _moe_body,
        grid_spec=grid_spec,
        out_shape=jax.ShapeDtypeStruct((MOE_ROWS * SLAB, 128), jnp.uint32),
        compiler_params=_cp(("arbitrary", "arbitrary")),
        name="moe_experts",
    )(mexp, mstart, mnsub, row_tok, x1p, wg, wu, wd)


CMB_TM = 128
CMB_PROMPT_TILES = N_PROMPT // CMB_TM


def _combine_body(pos_ref, eo_hbm, wts_ref, x1_ref, g_ref, b_ref, yp_ref, ys_ref, buf, sem):
    i = pl.program_id(0)
    slot = i % 2

    def fetch(tile, s):
        base = 2 * tile * CMB_TM
        for r in range(CMB_TM):
            _row_copy(eo_hbm, pos_ref[base + 2 * r], buf.at[s, 0], r, sem.at[s]).start()
            _row_copy(eo_hbm, pos_ref[base + 2 * r + 1], buf.at[s, 1], r, sem.at[s]).start()

    @pl.when(i == 0)
    def _():
        fetch(0, 0)

    for s in range(2):
        @pl.when((i + 1 < pl.num_programs(0)) & (slot == s))
        def _(s=s):
            fetch(i + 1, 1 - s)

    for k in range(2):
        pltpu.make_async_copy(eo_hbm.at[_slab_rows(0, CMB_TM), :], buf.at[slot, k], sem.at[slot]).wait()
    w = wts_ref[...]
    hi0, lo0 = _unpack_bf16_pair(_slab_load(buf.at[slot, 0], 0, CMB_TM))
    hi1, lo1 = _unpack_bf16_pair(_slab_load(buf.at[slot, 1], 0, CMB_TM))
    w0, w1 = w[:, 0:1], w[:, 1:2]
    y = jnp.concatenate([w0 * hi0 + w1 * hi1, w0 * lo0 + w1 * lo1], axis=1)
    x2 = _layer_norm(ALPHA * x1_ref[...] + y) * g_ref[...] + b_ref[...]

    @pl.when(i < CMB_PROMPT_TILES)
    def _():
        yp_ref[...] = x2

    @pl.when(i >= CMB_PROMPT_TILES)
    def _():
        ys_ref[...] = x2


def _combine(pos, eo, wts, x1, g, b):
    grid_spec = pltpu.PrefetchScalarGridSpec(
        num_scalar_prefetch=1,
        grid=(N_TOK // CMB_TM,),
        in_specs=[
            pl.BlockSpec(memory_space=pl.ANY),
            pl.BlockSpec((CMB_TM, 128), lambda i, p: (i, 0)),
            pl.BlockSpec((CMB_TM, D_MODEL), lambda i, p: (i, 0)),
            pl.BlockSpec((1, D_MODEL), lambda i, p: (0, 0)),
            pl.BlockSpec((1, D_MODEL), lambda i, p: (0, 0)),
        ],
        out_specs=[
            pl.BlockSpec((CMB_TM, D_MODEL), lambda i, p: (jnp.minimum(i, CMB_PROMPT_TILES - 1), 0)),
            pl.BlockSpec((CMB_TM, D_MODEL), lambda i, p: (0, 0)),
        ],
        scratch_shapes=[pltpu.VMEM((2, 2, CMB_TM * SLAB, 128), jnp.uint32), pltpu.SemaphoreType.DMA((2,))],
    )
    return pl.pallas_call(
        _combine_body,
        grid_spec=grid_spec,
        out_shape=[jax.ShapeDtypeStruct((N_PROMPT, D_MODEL), F32),
                   jax.ShapeDtypeStruct((N_SAMPLE, D_MODEL), F32)],
        compiler_params=_cp(("arbitrary",)),
        name="combine_ln2",
    )(pos, eo, wts, x1, g, b)


def _route_positions(ids):
    eid = ids[:, :2].reshape(-1)
    onehot = (eid[:, None] == jnp.arange(N_EXPERTS, dtype=jnp.int32)[None, :]).astype(jnp.int32)
    csum = jnp.cumsum(onehot, axis=0)
    rank = jnp.take_along_axis(csum, eid[:, None], axis=1)[:, 0] - 1
    counts = csum[-1]
    nblk_e = (counts + MOE_BLOCK - 1) // MOE_BLOCK
    bend = jnp.cumsum(nblk_e)
    bstart = bend - nblk_e
    pos = bstart[eid] * MOE_BLOCK + rank
    row_tok = jnp.zeros((MOE_ROWS,), jnp.int32).at[pos].set(jnp.arange(N_ASSIGN, dtype=jnp.int32) // 2)
    nstep_e = (nblk_e + MOE_MACRO - 1) // MOE_MACRO
    send = jnp.cumsum(nstep_e)
    n_steps, n_blocks = send[-1], bend[-1]
    m = jnp.arange(MOE_NMACRO, dtype=jnp.int32)
    e_of_m = jnp.minimum(jnp.searchsorted(send, m, side='right'), N_EXPERTS - 1).astype(jnp.int32)
    local = m - (send - nstep_e)[e_of_m]
    valid = m < n_steps
    mexp = jnp.where(valid, e_of_m, e_of_m[jnp.maximum(n_steps - 1, 0)])
    mstart = jnp.where(valid, bstart[e_of_m] + MOE_MACRO * local, n_blocks + MOE_MACRO * (m - n_steps))
    mnsub = jnp.where(valid, jnp.clip(nblk_e[e_of_m] - MOE_MACRO * local, 0, MOE_MACRO), 0)
    return (pos.astype(jnp.int32), row_tok, mexp.astype(jnp.int32), mstart.astype(jnp.int32),
            mnsub.astype(jnp.int32))


def kernel(x_prompt, x_sample, state_gla, state_s5_re, state_s5_im, w_in, w_gla_gate_up, b_gla_gate_up, w_gla_norm, s5_a_re, s5_a_im, s5_b_re, s5_b_im, s5_c_re, s5_c_im, s5_d, s5_log_dt, w_s5_glu, b_s5_glu, w_branch, w_out, ln1_g, ln1_b, w_router_group, b_router_group, w_router_expert, b_router_expert, w_moe_gate, w_moe_up, w_moe_down, ln2_g, ln2_b):
    x_p = x_prompt.reshape(N_PROMPT, D_MODEL)
    x_s = x_sample.reshape(N_SAMPLE, D_MODEL)
    x_bf = jnp.concatenate([x_p.astype(BF16), x_s.astype(BF16)], axis=0)

    w_in_t = w_in.T
    qkvr = _mm_t(x_bf, w_in_t, row0=0, ncols=COL_A, tn=TN, out_dtype=F32, name="proj_qkvr")
    a_low = _mm_t(x_bf, w_in_t, row0=COL_A, ncols=128, tn=128, out_dtype=F32, name="proj_a")
    u = _mm_t(x_bf, w_in_t, row0=COL_U, ncols=BRANCH, tn=TN, out_dtype=F32, name="proj_u")
    gates = _mm_t(x_bf, w_in_t, row0=COL_G, ncols=2 * D_MODEL, tn=TN, out_dtype=F32, name="proj_gates",
                  epilogue=_sigmoid)

    wgu = jnp.pad(w_gla_gate_up, ((0, 128 - GLA_RANK), (0, 0)))
    bgu = b_gla_gate_up.reshape(1, GLA_DK)
    wn = w_gla_norm.reshape(1, BRANCH)
    o_p, gla_p_t = _gla_prompt(qkvr, a_low, wgu, bgu, wn)
    o_s, gla_s = _gla_sample(qkvr, a_low, wgu, bgu, wn, state_gla)
    o_all = jnp.concatenate([o_p, o_s], axis=0)
    gla_p = jnp.swapaxes(gla_p_t, 2, 3)

    s5w = _s5_weights(s5_a_re, s5_a_im, s5_b_re, s5_b_im, s5_c_re, s5_c_im, s5_d, s5_log_dt)
    y, s5_re_p, s5_im_p, s5_re_s, s5_im_s = _s5(
        u, s5w, state_s5_re.reshape(N_SAMPLE, -1), state_s5_im.reshape(N_SAMPLE, -1))
    z = _mm(y, w_s5_glu, col0=0, ncols=BRANCH, tn=TN, out_dtype=BF16, name="s5_glu",
            epilogue=lambda acc, yt, bt: yt * _sigmoid(acc + bt),
            extra=(y, b_s5_glu.reshape(1, BRANCH)),
            extra_specs=(pl.BlockSpec((TM, TN), lambda i, j: (i, j)), pl.BlockSpec((1, TN), lambda i, j: (0, j))))

    pre = _merge(o_all, z, w_branch, gates)
    w_r = jnp.concatenate([w_router_group,
                           jnp.moveaxis(w_router_expert, 0, 1).reshape(D_MODEL, N_EXPERTS),
                           jnp.zeros((D_MODEL, 128 - 8 - N_EXPERTS), F32)], axis=1)
    b_r = jnp.concatenate([b_router_group, b_router_expert.reshape(-1),
                           jnp.zeros((128 - 8 - N_EXPERTS,), F32)]).reshape(1, 128)
    mixed = _mm(pre, w_out, col0=0, ncols=D_MODEL, tn=TN, out_dtype=F32, name="out_proj")
    x1, x1p, ids, wts = _ln_router(mixed, x_p, x_s, ln1_g.reshape(1, D_MODEL), ln1_b.reshape(1, D_MODEL), w_r, b_r)

    pos, row_tok, mexp, mstart, mnsub = _route_positions(ids)
    eo = _moe_experts(x1p,
                      w_moe_gate.reshape(N_EXPERTS, D_MODEL, MOE_HIDDEN),
                      w_moe_up.reshape(N_EXPERTS, D_MODEL, MOE_HIDDEN),
                      w_moe_down.reshape(N_EXPERTS, MOE_HIDDEN, D_MODEL),
                      mexp, mstart, mnsub, row_tok)
    y_p, y_s = _combine(pos, eo, wts, x1, ln2_g.reshape(1, D_MODEL), ln2_b.reshape(1, D_MODEL))

    return (y_p.reshape(N_PROMPT_SEQ, SEQ, D_MODEL), y_s.reshape(N_SAMPLE, 1, D_MODEL),
            gla_p,
            s5_re_p.reshape(N_PROMPT_SEQ, S5_GROUPS, S5_STATE), s5_im_p.reshape(N_PROMPT_SEQ, S5_GROUPS, S5_STATE),
            gla_s,
            s5_re_s.reshape(N_SAMPLE, S5_GROUPS, S5_STATE), s5_im_s.reshape(N_SAMPLE, S5_GROUPS, S5_STATE))
```

```python
import functools
import math

import jax
import jax.numpy as jnp
import numpy as np
from jax import lax
from jax.experimental import pallas as pl
from jax.experimental.pallas import tpu as pltpu

F32 = jnp.float32
BF16 = jnp.bfloat16
HIGHEST = lax.Precision.HIGHEST

D_MODEL = 4096
N_PROMPT_SEQ = 4
SEQ = 2048
N_SAMPLE = 128
N_PROMPT = N_PROMPT_SEQ * SEQ
N_TOK = N_PROMPT + N_SAMPLE
BRANCH = D_MODEL // 2
GLA_HEADS = 4
GLA_DK = D_MODEL // 4
GLA_HDK = GLA_DK // GLA_HEADS
GLA_HDV = BRANCH // GLA_HEADS
GLA_RANK = 16
GLA_TAU = 16.0
GLA_CHUNK = 64
GLA_LEVELS = 6
S5_GROUP = 16
S5_GROUPS = BRANCH // S5_GROUP
S5_STATE = 64
S5_CHUNK = 8
S5_LANE_GROUPS = 8
S5_BLOCKS = S5_GROUPS // S5_LANE_GROUPS
N_EXPERTS = 64
MOE_HIDDEN = D_MODEL // 8
MOE_BLOCK = 128
MOE_HSPLIT = 2
LN_EPS = 1e-5
ALPHA = 2.0 ** 0.25
COL_A = 6144
COL_U = 6160
COL_G = 8208
IN_COLS = 16400

VMEM_LIMIT = 56 * 1024 * 1024
TM = 1040
TN = 512


_NT = (((1,), (1,)), ((), ()))
_TN = (((0,), (0,)), ((), ()))


def _cp(sem):
    return pltpu.CompilerParams(dimension_semantics=sem, vmem_limit_bytes=VMEM_LIMIT)


def _sigmoid(x):
    return 1.0 / (1.0 + jnp.exp(-x))


def _log_sigmoid(z):
    return jnp.minimum(z, 0.0) - jnp.log1p(jnp.exp(-jnp.abs(z)))


def _gelu_tanh(x):
    return 0.5 * x * (1.0 + jnp.tanh(math.sqrt(2.0 / math.pi) * (x + 0.044715 * (x * x * x))))


def _layer_norm(x):
    mu = jnp.mean(x, axis=-1, keepdims=True)
    xc = x - mu
    var = jnp.mean(xc * xc, axis=-1, keepdims=True)
    return xc * lax.rsqrt(var + LN_EPS)


def _mm_body(a_ref, w_ref, *rest, epilogue):
    *extra, o_ref = rest
    acc = jnp.dot(a_ref[...].astype(BF16), w_ref[...].astype(BF16), preferred_element_type=F32)
    o_ref[...] = epilogue(acc, *[e[...] for e in extra]).astype(o_ref.dtype)


def _mm(a, w, *, col0, ncols, tn, out_dtype, name, epilogue=lambda acc: acc, extra=(), extra_specs=()):
    m, k = a.shape
    cb0 = col0 // tn
    return pl.pallas_call(
        functools.partial(_mm_body, epilogue=epilogue),
        grid=(m // TM, ncols // tn),
        in_specs=[pl.BlockSpec((TM, k), lambda i, j: (i, 0)),
                  pl.BlockSpec((k, tn), lambda i, j: (0, cb0 + j)),
                  *extra_specs],
        out_specs=pl.BlockSpec((TM, tn), lambda i, j: (i, j)),
        out_shape=jax.ShapeDtypeStruct((m, ncols), out_dtype),
        compiler_params=_cp(("arbitrary", "arbitrary")),
        name=name,
    )(a, w, *extra)


def _mm_t_body(a_ref, wt_ref, o_ref, *, epilogue):
    acc = lax.dot_general(a_ref[...], wt_ref[...].astype(BF16), _NT, preferred_element_type=F32)
    o_ref[...] = epilogue(acc).astype(o_ref.dtype)


def _mm_t(a, wt, *, row0, ncols, tn, out_dtype, name, epilogue=lambda acc: acc):
    m, k = a.shape
    return pl.pallas_call(
        functools.partial(_mm_t_body, epilogue=epilogue),
        grid=(m // TM, ncols // tn),
        in_specs=[pl.BlockSpec((TM, k), lambda i, j: (i, 0)),
                  pl.BlockSpec((pl.Element(tn), pl.Element(k)),
                               lambda i, j: (pl.multiple_of(row0 + j * tn, 8), 0))],
        out_specs=pl.BlockSpec((TM, tn), lambda i, j: (i, j)),
        out_shape=jax.ShapeDtypeStruct((m, ncols), out_dtype),
        compiler_params=_cp(("arbitrary", "arbitrary")),
        name=name,
    )(a, wt)


def _gla_coeff_matrix():
    c = GLA_CHUNK
    t = np.arange(c)[:, None]
    u = np.arange(c)[None, :]
    blocks = [(u <= t), (u > t)]
    for lvl in range(1, GLA_LEVELS + 1):
        m = 1 << lvl
        half = m // 2
        mid = (t // m) * m + half - 1
        lower = (t % m) >= half
        blocks.append(np.where(lower, (u > mid) & (u <= t), (u > t) & (u <= mid)))
    return np.concatenate(blocks, axis=0).astype(np.float32)


def _gla_level_masks():
    c = GLA_CHUNK
    t = lax.broadcasted_iota(jnp.int32, (c, c), 0)
    s = lax.broadcasted_iota(jnp.int32, (c, c), 1)
    masks = []
    for lvl in range(1, GLA_LEVELS + 1):
        m = 1 << lvl
        half = m // 2
        masks.append(((t >> lvl) == (s >> lvl)) & ((t & (m - 1)) >= half) & ((s & (m - 1)) < half))
    return masks


def _gla_out_norm(o, r, wn):
    return _layer_norm(o) * wn * (r * _sigmoid(r))


def _gla_prompt_body(q_ref, k_ref, v_ref, r_ref, a_ref, wgu_ref, bgu_ref, wn_ref, cm_ref,
                     o_ref, st_ref, s_scr, *, n_sub):
    c = GLA_CHUNK

    @pl.when(pl.program_id(2) == 0)
    def _():
        s_scr[...] = jnp.zeros_like(s_scr)

    masks = _gla_level_masks()
    cm = cm_ref[...]
    nt, tn = _NT, _TN

    def chunk(ci, carry):
        for hh in range(GLA_HP):
            head_chunk(ci, hh)
        return carry

    def head_chunk(ci, hh):
        rows = pl.ds(pl.multiple_of(ci * c, c), c)
        kcols = slice(hh * GLA_HDK, (hh + 1) * GLA_HDK)
        vcols = slice(hh * GLA_HDV, (hh + 1) * GLA_HDV)
        q = q_ref[rows, kcols] * (GLA_HDK ** -0.5)
        k = k_ref[rows, kcols]
        v = v_ref[rows, vcols]
        z = jnp.dot(a_ref[rows, :], wgu_ref[:, kcols], preferred_element_type=F32, precision=HIGHEST) + bgu_ref[:, kcols]
        g = _log_sigmoid(z) * (1.0 / GLA_TAU)
        g_hi = g.astype(BF16)
        r1 = g - g_hi.astype(F32)
        g_mid = r1.astype(BF16)
        g_lo = (r1 - g_mid.astype(F32)).astype(BF16)
        g3 = jnp.concatenate([g_hi, g_mid, g_lo], axis=0)
        f = jnp.exp(jnp.dot(cm, g3, preferred_element_type=F32))
        st = s_scr[hh]
        vb = v.astype(BF16)
        o = lax.dot_general((q * f[0:c]).astype(BF16), st.astype(BF16), nt, preferred_element_type=F32)
        scores = jnp.zeros((c, c), F32)
        for lvl in range(GLA_LEVELS):
            fl = f[(2 + lvl) * c:(3 + lvl) * c]
            p = lax.dot_general((q * fl).astype(BF16), (k * fl).astype(BF16), nt, preferred_element_type=F32)
            scores = scores + jnp.where(masks[lvl], p, 0.0)
        diag = jnp.sum(q * k, axis=1, keepdims=True)
        o = o + jnp.dot(scores.astype(BF16), vb, preferred_element_type=F32) + diag * v
        kd = (k * f[c:2 * c]).astype(BF16)
        s_scr[hh] = st * f[c - 1:c, :] + lax.dot_general(vb, kd, tn, preferred_element_type=F32)
        o_ref[rows, vcols] = _gla_out_norm(o, r_ref[rows, vcols], wn_ref[:, vcols]).astype(o_ref.dtype)

    lax.fori_loop(0, n_sub, chunk, 0, unroll=True)

    @pl.when(pl.program_id(2) == pl.num_programs(2) - 1)
    def _():
        st_ref[0] = s_scr[...]


GLA_HP = 4


def _gla_prompt(qkvr, a_low, wgu, bgu, wn):
    n_sub = 4
    tt = GLA_CHUNK * n_sub
    nt_steps = SEQ // tt
    cm = jnp.asarray(np.tile(_gla_coeff_matrix(), (1, 3)), dtype=BF16)
    rows = lambda b, h, c: b * nt_steps + c
    wk, wv = GLA_HP * GLA_HDK, GLA_HP * GLA_HDV
    n_groups = GLA_HEADS // GLA_HP
    o, st = pl.pallas_call(
        functools.partial(_gla_prompt_body, n_sub=n_sub),
        grid=(N_PROMPT_SEQ, n_groups, nt_steps),
        in_specs=[
            pl.BlockSpec((tt, wk), lambda b, h, c: (rows(b, h, c), h)),
            pl.BlockSpec((tt, wk), lambda b, h, c: (rows(b, h, c), n_groups + h)),
            pl.BlockSpec((tt, wv), lambda b, h, c: (rows(b, h, c), n_groups + h)),
            pl.BlockSpec((tt, wv), lambda b, h, c: (rows(b, h, c), 2 * n_groups + h)),
            pl.BlockSpec((tt, 128), lambda b, h, c: (rows(b, h, c), 0)),
            pl.BlockSpec((128, wk), lambda b, h, c: (0, h)),
            pl.BlockSpec((1, wk), lambda b, h, c: (0, h)),
            pl.BlockSpec((1, wv), lambda b, h, c: (0, h)),
            pl.BlockSpec(cm.shape, lambda b, h, c: (0, 0)),
        ],
        out_specs=[
            pl.BlockSpec((tt, wv), lambda b, h, c: (rows(b, h, c), h)),
            pl.BlockSpec((1, GLA_HP, GLA_HDV, GLA_HDK), lambda b, h, c: (b, h, 0, 0)),
        ],
        out_shape=[
            jax.ShapeDtypeStruct((N_PROMPT, BRANCH), BF16),
            jax.ShapeDtypeStruct((N_PROMPT_SEQ, GLA_HEADS, GLA_HDV, GLA_HDK), F32),
        ],
        scratch_shapes=[pltpu.VMEM((GLA_HP, GLA_HDV, GLA_HDK), F32)],
        compiler_params=_cp(("arbitrary", "arbitrary", "arbitrary")),
        name="gla_prompt",
    )(qkvr, qkvr, qkvr, qkvr, a_low, wgu, bgu, wn, cm)
    return o, st


GLA_SB = 16


def _gla_sample_body(q_ref, k_ref, v_ref, r_ref, a_ref, wgu_ref, bgu_ref, wn_ref, s_ref, o_ref, so_ref):
    q = q_ref[...] * (GLA_HDK ** -0.5)
    k = k_ref[...]
    v = v_ref[...]
    z = jnp.dot(a_ref[...], wgu_ref[...], preferred_element_type=F32, precision=HIGHEST) + bgu_ref[...]
    eg = jnp.exp(_log_sigmoid(z) * (1.0 / GLA_TAU))
    qe = (q * eg).astype(BF16)
    eg_t = eg.T
    k_t = k.T
    rows = []
    for n in range(GLA_SB):
        s0 = s_ref[n, 0]
        rows.append(jnp.dot(qe, s0.astype(BF16), preferred_element_type=F32)[n:n + 1])
        so_ref[n, 0] = s0 * eg_t[:, n:n + 1] + k_t[:, n:n + 1] * v[n:n + 1, :]
    o = jnp.concatenate(rows, axis=0) + jnp.sum(q * k, axis=1, keepdims=True) * v
    o_ref[...] = _gla_out_norm(o, r_ref[...], wn_ref[...]).astype(o_ref.dtype)


def _gla_sample(qkvr, a_low, wgu, bgu, wn, state):
    r0 = N_PROMPT // GLA_SB
    o, st = pl.pallas_call(
        _gla_sample_body,
        grid=(GLA_HEADS, N_SAMPLE // GLA_SB),
        in_specs=[
            pl.BlockSpec((GLA_SB, GLA_HDK), lambda h, i: (r0 + i, h)),
            pl.BlockSpec((GLA_SB, GLA_HDK), lambda h, i: (r0 + i, GLA_HEADS + h)),
            pl.BlockSpec((GLA_SB, GLA_HDV), lambda h, i: (r0 + i, GLA_HEADS + h)),
            pl.BlockSpec((GLA_SB, GLA_HDV), lambda h, i: (r0 + i, 2 * GLA_HEADS + h)),
            pl.BlockSpec((GLA_SB, 128), lambda h, i: (r0 + i, 0)),
            pl.BlockSpec((128, GLA_HDK), lambda h, i: (0, h)),
            pl.BlockSpec((1, GLA_HDK), lambda h, i: (0, h)),
            pl.BlockSpec((1, GLA_HDV), lambda h, i: (0, h)),
            pl.BlockSpec((GLA_SB, 1, GLA_HDK, GLA_HDV), lambda h, i: (i, h, 0, 0)),
        ],
        out_specs=[
            pl.BlockSpec((GLA_SB, GLA_HDV), lambda h, i: (i, h)),
            pl.BlockSpec((GLA_SB, 1, GLA_HDK, GLA_HDV), lambda h, i: (i, h, 0, 0)),
        ],
        out_shape=[
            jax.ShapeDtypeStruct((N_SAMPLE, BRANCH), BF16),
            jax.ShapeDtypeStruct(state.shape, F32),
        ],
        compiler_params=_cp(("arbitrary", "arbitrary")),
        name="gla_sample",
    )(qkvr, qkvr, qkvr, qkvr, a_low, wgu, bgu, wn, state)
    return o, st


def _s5_weights(a_re, a_im, b_re, b_im, c_re, c_im, d, log_dt):
    L = S5_CHUNK
    lam_re = jnp.minimum(a_re, -1e-4)
    lam_im = a_im
    dt = jnp.exp(log_dt)[:, None]
    kk = jnp.arange(L + 1, dtype=F32)[:, None, None]
    pow_re = jnp.exp(lam_re * dt * kk) * jnp.cos(lam_im * dt * kk)
    pow_im = jnp.exp(lam_re * dt * kk) * jnp.sin(lam_im * dt * kk)
    lbar_re, lbar_im = pow_re[1], pow_im[1]
    den = lam_re * lam_re + lam_im * lam_im
    f_re = ((lbar_re - 1.0) * lam_re + lbar_im * lam_im) / den
    f_im = (lbar_im * lam_re - (lbar_re - 1.0) * lam_im) / den
    bb_re = f_re[..., None] * b_re - f_im[..., None] * b_im
    bb_im = f_re[..., None] * b_im + f_im[..., None] * b_re
    nb, ng = S5_BLOCKS, S5_LANE_GROUPS
    bb_re_t = jnp.swapaxes(bb_re, 1, 2)
    bb_im_t = jnp.swapaxes(bb_im, 1, 2)
    lb_re = (pow_re[:L, :, None, :] * bb_re_t - pow_im[:L, :, None, :] * bb_im_t).reshape(L, nb, 128, S5_STATE)
    lb_im = (pow_re[:L, :, None, :] * bb_im_t + pow_im[:L, :, None, :] * bb_re_t).reshape(L, nb, 128, S5_STATE)
    cl_re = (c_re[None] * pow_re[:, :, None, :] - c_im[None] * pow_im[:, :, None, :]).reshape(L + 1, nb, 128, S5_STATE)
    cl_im = (c_re[None] * pow_im[:, :, None, :] + c_im[None] * pow_re[:, :, None, :]).reshape(L + 1, nb, 128, S5_STATE)
    lb = jnp.concatenate([lb_re, lb_im], axis=-1)
    cl = jnp.concatenate([cl_re, -cl_im], axis=-1)
    n_steps = int(math.log2(SEQ // L))
    mult = (L * (2.0 ** jnp.arange(n_steps, dtype=F32)))[:, None, None]
    sc_re = jnp.exp(lam_re * dt * mult) * jnp.cos(lam_im * dt * mult)
    sc_im = jnp.exp(lam_re * dt * mult) * jnp.sin(lam_im * dt * mult)

    def state_lanes(x):
        lead = x.shape[:-2]
        return jnp.moveaxis(x.reshape(lead + (nb, ng * S5_STATE)), -2, 0)

    scan_mult = jnp.concatenate([state_lanes(sc_re), state_lanes(sc_im)], axis=-1)
    lbar1 = jnp.concatenate([state_lanes(lbar_re[None]), state_lanes(lbar_im[None])], axis=-1)
    dvec = jnp.tile(d.reshape(nb, 1, 128), (1, 1, L))
    return lb, cl, scan_mult, lbar1, dvec


def _s5_expand(src):
    ng, p = S5_LANE_GROUPS, S5_STATE
    lane = lax.broadcasted_iota(jnp.int32, src.shape, 1)
    other = pltpu.roll(src, p, 1)
    re2 = jnp.where(lane < p, src, other)
    im2 = jnp.where(lane < p, other, src)
    full = jnp.concatenate([re2] * (ng // 2) + [im2] * (ng // 2), axis=1)
    row = lax.broadcasted_iota(jnp.int32, full.shape, 0)
    col = lax.broadcasted_iota(jnp.int32, full.shape, 1)
    same_group = (row >> 4) == ((col >> 6) & (ng - 1))
    return jnp.where(same_group, full, 0.0)


def _s5_lag_kernel(zpow, cpow0):
    return lax.dot_general(zpow, cpow0, _NT, preferred_element_type=F32, precision=HIGHEST)


def _s5_build_weights(lb_ref, cl_ref, t_scr, wz_scr, wc_scr):
    L = S5_CHUNK
    cpow0 = _s5_expand(cl_ref[0, 0])
    t_scr[...] = jnp.zeros_like(t_scr)
    for k in range(L):
        zpow = _s5_expand(lb_ref[k, 0])
        wz_scr[(L - 1 - k) * 128:(L - k) * 128, :] = zpow.astype(BF16)
        wc_scr[k * 128:(k + 1) * 128, :] = _s5_expand(cl_ref[k + 1, 0]).astype(BF16)
        bd = _s5_lag_kernel(zpow, cpow0).astype(BF16)
        for sp in range(L - k):
            t_scr[sp * 128:(sp + 1) * 128, (sp + k) * 128:(sp + k + 1) * 128] = bd


def _s5_prompt_body(u_ref, lb_ref, cl_ref, sm_ref, d_ref, y_ref, fre_ref, fim_ref, t_ref, wz_ref, wc_ref,
                    *, n_steps):
    L = S5_CHUNK
    n_rows = SEQ // L
    ns = S5_LANE_GROUPS * S5_STATE

    @pl.when(pl.program_id(1) == 0)
    def _():
        _s5_build_weights(lb_ref, cl_ref, t_ref, wz_ref, wc_ref)

    v = jnp.concatenate([u_ref[pl.ds(s, n_rows, stride=L), :] for s in range(L)], axis=1)
    vb = v.astype(BF16)
    z = jnp.dot(vb, wz_ref[...], preferred_element_type=F32)
    hr, hi = z[:, :ns], z[:, ns:]
    pos = lax.broadcasted_iota(jnp.int32, (n_rows, ns), 0)
    sm = sm_ref[0]
    for d in range(n_steps):
        sh = 1 << d
        ar, ai = sm[d:d + 1, :ns], sm[d:d + 1, ns:]
        keep = pos >= sh
        pr = jnp.where(keep, pltpu.roll(hr, sh, 0), 0.0)
        pi = jnp.where(keep, pltpu.roll(hi, sh, 0), 0.0)
        hr, hi = hr + ar * pr - ai * pi, hi + ar * pi + ai * pr
    fre_ref[0] = hr[n_rows - 1:n_rows]
    fim_ref[0] = hi[n_rows - 1:n_rows]
    first = pos >= 1
    h_prev = jnp.concatenate([jnp.where(first, pltpu.roll(hr, 1, 0), 0.0),
                              jnp.where(first, pltpu.roll(hi, 1, 0), 0.0)], axis=1)
    y = (jnp.dot(vb, t_ref[...], preferred_element_type=F32)
         + lax.dot_general(h_prev.astype(BF16), wc_ref[...], _NT, preferred_element_type=F32)
         + d_ref[0] * v)
    y = _gelu_tanh(y)
    for s in range(L):
        y_ref[pl.ds(s, n_rows, stride=L), :] = y[:, s * 128:(s + 1) * 128]


def _s5_sample_body(u_ref, lb_ref, cl_ref, l1_ref, d_ref, hre_ref, him_ref, y_ref, sre_ref, sim_ref):
    ns = S5_LANE_GROUPS * S5_STATE
    bbar = _s5_expand(lb_ref[0, 0])
    k0 = _s5_lag_kernel(bbar, _s5_expand(cl_ref[0, 0])).astype(BF16)
    wc0 = _s5_expand(cl_ref[1, 0]).astype(BF16)
    us = u_ref[...]
    usb = us.astype(BF16)
    h0r, h0i = hre_ref[...], him_ref[...]
    l1 = l1_ref[0]
    bu = jnp.dot(usb, bbar.astype(BF16), preferred_element_type=F32)
    sre_ref[...] = l1[:, :ns] * h0r - l1[:, ns:] * h0i + bu[:, :ns]
    sim_ref[...] = l1[:, :ns] * h0i + l1[:, ns:] * h0r + bu[:, ns:]
    h0 = jnp.concatenate([h0r, h0i], axis=1).astype(BF16)
    ys = (jnp.dot(usb, k0, preferred_element_type=F32)
          + lax.dot_general(h0, wc0, _NT, preferred_element_type=F32)
          + d_ref[0] * us)
    y_ref[...] = _gelu_tanh(ys)


def _s5(u, weights, st_re, st_im):
    lb, cl, scan_mult, lbar1, dvec = weights
    L = S5_CHUNK
    ns = S5_LANE_GROUPS * S5_STATE
    n_steps = scan_mult.shape[1]
    wblk = lambda a: pl.BlockSpec((1,) + a.shape[1:], lambda j, b: (j, 0, 0))
    pblk = lambda a: pl.BlockSpec((a.shape[0], 1, 128, 128), lambda j, b: (0, j, 0, 0))
    mat = pltpu.VMEM((L * 128, L * 128), BF16)
    y, f_re, f_im = pl.pallas_call(
        functools.partial(_s5_prompt_body, n_steps=n_steps),
        grid=(S5_BLOCKS, N_PROMPT_SEQ),
        in_specs=[
            pl.BlockSpec((SEQ, 128), lambda j, b: (b, j)),
            pblk(lb), pblk(cl), wblk(scan_mult), wblk(dvec),
        ],
        out_specs=[
            pl.BlockSpec((SEQ, 128), lambda j, b: (b, j)),
            pl.BlockSpec((1, 1, ns), lambda j, b: (b, 0, j)),
            pl.BlockSpec((1, 1, ns), lambda j, b: (b, 0, j)),
        ],
        out_shape=[
            jax.ShapeDtypeStruct((N_PROMPT, BRANCH), F32),
            jax.ShapeDtypeStruct((N_PROMPT_SEQ, 1, S5_GROUPS * S5_STATE), F32),
            jax.ShapeDtypeStruct((N_PROMPT_SEQ, 1, S5_GROUPS * S5_STATE), F32),
        ],
        scratch_shapes=[mat, mat, mat],
        compiler_params=_cp(("arbitrary", "arbitrary")),
        name="s5_prompt",
    )(u, lb, cl, scan_mult, dvec)
    d0 = dvec[:, :, 0:128]
    sblk = lambda a: pl.BlockSpec((1,) + a.shape[1:], lambda j: (j, 0, 0))
    spblk = lambda a: pl.BlockSpec((2, 1, 128, 128), lambda j: (0, j, 0, 0))
    r0 = N_PROMPT // N_SAMPLE
    y_s, s_re, s_im = pl.pallas_call(
        _s5_sample_body,
        grid=(S5_BLOCKS,),
        in_specs=[
            pl.BlockSpec((N_SAMPLE, 128), lambda j: (r0, j)),
            spblk(lb), spblk(cl), sblk(lbar1), sblk(d0),
            pl.BlockSpec((N_SAMPLE, ns), lambda j: (0, j)),
            pl.BlockSpec((N_SAMPLE, ns), lambda j: (0, j)),
        ],
        out_specs=[
            pl.BlockSpec((N_SAMPLE, 128), lambda j: (0, j)),
            pl.BlockSpec((N_SAMPLE, ns), lambda j: (0, j)),
            pl.BlockSpec((N_SAMPLE, ns), lambda j: (0, j)),
        ],
        out_shape=[
            jax.ShapeDtypeStruct((N_SAMPLE, BRANCH), F32),
            jax.ShapeDtypeStruct((N_SAMPLE, S5_GROUPS * S5_STATE), F32),
            jax.ShapeDtypeStruct((N_SAMPLE, S5_GROUPS * S5_STATE), F32),
        ],
        compiler_params=_cp(("arbitrary",)),
        name="s5_sample",
    )(u, lb, cl, lbar1, d0, st_re, st_im)
    return jnp.concatenate([y, y_s], axis=0), f_re, f_im, s_re, s_im


def _merge_body(o_ref, z_ref, w0_ref, w1_ref, g0_ref, g1_ref, out_ref):
    p0 = jnp.dot(o_ref[...], w0_ref[0].astype(BF16), preferred_element_type=F32)
    p1 = jnp.dot(z_ref[...], w1_ref[0].astype(BF16), preferred_element_type=F32)
    out_ref[...] = (g0_ref[...] * p0 + g1_ref[...] * p1).astype(out_ref.dtype)


def _merge(o, z, w_branch, gates):
    ncb = D_MODEL // TN
    return pl.pallas_call(
        _merge_body,
        grid=(N_TOK // TM, ncb),
        in_specs=[
            pl.BlockSpec((TM, BRANCH), lambda i, j: (i, 0)),
            pl.BlockSpec((TM, BRANCH), lambda i, j: (i, 0)),
            pl.BlockSpec((1, BRANCH, TN), lambda i, j: (0, 0, j)),
            pl.BlockSpec((1, BRANCH, TN), lambda i, j: (1, 0, j)),
            pl.BlockSpec((TM, TN), lambda i, j: (i, j)),
            pl.BlockSpec((TM, TN), lambda i, j: (i, ncb + j)),
        ],
        out_specs=pl.BlockSpec((TM, TN), lambda i, j: (i, j)),
        out_shape=jax.ShapeDtypeStruct((N_TOK, D_MODEL), BF16),
        compiler_params=_cp(("arbitrary", "arbitrary")),
        name="merge",
    )(o, z, w_branch, w_branch, gates, gates)


LN_TM = 128
LN_PROMPT_TILES = N_PROMPT // LN_TM
HALF = D_MODEL // 2


def _pack_bf16_pair(x):
    bits = pltpu.bitcast(x.astype(BF16).astype(F32), jnp.uint32)
    return (bits[:, :HALF] & jnp.uint32(0xFFFF0000)) | (bits[:, HALF:] >> 16)


SLAB = HALF // 128


def _slab_store(ref, row0, x):
    for c in range(SLAB):
        ref[pl.ds(row0 * SLAB + c, x.shape[0], stride=SLAB), :] = x[:, c * 128:(c + 1) * 128]


def _slab_load(ref, row0, n):
    return jnp.concatenate([ref[pl.ds(row0 * SLAB + c, n, stride=SLAB), :] for c in range(SLAB)], axis=1)


def _slab_rows(first, n):
    return pl.ds(pl.multiple_of(first * SLAB, SLAB), n * SLAB)


def _unpack_bf16_pair(w):
    hi = pltpu.bitcast(w & jnp.uint32(0xFFFF0000), F32)
    lo = pltpu.bitcast(w << 16, F32)
    return hi, lo


def _ln_router_body(m_ref, xp_ref, xs_ref, g_ref, b_ref, wr_ref, br_ref, o_ref, op_ref, ids_ref, wts_ref):
    x = jnp.where(pl.program_id(0) < LN_PROMPT_TILES, xp_ref[...], xs_ref[...])
    x1 = _layer_norm(ALPHA * x + m_ref[...]) * g_ref[...] + b_ref[...]
    o_ref[...] = x1
    _slab_store(op_ref, 0, _pack_bf16_pair(x1))
    x_hi = x1.astype(BF16)
    x_mid = (x1 - x_hi.astype(F32)).astype(BF16)
    wr = wr_ref[...]
    both = jnp.dot(x_hi, wr, preferred_element_type=F32)
    logits = (both[:, :128] + both[:, 128:]
              + jnp.dot(x_mid, wr[:, :128], preferred_element_type=F32) + br_ref[...])
    _route(logits, ids_ref, wts_ref)


def _ln_router(mixed, x_p, x_s, g, b, w_r, b_r):
    w_hi = w_r.astype(BF16)
    w_mid = (w_r - w_hi.astype(F32)).astype(BF16)
    wr = jnp.concatenate([w_hi, w_mid], axis=1)
    row_blk = pl.BlockSpec((LN_TM, 128), lambda i: (i, 0))
    return pl.pallas_call(
        _ln_router_body,
        grid=(N_TOK // LN_TM,),
        in_specs=[
            pl.BlockSpec((LN_TM, D_MODEL), lambda i: (i, 0)),
            pl.BlockSpec((LN_TM, D_MODEL), lambda i: (jnp.minimum(i, LN_PROMPT_TILES - 1), 0)),
            pl.BlockSpec((LN_TM, D_MODEL), lambda i: (jnp.maximum(i - LN_PROMPT_TILES, 0), 0)),
            pl.BlockSpec((1, D_MODEL), lambda i: (0, 0)),
            pl.BlockSpec((1, D_MODEL), lambda i: (0, 0)),
            pl.BlockSpec((D_MODEL, 256), lambda i: (0, 0)),
            pl.BlockSpec((1, 128), lambda i: (0, 0)),
        ],
        out_specs=[pl.BlockSpec((LN_TM, D_MODEL), lambda i: (i, 0)),
                   pl.BlockSpec((LN_TM * SLAB, 128), lambda i: (i, 0)), row_blk, row_blk],
        out_shape=[jax.ShapeDtypeStruct((N_TOK, D_MODEL), F32),
                   jax.ShapeDtypeStruct((N_TOK * SLAB, 128), jnp.uint32),
                   jax.ShapeDtypeStruct((N_TOK, 128), jnp.int32),
                   jax.ShapeDtypeStruct((N_TOK, 128), F32)],
        compiler_params=_cp(("arbitrary",)),
        name="ln1_router",
    )(mixed, x_p, x_s, g, b, wr, b_r)


def _route(logits, ids_ref, wts_ref):
    lane = lax.broadcasted_iota(jnp.int32, logits.shape, 1)
    neg = -jnp.inf
    big = 1 << 20
    gl = jnp.where(lane < 8, logits, neg)
    gmax = jnp.max(gl, axis=1, keepdims=True)
    gidx = jnp.min(jnp.where(gl == gmax, lane, big), axis=1, keepdims=True)
    gprob = 1.0 / jnp.sum(jnp.exp(gl - gmax), axis=1, keepdims=True)
    in_group = (lane >= 8) & (lane < 8 + N_EXPERTS) & (((lane - 8) >> 3) == gidx)
    el = jnp.where(in_group, logits, neg)
    v1 = jnp.max(el, axis=1, keepdims=True)
    i1 = jnp.min(jnp.where(el == v1, lane, big), axis=1, keepdims=True)
    el2 = jnp.where(lane == i1, neg, el)
    v2 = jnp.max(el2, axis=1, keepdims=True)
    i2 = jnp.min(jnp.where(el2 == v2, lane, big), axis=1, keepdims=True)
    e2 = jnp.exp(v2 - v1)
    w1 = gprob / (1.0 + e2)
    w2 = gprob * e2 / (1.0 + e2)
    ids_ref[...] = jnp.where(lane == 0, i1 - 8, jnp.where(lane == 1, i2 - 8, 0))
    wts_ref[...] = jnp.where(lane == 0, w1, jnp.where(lane == 1, w2, 0.0))


N_ASSIGN = 2 * N_TOK
MOE_NBLOCKS = (N_ASSIGN + N_EXPERTS * (MOE_BLOCK - 1) + MOE_BLOCK - 1) // MOE_BLOCK
MOE_ROWS = MOE_NBLOCKS * MOE_BLOCK
MOE_HC = MOE_HIDDEN // MOE_HSPLIT


def _row_copy(src_hbm, row, dst, dst_row, sem):
    return pltpu.make_async_copy(src_hbm.at[_slab_rows(row, 1), :], dst.at[_slab_rows(dst_row, 1), :], sem)


MOE_MACRO = 4
MOE_NMACRO = N_ASSIGN // (MOE_MACRO * MOE_BLOCK) + N_EXPERTS + 1
MOE_DCOLS = 1024


def _block_copy(src, t, o_hbm, blk, sem):
    return pltpu.make_async_copy(src.at[_slab_rows(t * MOE_BLOCK, MOE_BLOCK), :],
                                 o_hbm.at[_slab_rows(blk * MOE_BLOCK, MOE_BLOCK), :], sem)


def _moe_body(mexp_ref, mstart_ref, mnsub_ref, tok_ref, x_hbm, wg_ref, wu_hbm, wd_hbm, o_hbm,
              stage, xb, acc, ostage, wubuf, wdbuf, gsem, osem, wsem):
    m = pl.program_id(0)
    h = pl.program_id(1)
    nsub = mnsub_ref[m]
    start = mstart_ref[m]
    last_m = pl.num_programs(0) - 1
    last_h = pl.num_programs(1) - 1

    def wait_blocks(n):
        def done(t, c):
            _block_copy(ostage, t, o_hbm, 0, osem).wait()
            return c

        lax.fori_loop(0, n, done, 0)

    half_rows = D_MODEL // 2

    def w_copies(step, hh):
        e = mexp_ref[step]
        cols = pl.ds(pl.multiple_of(hh * MOE_HC, MOE_HC), MOE_HC)
        return (
            (pltpu.make_async_copy(wu_hbm.at[e, pl.ds(0, half_rows), cols],
                                   wubuf.at[hh, pl.ds(0, half_rows), :], wsem.at[0, hh]), 0),
            (pltpu.make_async_copy(wu_hbm.at[e, pl.ds(half_rows, half_rows), cols],
                                   wubuf.at[hh, pl.ds(half_rows, half_rows), :], wsem.at[1, hh]), 1),
            (pltpu.make_async_copy(wd_hbm.at[e, cols, :], wdbuf.at[hh], wsem.at[2, hh]), 1),
        )

    def w_start(step, hh):
        for cp, prio in w_copies(step, hh):
            cp.start(priority=prio)

    def w_wait(step, hh):
        for cp, _ in w_copies(step, hh):
            cp.wait()

    @pl.when((m == 0) & (h == 0) & (nsub > 0))
    def _():
        w_start(0, 0)

    @pl.when((h == 0) & (nsub > 0))
    def _():
        w_start(m, 1)

    nxt = jnp.minimum(m + 1, last_m)

    @pl.when((h == last_h) & (m < last_m) & (mnsub_ref[nxt] > 0))
    def _():
        w_start(nxt, 0)

    def gather_start(step):
        def sub(t, c):
            base = (mstart_ref[step] + t) * MOE_BLOCK
            off = pl.multiple_of(t * MOE_BLOCK, MOE_BLOCK)
            for r in range(MOE_BLOCK):
                _row_copy(x_hbm, tok_ref[base + r], stage, off + r, gsem).start()
            return c

        lax.fori_loop(0, mnsub_ref[step], sub, 0)

    @pl.when((h == 0) & (m == 0))
    def _():
        gather_start(0)

    @pl.when((h == 0) & (nsub > 0))
    def _():
        def landed(t, c):
            pltpu.make_async_copy(x_hbm.at[_slab_rows(0, MOE_BLOCK), :],
                                  stage.at[_slab_rows(t * MOE_BLOCK, MOE_BLOCK), :], gsem).wait()
            return c

        lax.fori_loop(0, nsub, landed, 0)

        def sub(t, c):
            rows = pl.ds(pl.multiple_of(t * MOE_BLOCK, MOE_BLOCK), MOE_BLOCK)
            hi, lo = _unpack_bf16_pair(_slab_load(stage, t * MOE_BLOCK, MOE_BLOCK))
            xb[rows, 0:HALF] = hi.astype(BF16)
            xb[rows, HALF:D_MODEL] = lo.astype(BF16)
            return c

        lax.fori_loop(0, nsub, sub, 0)

    @pl.when((h == last_h) & (m < last_m))
    def _():
        gather_start(jnp.minimum(m + 1, last_m))

    for ns in range(1, MOE_MACRO + 1):
        @pl.when(nsub == ns)
        def _(ns=ns):
            rows = ns * MOE_BLOCK

            @pl.when(h == 0)
            def _():
                acc[0:rows, :] = jnp.zeros((rows, D_MODEL), F32)

            x = xb[0:rows, :]
            hg = jnp.dot(x, wg_ref[0].astype(BF16), preferred_element_type=F32)
            w_wait(m, h)
            hu = jnp.dot(x, wubuf[h].astype(BF16), preferred_element_type=F32)
            hh = (hg * _sigmoid(hg) * hu).astype(BF16)
            wd = wdbuf[h].astype(BF16)
            for cc in range(D_MODEL // MOE_DCOLS):
                cols = slice(cc * MOE_DCOLS, (cc + 1) * MOE_DCOLS)
                acc[0:rows, cols] += jnp.dot(hh, wd[:, cols], preferred_element_type=F32)

    @pl.when((h == last_h) & (m > 0))
    def _():
        wait_blocks(mnsub_ref[jnp.maximum(m - 1, 0)])

    @pl.when((h == last_h) & (nsub > 0))
    def _():
        def put(t, c):
            rows = pl.ds(pl.multiple_of(t * MOE_BLOCK, MOE_BLOCK), MOE_BLOCK)
            _slab_store(ostage, t * MOE_BLOCK, _pack_bf16_pair(acc[rows, :]))
            _block_copy(ostage, t, o_hbm, start + t, osem).start()
            return c

        lax.fori_loop(0, nsub, put, 0)

        @pl.when(m == last_m)
        def _():
            wait_blocks(nsub)

    @pl.when((h == last_h) & (nsub == 0))
    def _():
        ostage[0:MOE_BLOCK * SLAB, :] = jnp.zeros((MOE_BLOCK * SLAB, 128), jnp.uint32)
        for t in range(MOE_MACRO):
            @pl.when(start + t < MOE_NBLOCKS)
            def _(t=t):
                cp = _block_copy(ostage, 0, o_hbm, start + t, osem)
                cp.start()
                cp.wait()


def _moe_experts(x1p, wg, wu, wd, mexp, mstart, mnsub, row_tok):
    grid_spec = pltpu.PrefetchScalarGridSpec(
        num_scalar_prefetch=4,
        grid=(MOE_NMACRO, MOE_HSPLIT),
        in_specs=[
            pl.BlockSpec(memory_space=pl.ANY),
            pl.BlockSpec((1, D_MODEL, MOE_HC), lambda m, h, me, ms, mn, rt: (me[m], 0, h)),
            pl.BlockSpec(memory_space=pl.ANY),
            pl.BlockSpec(memory_space=pl.ANY),
        ],
        out_specs=pl.BlockSpec(memory_space=pl.ANY),
        scratch_shapes=[pltpu.VMEM((MOE_MACRO * MOE_BLOCK * SLAB, 128), jnp.uint32),
                        pltpu.VMEM((MOE_MACRO * MOE_BLOCK, D_MODEL), BF16),
                        pltpu.VMEM((MOE_MACRO * MOE_BLOCK, D_MODEL), F32),
                        pltpu.VMEM((MOE_MACRO * MOE_BLOCK * SLAB, 128), jnp.uint32),
                        pltpu.VMEM((MOE_HSPLIT, D_MODEL, MOE_HC), F32),
                        pltpu.VMEM((MOE_HSPLIT, MOE_HC, D_MODEL), F32),
                        pltpu.SemaphoreType.DMA(()), pltpu.SemaphoreType.DMA(()),
                        pltpu.SemaphoreType.DMA((3, MOE_HSPLIT))],
    )
    return pl.pallas_call(
        _moe_body,
        grid_spec=grid_spec,
        out_shape=jax.ShapeDtypeStruct((MOE_ROWS * SLAB, 128), jnp.uint32),
        compiler_params=_cp(("arbitrary", "arbitrary")),
        name="moe_experts",
    )(mexp, mstart, mnsub, row_tok, x1p, wg, wu, wd)


CMB_TM = 128
CMB_PROMPT_TILES = N_PROMPT // CMB_TM


def _combine_body(pos_ref, eo_hbm, wts_ref, x1_ref, g_ref, b_ref, yp_ref, ys_ref, buf, sem):
    i = pl.program_id(0)
    slot = i % 2

    def fetch(tile, s):
        base = 2 * tile * CMB_TM
        for r in range(CMB_TM):
            _row_copy(eo_hbm, pos_ref[base + 2 * r], buf.at[s, 0], r, sem.at[s]).start()
            _row_copy(eo_hbm, pos_ref[base + 2 * r + 1], buf.at[s, 1], r, sem.at[s]).start()

    @pl.when(i == 0)
    def _():
        fetch(0, 0)

    for s in range(2):
        @pl.when((i + 1 < pl.num_programs(0)) & (slot == s))
        def _(s=s):
            fetch(i + 1, 1 - s)

    for k in range(2):
        pltpu.make_async_copy(eo_hbm.at[_slab_rows(0, CMB_TM), :], buf.at[slot, k], sem.at[slot]).wait()
    w = wts_ref[...]
    hi0, lo0 = _unpack_bf16_pair(_slab_load(buf.at[slot, 0], 0, CMB_TM))
    hi1, lo1 = _unpack_bf16_pair(_slab_load(buf.at[slot, 1], 0, CMB_TM))
    w0, w1 = w[:, 0:1], w[:, 1:2]
    y = jnp.concatenate([w0 * hi0 + w1 * hi1, w0 * lo0 + w1 * lo1], axis=1)
    x2 = _layer_norm(ALPHA * x1_ref[...] + y) * g_ref[...] + b_ref[...]

    @pl.when(i < CMB_PROMPT_TILES)
    def _():
        yp_ref[...] = x2

    @pl.when(i >= CMB_PROMPT_TILES)
    def _():
        ys_ref[...] = x2


def _combine(pos, eo, wts, x1, g, b):
    grid_spec = pltpu.PrefetchScalarGridSpec(
        num_scalar_prefetch=1,
        grid=(N_TOK // CMB_TM,),
        in_specs=[
            pl.BlockSpec(memory_space=pl.ANY),
            pl.BlockSpec((CMB_TM, 128), lambda i, p: (i, 0)),
            pl.BlockSpec((CMB_TM, D_MODEL), lambda i, p: (i, 0)),
            pl.BlockSpec((1, D_MODEL), lambda i, p: (0, 0)),
            pl.BlockSpec((1, D_MODEL), lambda i, p: (0, 0)),
        ],
        out_specs=[
            pl.BlockSpec((CMB_TM, D_MODEL), lambda i, p: (jnp.minimum(i, CMB_PROMPT_TILES - 1), 0)),
            pl.BlockSpec((CMB_TM, D_MODEL), lambda i, p: (0, 0)),
        ],
        scratch_shapes=[pltpu.VMEM((2, 2, CMB_TM * SLAB, 128), jnp.uint32), pltpu.SemaphoreType.DMA((2,))],
    )
    return pl.pallas_call(
        _combine_body,
        grid_spec=grid_spec,
        out_shape=[jax.ShapeDtypeStruct((N_PROMPT, D_MODEL), F32),
                   jax.ShapeDtypeStruct((N_SAMPLE, D_MODEL), F32)],
        compiler_params=_cp(("arbitrary",)),
        name="combine_ln2",
    )(pos, eo, wts, x1, g, b)


def _route_positions(ids):
    eid = ids[:, :2].reshape(-1)
    onehot = (eid[:, None] == jnp.arange(N_EXPERTS, dtype=jnp.int32)[None, :]).astype(jnp.int32)
    csum = jnp.cumsum(onehot, axis=0)
    rank = jnp.take_along_axis(csum, eid[:, None], axis=1)[:, 0] - 1
    counts = csum[-1]
    nblk_e = (counts + MOE_BLOCK - 1) // MOE_BLOCK
    bend = jnp.cumsum(nblk_e)
    bstart = bend - nblk_e
    pos = bstart[eid] * MOE_BLOCK + rank
    row_tok = jnp.zeros((MOE_ROWS,), jnp.int32).at[pos].set(jnp.arange(N_ASSIGN, dtype=jnp.int32) // 2)
    nstep_e = (nblk_e + MOE_MACRO - 1) // MOE_MACRO
    send = jnp.cumsum(nstep_e)
    n_steps, n_blocks = send[-1], bend[-1]
    m = jnp.arange(MOE_NMACRO, dtype=jnp.int32)
    e_of_m = jnp.minimum(jnp.searchsorted(send, m, side='right'), N_EXPERTS - 1).astype(jnp.int32)
    local = m - (send - nstep_e)[e_of_m]
    valid = m < n_steps
    mexp = jnp.where(valid, e_of_m, e_of_m[jnp.maximum(n_steps - 1, 0)])
    mstart = jnp.where(valid, bstart[e_of_m] + MOE_MACRO * local, n_blocks + MOE_MACRO * (m - n_steps))
    mnsub = jnp.where(valid, jnp.clip(nblk_e[e_of_m] - MOE_MACRO * local, 0, MOE_MACRO), 0)
    return (pos.astype(jnp.int32), row_tok, mexp.astype(jnp.int32), mstart.astype(jnp.int32),
            mnsub.astype(jnp.int32))


def kernel(x_prompt, x_sample, state_gla, state_s5_re, state_s5_im, w_in, w_gla_gate_up, b_gla_gate_up, w_gla_norm, s5_a_re, s5_a_im, s5_b_re, s5_b_im, s5_c_re, s5_c_im, s5_d, s5_log_dt, w_s5_glu, b_s5_glu, w_branch, w_out, ln1_g, ln1_b, w_router_group, b_router_group, w_router_expert, b_router_expert, w_moe_gate, w_moe_up, w_moe_down, ln2_g, ln2_b):
    x_p = x_prompt.reshape(N_PROMPT, D_MODEL)
    x_s = x_sample.reshape(N_SAMPLE, D_MODEL)
    x_bf = jnp.concatenate([x_p.astype(BF16), x_s.astype(BF16)], axis=0)

    w_in_t = w_in.T
    qkvr = _mm_t(x_bf, w_in_t, row0=0, ncols=COL_A, tn=TN, out_dtype=F32, name="proj_qkvr")
    a_low = _mm_t(x_bf, w_in_t, row0=COL_A, ncols=128, tn=128, out_dtype=F32, name="proj_a")
    u = _mm_t(x_bf, w_in_t, row0=COL_U, ncols=BRANCH, tn=TN, out_dtype=F32, name="proj_u")
    gates = _mm_t(x_bf, w_in_t, row0=COL_G, ncols=2 * D_MODEL, tn=TN, out_dtype=F32, name="proj_gates",
                  epilogue=_sigmoid)

    wgu = jnp.pad(w_gla_gate_up, ((0, 128 - GLA_RANK), (0, 0)))
    bgu = b_gla_gate_up.reshape(1, GLA_DK)
    wn = w_gla_norm.reshape(1, BRANCH)
    o_p, gla_p_t = _gla_prompt(qkvr, a_low, wgu, bgu, wn)
    o_s, gla_s = _gla_sample(qkvr, a_low, wgu, bgu, wn, state_gla)
    o_all = jnp.concatenate([o_p, o_s], axis=0)
    gla_p = jnp.swapaxes(gla_p_t, 2, 3)

    s5w = _s5_weights(s5_a_re, s5_a_im, s5_b_re, s5_b_im, s5_c_re, s5_c_im, s5_d, s5_log_dt)
    y, s5_re_p, s5_im_p, s5_re_s, s5_im_s = _s5(
        u, s5w, state_s5_re.reshape(N_SAMPLE, -1), state_s5_im.reshape(N_SAMPLE, -1))
    z = _mm(y, w_s5_glu, col0=0, ncols=BRANCH, tn=TN, out_dtype=BF16, name="s5_glu",
            epilogue=lambda acc, yt, bt: yt * _sigmoid(acc + bt),
            extra=(y, b_s5_glu.reshape(1, BRANCH)),
            extra_specs=(pl.BlockSpec((TM, TN), lambda i, j: (i, j)), pl.BlockSpec((1, TN), lambda i, j: (0, j))))

    pre = _merge(o_all, z, w_branch, gates)
    w_r = jnp.concatenate([w_router_group,
                           jnp.moveaxis(w_router_expert, 0, 1).reshape(D_MODEL, N_EXPERTS),
                           jnp.zeros((D_MODEL, 128 - 8 - N_EXPERTS), F32)], axis=1)
    b_r = jnp.concatenate([b_router_group, b_router_expert.reshape(-1),
                           jnp.zeros((128 - 8 - N_EXPERTS,), F32)]).reshape(1, 128)
    mixed = _mm(pre, w_out, col0=0, ncols=D_MODEL, tn=TN, out_dtype=F32, name="out_proj")
    x1, x1p, ids, wts = _ln_router(mixed, x_p, x_s, ln1_g.reshape(1, D_MODEL), ln1_b.reshape(1, D_MODEL), w_r, b_r)

    pos, row_tok, mexp, mstart, mnsub = _route_positions(ids)
    eo = _moe_experts(x1p,
                      w_moe_gate.reshape(N_EXPERTS, D_MODEL, MOE_HIDDEN),
                      w_moe_up.reshape(N_EXPERTS, D_MODEL, MOE_HIDDEN),
                      w_moe_down.reshape(N_EXPERTS, MOE_HIDDEN, D_MODEL),
                      mexp, mstart, mnsub, row_tok)
    y_p, y_s = _combine(pos, eo, wts, x1, ln2_g.reshape(1, D_MODEL), ln2_b.reshape(1, D_MODEL))

    return (y_p.reshape(N_PROMPT_SEQ, SEQ, D_MODEL), y_s.reshape(N_SAMPLE, 1, D_MODEL),
            gla_p,
            s5_re_p.reshape(N_PROMPT_SEQ, S5_GROUPS, S5_STATE), s5_im_p.reshape(N_PROMPT_SEQ, S5_GROUPS, S5_STATE),
            gla_s,
            s5_re_s.reshape(N_SAMPLE, S5_GROUPS, S5_STATE), s5_im_s.reshape(N_SAMPLE, S5_GROUPS, S5_STATE))
```

```python
import functools
import math

import jax
import jax.numpy as jnp
import numpy as np
from jax import lax
from jax.experimental import pallas as pl
from jax.experimental.pallas import tpu as pltpu

F32 = jnp.float32
BF16 = jnp.bfloat16
HIGHEST = lax.Precision.HIGHEST

D_MODEL = 4096
N_PROMPT_SEQ = 4
SEQ = 2048
N_SAMPLE = 128
N_PROMPT = N_PROMPT_SEQ * SEQ
N_TOK = N_PROMPT + N_SAMPLE
BRANCH = D_MODEL // 2
GLA_HEADS = 4
GLA_DK = D_MODEL // 4
GLA_HDK = GLA_DK // GLA_HEADS
GLA_HDV = BRANCH // GLA_HEADS
GLA_RANK = 16
GLA_TAU = 16.0
GLA_CHUNK = 64
GLA_LEVELS = 6
S5_GROUP = 16
S5_GROUPS = BRANCH // S5_GROUP
S5_STATE = 64
S5_CHUNK = 8
S5_LANE_GROUPS = 8
S5_BLOCKS = S5_GROUPS // S5_LANE_GROUPS
N_EXPERTS = 64
MOE_HIDDEN = D_MODEL // 8
MOE_BLOCK = 128
MOE_HSPLIT = 2
LN_EPS = 1e-5
ALPHA = 2.0 ** 0.25
COL_A = 6144
COL_U = 6160
COL_G = 8208
IN_COLS = 16400

VMEM_LIMIT = 56 * 1024 * 1024
TM = 1040
TN = 512


_NT = (((1,), (1,)), ((), ()))
_TN = (((0,), (0,)), ((), ()))


def _cp(sem):
    return pltpu.CompilerParams(dimension_semantics=sem, vmem_limit_bytes=VMEM_LIMIT)


def _sigmoid(x):
    return 1.0 / (1.0 + jnp.exp(-x))


def _log_sigmoid(z):
    return jnp.minimum(z, 0.0) - jnp.log1p(jnp.exp(-jnp.abs(z)))


def _gelu_tanh(x):
    return 0.5 * x * (1.0 + jnp.tanh(math.sqrt(2.0 / math.pi) * (x + 0.044715 * (x * x * x))))


def _layer_norm(x):
    mu = jnp.mean(x, axis=-1, keepdims=True)
    xc = x - mu
    var = jnp.mean(xc * xc, axis=-1, keepdims=True)
    return xc * lax.rsqrt(var + LN_EPS)


def _mm_body(a_ref, w_ref, *rest, epilogue):
    *extra, o_ref = rest
    acc = jnp.dot(a_ref[...].astype(BF16), w_ref[...].astype(BF16), preferred_element_type=F32)
    o_ref[...] = epilogue(acc, *[e[...] for e in extra]).astype(o_ref.dtype)


def _mm(a, w, *, col0, ncols, tn, out_dtype, name, epilogue=lambda acc: acc, extra=(), extra_specs=()):
    m, k = a.shape
    cb0 = col0 // tn
    return pl.pallas_call(
        functools.partial(_mm_body, epilogue=epilogue),
        grid=(m // TM, ncols // tn),
        in_specs=[pl.BlockSpec((TM, k), lambda i, j: (i, 0)),
                  pl.BlockSpec((k, tn), lambda i, j: (0, cb0 + j)),
                  *extra_specs],
        out_specs=pl.BlockSpec((TM, tn), lambda i, j: (i, j)),
        out_shape=jax.ShapeDtypeStruct((m, ncols), out_dtype),
        compiler_params=_cp(("arbitrary", "arbitrary")),
        name=name,
    )(a, w, *extra)


def _mm_t_body(a_ref, wt_ref, o_ref, *, epilogue):
    acc = lax.dot_general(a_ref[...], wt_ref[...].astype(BF16), _NT, preferred_element_type=F32)
    o_ref[...] = epilogue(acc).astype(o_ref.dtype)


def _mm_t(a, wt, *, row0, ncols, tn, out_dtype, name, epilogue=lambda acc: acc):
    m, k = a.shape
    return pl.pallas_call(
        functools.partial(_mm_t_body, epilogue=epilogue),
        grid=(m // TM, ncols // tn),
        in_specs=[pl.BlockSpec((TM, k), lambda i, j: (i, 0)),
                  pl.BlockSpec((pl.Element(tn), pl.Element(k)),
                               lambda i, j: (pl.multiple_of(row0 + j * tn, 8), 0))],
        out_specs=pl.BlockSpec((TM, tn), lambda i, j: (i, j)),
        out_shape=jax.ShapeDtypeStruct((m, ncols), out_dtype),
        compiler_params=_cp(("arbitrary", "arbitrary")),
        name=name,
    )(a, wt)


def _gla_coeff_matrix():
    c = GLA_CHUNK
    t = np.arange(c)[:, None]
    u = np.arange(c)[None, :]
    blocks = [(u <= t), (u > t)]
    for lvl in range(1, GLA_LEVELS + 1):
        m = 1 << lvl
        half = m // 2
        mid = (t // m) * m + half - 1
        lower = (t % m) >= half
        blocks.append(np.where(lower, (u > mid) & (u <= t), (u > t) & (u <= mid)))
    return np.concatenate(blocks, axis=0).astype(np.float32)


def _gla_level_masks():
    c = GLA_CHUNK
    t = lax.broadcasted_iota(jnp.int32, (c, c), 0)
    s = lax.broadcasted_iota(jnp.int32, (c, c), 1)
    masks = []
    for lvl in range(1, GLA_LEVELS + 1):
        m = 1 << lvl
        half = m // 2
        masks.append(((t >> lvl) == (s >> lvl)) & ((t & (m - 1)) >= half) & ((s & (m - 1)) < half))
    return masks


def _gla_out_norm(o, r, wn):
    return _layer_norm(o) * wn * (r * _sigmoid(r))


def _gla_prompt_body(q_ref, k_ref, v_ref, r_ref, a_ref, wgu_ref, bgu_ref, wn_ref, cm_ref,
                     o_ref, st_ref, s_scr, *, n_sub):
    c = GLA_CHUNK

    @pl.when(pl.program_id(2) == 0)
    def _():
        s_scr[...] = jnp.zeros_like(s_scr)

    masks = _gla_level_masks()
    cm = cm_ref[...]
    nt, tn = _NT, _TN

    def chunk(ci, carry):
        for hh in range(GLA_HP):
            head_chunk(ci, hh)
        return carry

    def head_chunk(ci, hh):
        rows = pl.ds(pl.multiple_of(ci * c, c), c)
        kcols = slice(hh * GLA_HDK, (hh + 1) * GLA_HDK)
        vcols = slice(hh * GLA_HDV, (hh + 1) * GLA_HDV)
        q = q_ref[rows, kcols] * (GLA_HDK ** -0.5)
        k = k_ref[rows, kcols]
        v = v_ref[rows, vcols]
        z = jnp.dot(a_ref[rows, :], wgu_ref[:, kcols], preferred_element_type=F32, precision=HIGHEST) + bgu_ref[:, kcols]
        g = _log_sigmoid(z) * (1.0 / GLA_TAU)
        g_hi = g.astype(BF16)
        r1 = g - g_hi.astype(F32)
        g_mid = r1.astype(BF16)
        g_lo = (r1 - g_mid.astype(F32)).astype(BF16)
        g3 = jnp.concatenate([g_hi, g_mid, g_lo], axis=0)
        f = jnp.exp(jnp.dot(cm, g3, preferred_element_type=F32))
        st = s_scr[hh]
        vb = v.astype(BF16)
        o = lax.dot_general((q * f[0:c]).astype(BF16), st.astype(BF16), nt, preferred_element_type=F32)
        scores = jnp.zeros((c, c), F32)
        for lvl in range(GLA_LEVELS):
            fl = f[(2 + lvl) * c:(3 + lvl) * c]
            p = lax.dot_general((q * fl).astype(BF16), (k * fl).astype(BF16), nt, preferred_element_type=F32)
            scores = scores + jnp.where(masks[lvl], p, 0.0)
        diag = jnp.sum(q * k, axis=1, keepdims=True)
        o = o + jnp.dot(scores.astype(BF16), vb, preferred_element_type=F32) + diag * v
        kd = (k * f[c:2 * c]).astype(BF16)
        s_scr[hh] = st * f[c - 1:c, :] + lax.dot_general(vb, kd, tn, preferred_element_type=F32)
        o_ref[rows, vcols] = _gla_out_norm(o, r_ref[rows, vcols], wn_ref[:, vcols]).astype(o_ref.dtype)

    lax.fori_loop(0, n_sub, chunk, 0, unroll=True)

    @pl.when(pl.program_id(2) == pl.num_programs(2) - 1)
    def _():
        st_ref[0] = s_scr[...]


GLA_HP = 4


def _gla_prompt(qkvr, a_low, wgu, bgu, wn):
    n_sub = 4
    tt = GLA_CHUNK * n_sub
    nt_steps = SEQ // tt
    cm = jnp.asarray(np.tile(_gla_coeff_matrix(), (1, 3)), dtype=BF16)
    rows = lambda b, h, c: b * nt_steps + c
    wk, wv = GLA_HP * GLA_HDK, GLA_HP * GLA_HDV
    n_groups = GLA_HEADS // GLA_HP
    o, st = pl.pallas_call(
        functools.partial(_gla_prompt_body, n_sub=n_sub),
        grid=(N_PROMPT_SEQ, n_groups, nt_steps),
        in_specs=[
            pl.BlockSpec((tt, wk), lambda b, h, c: (rows(b, h, c), h)),
            pl.BlockSpec((tt, wk), lambda b, h, c: (rows(b, h, c), n_groups + h)),
            pl.BlockSpec((tt, wv), lambda b, h, c: (rows(b, h, c), n_groups + h)),
            pl.BlockSpec((tt, wv), lambda b, h, c: (rows(b, h, c), 2 * n_groups + h)),
            pl.BlockSpec((tt, 128), lambda b, h, c: (rows(b, h, c), 0)),
            pl.BlockSpec((128, wk), lambda b, h, c: (0, h)),
            pl.BlockSpec((1, wk), lambda b, h, c: (0, h)),
            pl.BlockSpec((1, wv), lambda b, h, c: (0, h)),
            pl.BlockSpec(cm.shape, lambda b, h, c: (0, 0)),
        ],
        out_specs=[
            pl.BlockSpec((tt, wv), lambda b, h, c: (rows(b, h, c), h)),
            pl.BlockSpec((1, GLA_HP, GLA_HDV, GLA_HDK), lambda b, h, c: (b, h, 0, 0)),
        ],
        out_shape=[
            jax.ShapeDtypeStruct((N_PROMPT, BRANCH), BF16),
            jax.ShapeDtypeStruct((N_PROMPT_SEQ, GLA_HEADS, GLA_HDV, GLA_HDK), F32),
        ],
        scratch_shapes=[pltpu.VMEM((GLA_HP, GLA_HDV, GLA_HDK), F32)],
        compiler_params=_cp(("arbitrary", "arbitrary", "arbitrary")),
        name="gla_prompt",
    )(qkvr, qkvr, qkvr, qkvr, a_low, wgu, bgu, wn, cm)
    return o, st


GLA_SB = 16


def _gla_sample_body(q_ref, k_ref, v_ref, r_ref, a_ref, wgu_ref, bgu_ref, wn_ref, s_ref, o_ref, so_ref):
    q = q_ref[...] * (GLA_HDK ** -0.5)
    k = k_ref[...]
    v = v_ref[...]
    z = jnp.dot(a_ref[...], wgu_ref[...], preferred_element_type=F32, precision=HIGHEST) + bgu_ref[...]
    eg = jnp.exp(_log_sigmoid(z) * (1.0 / GLA_TAU))
    qe = (q * eg).astype(BF16)
    eg_t = eg.T
    k_t = k.T
    rows = []
    for n in range(GLA_SB):
        s0 = s_ref[n, 0]
        rows.append(jnp.dot(qe, s0.astype(BF16), preferred_element_type=F32)[n:n + 1])
        so_ref[n, 0] = s0 * eg_t[:, n:n + 1] + k_t[:, n:n + 1] * v[n:n + 1, :]
    o = jnp.concatenate(rows, axis=0) + jnp.sum(q * k, axis=1, keepdims=True) * v
    o_ref[...] = _gla_out_norm(o, r_ref[...], wn_ref[...]).astype(o_ref.dtype)


def _gla_sample(qkvr, a_low, wgu, bgu, wn, state):
    r0 = N_PROMPT // GLA_SB
    o, st = pl.pallas_call(
        _gla_sample_body,
        grid=(GLA_HEADS, N_SAMPLE // GLA_SB),
        in_specs=[
            pl.BlockSpec((GLA_SB, GLA_HDK), lambda h, i: (r0 + i, h)),
            pl.BlockSpec((GLA_SB, GLA_HDK), lambda h, i: (r0 + i, GLA_HEADS + h)),
            pl.BlockSpec((GLA_SB, GLA_HDV), lambda h, i: (r0 + i, GLA_HEADS + h)),
            pl.BlockSpec((GLA_SB, GLA_HDV), lambda h, i: (r0 + i, 2 * GLA_HEADS + h)),
            pl.BlockSpec((GLA_SB, 128), lambda h, i: (r0 + i, 0)),
            pl.BlockSpec((128, GLA_HDK), lambda h, i: (0, h)),
            pl.BlockSpec((1, GLA_HDK), lambda h, i: (0, h)),
            pl.BlockSpec((1, GLA_HDV), lambda h, i: (0, h)),
            pl.BlockSpec((GLA_SB, 1, GLA_HDK, GLA_HDV), lambda h, i: (i, h, 0, 0)),
        ],
        out_specs=[
            pl.BlockSpec((GLA_SB, GLA_HDV), lambda h, i: (i, h)),
            pl.BlockSpec((GLA_SB, 1, GLA_HDK, GLA_HDV), lambda h, i: (i, h, 0, 0)),
        ],
        out_shape=[
            jax.ShapeDtypeStruct((N_SAMPLE, BRANCH), BF16),
            jax.ShapeDtypeStruct(state.shape, F32),
        ],
        compiler_params=_cp(("arbitrary", "arbitrary")),
        name="gla_sample",
    )(qkvr, qkvr, qkvr, qkvr, a_low, wgu, bgu, wn, state)
    return o, st


def _s5_weights(a_re, a_im, b_re, b_im, c_re, c_im, d, log_dt):
    L = S5_CHUNK
    lam_re = jnp.minimum(a_re, -1e-4)
    lam_im = a_im
    dt = jnp.exp(log_dt)[:, None]
    kk = jnp.arange(L + 1, dtype=F32)[:, None, None]
    pow_re = jnp.exp(lam_re * dt * kk) * jnp.cos(lam_im * dt * kk)
    pow_im = jnp.exp(lam_re * dt * kk) * jnp.sin(lam_im * dt * kk)
    lbar_re, lbar_im = pow_re[1], pow_im[1]
    den = lam_re * lam_re + lam_im * lam_im
    f_re = ((lbar_re - 1.0) * lam_re + lbar_im * lam_im) / den
    f_im = (lbar_im * lam_re - (lbar_re - 1.0) * lam_im) / den
    bb_re = f_re[..., None] * b_re - f_im[..., None] * b_im
    bb_im = f_re[..., None] * b_im + f_im[..., None] * b_re
    nb, ng = S5_BLOCKS, S5_LANE_GROUPS
    bb_re_t = jnp.swapaxes(bb_re, 1, 2)
    bb_im_t = jnp.swapaxes(bb_im, 1, 2)
    lb_re = (pow_re[:L, :, None, :] * bb_re_t - pow_im[:L, :, None, :] * bb_im_t).reshape(L, nb, 128, S5_STATE)
    lb_im = (pow_re[:L, :, None, :] * bb_im_t + pow_im[:L, :, None, :] * bb_re_t).reshape(L, nb, 128, S5_STATE)
    cl_re = (c_re[None] * pow_re[:, :, None, :] - c_im[None] * pow_im[:, :, None, :]).reshape(L + 1, nb, 128, S5_STATE)
    cl_im = (c_re[None] * pow_im[:, :, None, :] + c_im[None] * pow_re[:, :, None, :]).reshape(L + 1, nb, 128, S5_STATE)
    lb = jnp.concatenate([lb_re, lb_im], axis=-1)
    cl = jnp.concatenate([cl_re, -cl_im], axis=-1)
    mult = (L * jnp.arange(1, 9, dtype=F32))[:, None, None]
    sc_re = jnp.exp(lam_re * dt * mult) * jnp.cos(lam_im * dt * mult)
    sc_im = jnp.exp(lam_re * dt * mult) * jnp.sin(lam_im * dt * mult)

    def state_lanes(x):
        lead = x.shape[:-2]
        return jnp.moveaxis(x.reshape(lead + (nb, ng * S5_STATE)), -2, 0)

    scan_mult = jnp.concatenate([state_lanes(sc_re), state_lanes(sc_im)], axis=-1)
    lbar1 = jnp.concatenate([state_lanes(lbar_re[None]), state_lanes(lbar_im[None])], axis=-1)
    dvec = jnp.tile(d.reshape(nb, 1, 128), (1, 1, L))
    return lb, cl, scan_mult, lbar1, dvec


def _s5_expand(src):
    ng, p = S5_LANE_GROUPS, S5_STATE
    lane = lax.broadcasted_iota(jnp.int32, src.shape, 1)
    other = pltpu.roll(src, p, 1)
    re2 = jnp.where(lane < p, src, other)
    im2 = jnp.where(lane < p, other, src)
    full = jnp.concatenate([re2] * (ng // 2) + [im2] * (ng // 2), axis=1)
    row = lax.broadcasted_iota(jnp.int32, full.shape, 0)
    col = lax.broadcasted_iota(jnp.int32, full.shape, 1)
    same_group = (row >> 4) == ((col >> 6) & (ng - 1))
    return jnp.where(same_group, full, 0.0)


def _s5_lag_kernel(zpow, cpow0):
    z_hi = zpow.astype(BF16)
    z_lo = (zpow - z_hi.astype(F32)).astype(BF16)
    c_hi = cpow0.astype(BF16)
    c_lo = (cpow0 - c_hi.astype(F32)).astype(BF16)
    dot = functools.partial(lax.dot_general, dimension_numbers=_NT, preferred_element_type=F32)
    return dot(z_hi, c_hi) + dot(z_hi, c_lo) + dot(z_lo, c_hi)


def _s5_build_weights(lb_ref, cl_ref, t_scr, wz_scr, wc_scr):
    L = S5_CHUNK
    cpow0 = _s5_expand(cl_ref[0, 0])
    t_scr[...] = jnp.zeros_like(t_scr)
    for k in range(L):
        zpow = _s5_expand(lb_ref[k, 0])
        wz_scr[(L - 1 - k) * 128:(L - k) * 128, :] = zpow.astype(BF16)
        wc_scr[k * 128:(k + 1) * 128, :] = _s5_expand(cl_ref[k + 1, 0]).astype(BF16)
        bd = _s5_lag_kernel(zpow, cpow0).astype(BF16)
        for sp in range(L - k):
            t_scr[sp * 128:(sp + 1) * 128, (sp + k) * 128:(sp + k + 1) * 128] = bd


def _s5_prompt_body(u_ref, lb_ref, cl_ref, sm_ref, d_ref, y_ref, fre_ref, fim_ref, t_ref, wz_ref, wc_ref):
    L = S5_CHUNK
    n_rows = SEQ // L
    ns = S5_LANE_GROUPS * S5_STATE

    @pl.when(pl.program_id(1) == 0)
    def _():
        _s5_build_weights(lb_ref, cl_ref, t_ref, wz_ref, wc_ref)

    v = jnp.concatenate([u_ref[pl.ds(s, n_rows, stride=L), :] for s in range(L)], axis=1)
    vb = v.astype(BF16)
    z = jnp.dot(vb, wz_ref[...], preferred_element_type=F32)
    hr, hi = z[:, :ns], z[:, ns:]
    pos = lax.broadcasted_iota(jnp.int32, (n_rows, ns), 0)
    sm = sm_ref[0]
    for d in range(3):
        sh = 1 << d
        ar, ai = sm[sh - 1:sh, :ns], sm[sh - 1:sh, ns:]
        keep = (pos & 7) >= sh
        pr = jnp.where(keep, pltpu.roll(hr, sh, 0), 0.0)
        pi = jnp.where(keep, pltpu.roll(hi, sh, 0), 0.0)
        hr, hi = hr + ar * pr - ai * pi, hi + ar * pi + ai * pr
    gr, gi = sm[:, :ns], sm[:, ns:]
    out_r, out_i = [hr[0:8]], [hi[0:8]]
    for grp in range(1, n_rows // 8):
        cr = jnp.broadcast_to(out_r[-1][7:8], (8, ns))
        ci = jnp.broadcast_to(out_i[-1][7:8], (8, ns))
        out_r.append(hr[grp * 8:(grp + 1) * 8] + gr * cr - gi * ci)
        out_i.append(hi[grp * 8:(grp + 1) * 8] + gr * ci + gi * cr)
    hr = jnp.concatenate(out_r, axis=0)
    hi = jnp.concatenate(out_i, axis=0)
    fre_ref[0] = hr[n_rows - 1:n_rows]
    fim_ref[0] = hi[n_rows - 1:n_rows]
    first = pos >= 1
    h_prev = jnp.concatenate([jnp.where(first, pltpu.roll(hr, 1, 0), 0.0),
                              jnp.where(first, pltpu.roll(hi, 1, 0), 0.0)], axis=1)
    y = (jnp.dot(vb, t_ref[...], preferred_element_type=F32)
         + lax.dot_general(h_prev.astype(BF16), wc_ref[...], _NT, preferred_element_type=F32)
         + d_ref[0] * v)
    y = _gelu_tanh(y)
    for s in range(L):
        y_ref[pl.ds(s, n_rows, stride=L), :] = y[:, s * 128:(s + 1) * 128]


def _s5_sample_body(u_ref, lb_ref, cl_ref, l1_ref, d_ref, hre_ref, him_ref, y_ref, sre_ref, sim_ref):
    ns = S5_LANE_GROUPS * S5_STATE
    bbar = _s5_expand(lb_ref[0, 0])
    k0 = _s5_lag_kernel(bbar, _s5_expand(cl_ref[0, 0])).astype(BF16)
    wc0 = _s5_expand(cl_ref[1, 0]).astype(BF16)
    us = u_ref[...]
    usb = us.astype(BF16)
    h0r, h0i = hre_ref[...], him_ref[...]
    l1 = l1_ref[0]
    bu = jnp.dot(usb, bbar.astype(BF16), preferred_element_type=F32)
    sre_ref[...] = l1[:, :ns] * h0r - l1[:, ns:] * h0i + bu[:, :ns]
    sim_ref[...] = l1[:, :ns] * h0i + l1[:, ns:] * h0r + bu[:, ns:]
    h0 = jnp.concatenate([h0r, h0i], axis=1).astype(BF16)
    ys = (jnp.dot(usb, k0, preferred_element_type=F32)
          + lax.dot_general(h0, wc0, _NT, preferred_element_type=F32)
          + d_ref[0] * us)
    y_ref[...] = _gelu_tanh(ys)


def _s5(u, weights, st_re, st_im):
    lb, cl, scan_mult, lbar1, dvec = weights
    L = S5_CHUNK
    ns = S5_LANE_GROUPS * S5_STATE
    wblk = lambda a: pl.BlockSpec((1,) + a.shape[1:], lambda j, b: (j, 0, 0))
    pblk = lambda a: pl.BlockSpec((a.shape[0], 1, 128, 128), lambda j, b: (0, j, 0, 0))
    mat = pltpu.VMEM((L * 128, L * 128), BF16)
    y, f_re, f_im = pl.pallas_call(
        _s5_prompt_body,
        grid=(S5_BLOCKS, N_PROMPT_SEQ),
        in_specs=[
            pl.BlockSpec((SEQ, 128), lambda j, b: (b, j)),
            pblk(lb), pblk(cl), wblk(scan_mult), wblk(dvec),
        ],
        out_specs=[
            pl.BlockSpec((SEQ, 128), lambda j, b: (b, j)),
            pl.BlockSpec((1, 1, ns), lambda j, b: (b, 0, j)),
            pl.BlockSpec((1, 1, ns), lambda j, b: (b, 0, j)),
        ],
        out_shape=[
            jax.ShapeDtypeStruct((N_PROMPT, BRANCH), F32),
            jax.ShapeDtypeStruct((N_PROMPT_SEQ, 1, S5_GROUPS * S5_STATE), F32),
            jax.ShapeDtypeStruct((N_PROMPT_SEQ, 1, S5_GROUPS * S5_STATE), F32),
        ],
        scratch_shapes=[mat, mat, mat],
        compiler_params=_cp(("arbitrary", "arbitrary")),
        name="s5_prompt",
    )(u, lb, cl, scan_mult, dvec)
    d0 = dvec[:, :, 0:128]
    sblk = lambda a: pl.BlockSpec((1,) + a.shape[1:], lambda j: (j, 0, 0))
    spblk = lambda a: pl.BlockSpec((2, 1, 128, 128), lambda j: (0, j, 0, 0))
    r0 = N_PROMPT // N_SAMPLE
    y_s, s_re, s_im = pl.pallas_call(
        _s5_sample_body,
        grid=(S5_BLOCKS,),
        in_specs=[
            pl.BlockSpec((N_SAMPLE, 128), lambda j: (r0, j)),
            spblk(lb), spblk(cl), sblk(lbar1), sblk(d0),
            pl.BlockSpec((N_SAMPLE, ns), lambda j: (0, j)),
            pl.BlockSpec((N_SAMPLE, ns), lambda j: (0, j)),
        ],
        out_specs=[
            pl.BlockSpec((N_SAMPLE, 128), lambda j: (0, j)),
            pl.BlockSpec((N_SAMPLE, ns), lambda j: (0, j)),
            pl.BlockSpec((N_SAMPLE, ns), lambda j: (0, j)),
        ],
        out_shape=[
            jax.ShapeDtypeStruct((N_SAMPLE, BRANCH), F32),
            jax.ShapeDtypeStruct((N_SAMPLE, S5_GROUPS * S5_STATE), F32),
            jax.ShapeDtypeStruct((N_SAMPLE, S5_GROUPS * S5_STATE), F32),
        ],
        compiler_params=_cp(("arbitrary",)),
        name="s5_sample",
    )(u, lb, cl, lbar1, d0, st_re, st_im)
    return jnp.concatenate([y, y_s], axis=0), f_re, f_im, s_re, s_im


def _merge_body(o_ref, z_ref, w0_ref, w1_ref, g0_ref, g1_ref, out_ref):
    p0 = jnp.dot(o_ref[...], w0_ref[0].astype(BF16), preferred_element_type=F32)
    p1 = jnp.dot(z_ref[...], w1_ref[0].astype(BF16), preferred_element_type=F32)
    out_ref[...] = (g0_ref[...] * p0 + g1_ref[...] * p1).astype(out_ref.dtype)


def _merge(o, z, w_branch, gates):
    ncb = D_MODEL // TN
    return pl.pallas_call(
        _merge_body,
        grid=(N_TOK // TM, ncb),
        in_specs=[
            pl.BlockSpec((TM, BRANCH), lambda i, j: (i, 0)),
            pl.BlockSpec((TM, BRANCH), lambda i, j: (i, 0)),
            pl.BlockSpec((1, BRANCH, TN), lambda i, j: (0, 0, j)),
            pl.BlockSpec((1, BRANCH, TN), lambda i, j: (1, 0, j)),
            pl.BlockSpec((TM, TN), lambda i, j: (i, j)),
            pl.BlockSpec((TM, TN), lambda i, j: (i, ncb + j)),
        ],
        out_specs=pl.BlockSpec((TM, TN), lambda i, j: (i, j)),
        out_shape=jax.ShapeDtypeStruct((N_TOK, D_MODEL), BF16),
        compiler_params=_cp(("arbitrary", "arbitrary")),
        name="merge",
    )(o, z, w_branch, w_branch, gates, gates)


LN_TM = 128
LN_PROMPT_TILES = N_PROMPT // LN_TM
HALF = D_MODEL // 2


def _pack_bf16_pair(x):
    bits = pltpu.bitcast(x.astype(BF16).astype(F32), jnp.uint32)
    return (bits[:, :HALF] & jnp.uint32(0xFFFF0000)) | (bits[:, HALF:] >> 16)


SLAB = HALF // 128


def _slab_store(ref, row0, x):
    for c in range(SLAB):
        ref[pl.ds(row0 * SLAB + c, x.shape[0], stride=SLAB), :] = x[:, c * 128:(c + 1) * 128]


def _slab_load(ref, row0, n):
    return jnp.concatenate([ref[pl.ds(row0 * SLAB + c, n, stride=SLAB), :] for c in range(SLAB)], axis=1)


def _slab_rows(first, n):
    return pl.ds(pl.multiple_of(first * SLAB, SLAB), n * SLAB)


def _unpack_bf16_pair(w):
    hi = pltpu.bitcast(w & jnp.uint32(0xFFFF0000), F32)
    lo = pltpu.bitcast(w << 16, F32)
    return hi, lo


def _ln_router_body(m_ref, xp_ref, xs_ref, g_ref, b_ref, wr_ref, br_ref, o_ref, op_ref, ids_ref, wts_ref):
    x = jnp.where(pl.program_id(0) < LN_PROMPT_TILES, xp_ref[...], xs_ref[...])
    x1 = _layer_norm(ALPHA * x + m_ref[...]) * g_ref[...] + b_ref[...]
    o_ref[...] = x1
    _slab_store(op_ref, 0, _pack_bf16_pair(x1))
    x_hi = x1.astype(BF16)
    x_mid = (x1 - x_hi.astype(F32)).astype(BF16)
    wr = wr_ref[...]
    both = jnp.dot(x_hi, wr, preferred_element_type=F32)
    logits = (both[:, :128] + both[:, 128:]
              + jnp.dot(x_mid, wr[:, :128], preferred_element_type=F32) + br_ref[...])
    _route(logits, ids_ref, wts_ref)


def _ln_router(mixed, x_p, x_s, g, b, w_r, b_r):
    w_hi = w_r.astype(BF16)
    w_mid = (w_r - w_hi.astype(F32)).astype(BF16)
    wr = jnp.concatenate([w_hi, w_mid], axis=1)
    row_blk = pl.BlockSpec((LN_TM, 128), lambda i: (i, 0))
    return pl.pallas_call(
        _ln_router_body,
        grid=(N_TOK // LN_TM,),
        in_specs=[
            pl.BlockSpec((LN_TM, D_MODEL), lambda i: (i, 0)),
            pl.BlockSpec((LN_TM, D_MODEL), lambda i: (jnp.minimum(i, LN_PROMPT_TILES - 1), 0)),
            pl.BlockSpec((LN_TM, D_MODEL), lambda i: (jnp.maximum(i - LN_PROMPT_TILES, 0), 0)),
            pl.BlockSpec((1, D_MODEL), lambda i: (0, 0)),
            pl.BlockSpec((1, D_MODEL), lambda i: (0, 0)),
            pl.BlockSpec((D_MODEL, 256), lambda i: (0, 0)),
            pl.BlockSpec((1, 128), lambda i: (0, 0)),
        ],
        out_specs=[pl.BlockSpec((LN_TM, D_MODEL), lambda i: (i, 0)),
                   pl.BlockSpec((LN_TM * SLAB, 128), lambda i: (i, 0)), row_blk, row_blk],
        out_shape=[jax.ShapeDtypeStruct((N_TOK, D_MODEL), F32),
                   jax.ShapeDtypeStruct((N_TOK * SLAB, 128), jnp.uint32),
                   jax.ShapeDtypeStruct((N_TOK, 128), jnp.int32),
                   jax.ShapeDtypeStruct((N_TOK, 128), F32)],
        compiler_params=_cp(("arbitrary",)),
        name="ln1_router",
    )(mixed, x_p, x_s, g, b, wr, b_r)


def _route(logits, ids_ref, wts_ref):
    lane = lax.broadcasted_iota(jnp.int32, logits.shape, 1)
    neg = -jnp.inf
    big = 1 << 20
    gl = jnp.where(lane < 8, logits, neg)
    gmax = jnp.max(gl, axis=1, keepdims=True)
    gidx = jnp.min(jnp.where(gl == gmax, lane, big), axis=1, keepdims=True)
    gprob = 1.0 / jnp.sum(jnp.exp(gl - gmax), axis=1, keepdims=True)
    in_group = (lane >= 8) & (lane < 8 + N_EXPERTS) & (((lane - 8) >> 3) == gidx)
    el = jnp.where(in_group, logits, neg)
    v1 = jnp.max(el, axis=1, keepdims=True)
    i1 = jnp.min(jnp.where(el == v1, lane, big), axis=1, keepdims=True)
    el2 = jnp.where(lane == i1, neg, el)
    v2 = jnp.max(el2, axis=1, keepdims=True)
    i2 = jnp.min(jnp.where(el2 == v2, lane, big), axis=1, keepdims=True)
    e2 = jnp.exp(v2 - v1)
    w1 = gprob / (1.0 + e2)
    w2 = gprob * e2 / (1.0 + e2)
    ids_ref[...] = jnp.where(lane == 0, i1 - 8, jnp.where(lane == 1, i2 - 8, 0))
    wts_ref[...] = jnp.where(lane == 0, w1, jnp.where(lane == 1, w2, 0.0))


N_ASSIGN = 2 * N_TOK
MOE_NBLOCKS = (N_ASSIGN + N_EXPERTS * (MOE_BLOCK - 1) + MOE_BLOCK - 1) // MOE_BLOCK
MOE_ROWS = MOE_NBLOCKS * MOE_BLOCK
MOE_HC = MOE_HIDDEN // MOE_HSPLIT


def _row_copy(src_hbm, row, dst, dst_row, sem):
    return pltpu.make_async_copy(src_hbm.at[_slab_rows(row, 1), :], dst.at[_slab_rows(dst_row, 1), :], sem)


MOE_MACRO = 4
MOE_NMACRO = N_ASSIGN // (MOE_MACRO * MOE_BLOCK) + N_EXPERTS + 1
MOE_DCOLS = 1024


def _block_copy(src, t, o_hbm, blk, sem):
    return pltpu.make_async_copy(src.at[_slab_rows(t * MOE_BLOCK, MOE_BLOCK), :],
                                 o_hbm.at[_slab_rows(blk * MOE_BLOCK, MOE_BLOCK), :], sem)


def _moe_body(mexp_ref, mstart_ref, mnsub_ref, tok_ref, x_hbm, wg_ref, wu_hbm, wd_hbm, o_hbm,
              stage, xb, acc, ostage, wubuf, wdbuf, gsem, osem, wsem):
    m = pl.program_id(0)
    h = pl.program_id(1)
    nsub = mnsub_ref[m]
    start = mstart_ref[m]
    last_m = pl.num_programs(0) - 1
    last_h = pl.num_programs(1) - 1

    def wait_blocks(n):
        def done(t, c):
            _block_copy(ostage, t, o_hbm, 0, osem).wait()
            return c

        lax.fori_loop(0, n, done, 0)

    half_rows = D_MODEL // 2

    def w_copies(step, hh):
        e = mexp_ref[step]
        cols = pl.ds(pl.multiple_of(hh * MOE_HC, MOE_HC), MOE_HC)
        return (
            (pltpu.make_async_copy(wu_hbm.at[e, pl.ds(0, half_rows), cols],
                                   wubuf.at[hh, pl.ds(0, half_rows), :], wsem.at[0, hh]), 0),
            (pltpu.make_async_copy(wu_hbm.at[e, pl.ds(half_rows, half_rows), cols],
                                   wubuf.at[hh, pl.ds(half_rows, half_rows), :], wsem.at[1, hh]), 1),
            (pltpu.make_async_copy(wd_hbm.at[e, cols, :], wdbuf.at[hh], wsem.at[2, hh]), 1),
        )

    def w_start(step, hh):
        for cp, prio in w_copies(step, hh):
            cp.start(priority=prio)

    def w_wait(step, hh):
        for cp, _ in w_copies(step, hh):
            cp.wait()

    @pl.when((m == 0) & (h == 0) & (nsub > 0))
    def _():
        w_start(0, 0)

    @pl.when((h == 0) & (nsub > 0))
    def _():
        w_start(m, 1)

    nxt = jnp.minimum(m + 1, last_m)

    @pl.when((h == last_h) & (m < last_m) & (mnsub_ref[nxt] > 0))
    def _():
        w_start(nxt, 0)

    def gather_start(step):
        def sub(t, c):
            base = (mstart_ref[step] + t) * MOE_BLOCK
            off = pl.multiple_of(t * MOE_BLOCK, MOE_BLOCK)
            for r in range(MOE_BLOCK):
                _row_copy(x_hbm, tok_ref[base + r], stage, off + r, gsem).start()
            return c

        lax.fori_loop(0, mnsub_ref[step], sub, 0)

    @pl.when((h == 0) & (m == 0))
    def _():
        gather_start(0)

    @pl.when((h == 0) & (nsub > 0))
    def _():
        def landed(t, c):
            pltpu.make_async_copy(x_hbm.at[_slab_rows(0, MOE_BLOCK), :],
                                  stage.at[_slab_rows(t * MOE_BLOCK, MOE_BLOCK), :], gsem).wait()
            return c

        lax.fori_loop(0, nsub, landed, 0)

        def sub(t, c):
            rows = pl.ds(pl.multiple_of(t * MOE_BLOCK, MOE_BLOCK), MOE_BLOCK)
            hi, lo = _unpack_bf16_pair(_slab_load(stage, t * MOE_BLOCK, MOE_BLOCK))
            xb[rows, 0:HALF] = hi.astype(BF16)
            xb[rows, HALF:D_MODEL] = lo.astype(BF16)
            return c

        lax.fori_loop(0, nsub, sub, 0)

    @pl.when((h == last_h) & (m < last_m))
    def _():
        gather_start(jnp.minimum(m + 1, last_m))

    for ns in range(1, MOE_MACRO + 1):
        @pl.when(nsub == ns)
        def _(ns=ns):
            rows = ns * MOE_BLOCK

            @pl.when(h == 0)
            def _():
                acc[0:rows, :] = jnp.zeros((rows, D_MODEL), F32)

            x = xb[0:rows, :]
            hg = jnp.dot(x, wg_ref[0].astype(BF16), preferred_element_type=F32)
            w_wait(m, h)
            hu = jnp.dot(x, wubuf[h].astype(BF16), preferred_element_type=F32)
            hh = (hg * _sigmoid(hg) * hu).astype(BF16)
            wd = wdbuf[h].astype(BF16)
            for cc in range(D_MODEL // MOE_DCOLS):
                cols = slice(cc * MOE_DCOLS, (cc + 1) * MOE_DCOLS)
                acc[0:rows, cols] += jnp.dot(hh, wd[:, cols], preferred_element_type=F32)

    @pl.when((h == last_h) & (m > 0))
    def _():
        wait_blocks(mnsub_ref[jnp.maximum(m - 1, 0)])

    @pl.when((h == last_h) & (nsub > 0))
    def _():
        def put(t, c):
            rows = pl.ds(pl.multiple_of(t * MOE_BLOCK, MOE_BLOCK), MOE_BLOCK)
            _slab_store(ostage, t * MOE_BLOCK, _pack_bf16_pair(acc[rows, :]))
            _block_copy(ostage, t, o_hbm, start + t, osem).start()
            return c

        lax.fori_loop(0, nsub, put, 0)

        @pl.when(m == last_m)
        def _():
            wait_blocks(nsub)

    @pl.when((h == last_h) & (nsub == 0))
    def _():
        ostage[0:MOE_BLOCK * SLAB, :] = jnp.zeros((MOE_BLOCK * SLAB, 128), jnp.uint32)
        for t in range(MOE_MACRO):
            @pl.when(start + t < MOE_NBLOCKS)
            def _(t=t):
                cp = _block_copy(ostage, 0, o_hbm, start + t, osem)
                cp.start()
                cp.wait()


def _moe_experts(x1p, wg, wu, wd, mexp, mstart, mnsub, row_tok):
    grid_spec = pltpu.PrefetchScalarGridSpec(
        num_scalar_prefetch=4,
        grid=(MOE_NMACRO, MOE_HSPLIT),
        in_specs=[
            pl.BlockSpec(memory_space=pl.ANY),
            pl.BlockSpec((1, D_MODEL, MOE_HC), lambda m, h, me, ms, mn, rt: (me[m], 0, h)),
            pl.BlockSpec(memory_space=pl.ANY),
            pl.BlockSpec(memory_space=pl.ANY),
        ],
        out_specs=pl.BlockSpec(memory_space=pl.ANY),
        scratch_shapes=[pltpu.VMEM((MOE_MACRO * MOE_BLOCK * SLAB, 128), jnp.uint32),
                        pltpu.VMEM((MOE_MACRO * MOE_BLOCK, D_MODEL), BF16),
                        pltpu.VMEM((MOE_MACRO * MOE_BLOCK, D_MODEL), F32),
                        pltpu.VMEM((MOE_MACRO * MOE_BLOCK * SLAB, 128), jnp.uint32),
                        pltpu.VMEM((MOE_HSPLIT, D_MODEL, MOE_HC), F32),
                        pltpu.VMEM((MOE_HSPLIT, MOE_HC, D_MODEL), F32),
                        pltpu.SemaphoreType.DMA(()), pltpu.SemaphoreType.DMA(()),
                        pltpu.SemaphoreType.DMA((3, MOE_HSPLIT))],
    )
    return pl.pallas_call(
        _moe_body,
        grid_spec=grid_spec,
        out_shape=jax.ShapeDtypeStruct((MOE_ROWS * SLAB, 128), jnp.uint32),
        compiler_params=_cp(("arbitrary", "arbitrary")),
        name="moe_experts",
    )(mexp, mstart, mnsub, row_tok, x1p, wg, wu, wd)


CMB_TM = 128
CMB_PROMPT_TILES = N_PROMPT // CMB_TM


def _combine_body(pos_ref, eo_hbm, wts_ref, x1_ref, g_ref, b_ref, yp_ref, ys_ref, buf, sem):
    i = pl.program_id(0)
    slot = i % 2

    def fetch(tile, s):
        base = 2 * tile * CMB_TM
        for r in range(CMB_TM):
            _row_copy(eo_hbm, pos_ref[base + 2 * r], buf.at[s, 0], r, sem.at[s]).start()
            _row_copy(eo_hbm, pos_ref[base + 2 * r + 1], buf.at[s, 1], r, sem.at[s]).start()

    @pl.when(i == 0)
    def _():
        fetch(0, 0)

    for s in range(2):
        @pl.when((i + 1 < pl.num_programs(0)) & (slot == s))
        def _(s=s):
            fetch(i + 1, 1 - s)

    for k in range(2):
        pltpu.make_async_copy(eo_hbm.at[_slab_rows(0, CMB_TM), :], buf.at[slot, k], sem.at[slot]).wait()
    w = wts_ref[...]
    hi0, lo0 = _unpack_bf16_pair(_slab_load(buf.at[slot, 0], 0, CMB_TM))
    hi1, lo1 = _unpack_bf16_pair(_slab_load(buf.at[slot, 1], 0, CMB_TM))
    w0, w1 = w[:, 0:1], w[:, 1:2]
    y = jnp.concatenate([w0 * hi0 + w1 * hi1, w0 * lo0 + w1 * lo1], axis=1)
    x2 = _layer_norm(ALPHA * x1_ref[...] + y) * g_ref[...] + b_ref[...]

    @pl.when(i < CMB_PROMPT_TILES)
    def _():
        yp_ref[...] = x2

    @pl.when(i >= CMB_PROMPT_TILES)
    def _():
        ys_ref[...] = x2


def _combine(pos, eo, wts, x1, g, b):
    grid_spec = pltpu.PrefetchScalarGridSpec(
        num_scalar_prefetch=1,
        grid=(N_TOK // CMB_TM,),
        in_specs=[
            pl.BlockSpec(memory_space=pl.ANY),
            pl.BlockSpec((CMB_TM, 128), lambda i, p: (i, 0)),
            pl.BlockSpec((CMB_TM, D_MODEL), lambda i, p: (i, 0)),
            pl.BlockSpec((1, D_MODEL), lambda i, p: (0, 0)),
            pl.BlockSpec((1, D_MODEL), lambda i, p: (0, 0)),
        ],
        out_specs=[
            pl.BlockSpec((CMB_TM, D_MODEL), lambda i, p: (jnp.minimum(i, CMB_PROMPT_TILES - 1), 0)),
            pl.BlockSpec((CMB_TM, D_MODEL), lambda i, p: (0, 0)),
        ],
        scratch_shapes=[pltpu.VMEM((2, 2, CMB_TM * SLAB, 128), jnp.uint32), pltpu.SemaphoreType.DMA((2,))],
    )
    return pl.pallas_call(
        _combine_body,
        grid_spec=grid_spec,
        out_shape=[jax.ShapeDtypeStruct((N_PROMPT, D_MODEL), F32),
                   jax.ShapeDtypeStruct((N_SAMPLE, D_MODEL), F32)],
        compiler_params=_cp(("arbitrary",)),
        name="combine_ln2",
    )(pos, eo, wts, x1, g, b)


def _route_positions(ids):
    eid = ids[:, :2].reshape(-1)
    onehot = (eid[:, None] == jnp.arange(N_EXPERTS, dtype=jnp.int32)[None, :]).astype(jnp.int32)
    csum = jnp.cumsum(onehot, axis=0)
    rank = jnp.take_along_axis(csum, eid[:, None], axis=1)[:, 0] - 1
    counts = csum[-1]
    nblk_e = (counts + MOE_BLOCK - 1) // MOE_BLOCK
    bend = jnp.cumsum(nblk_e)
    bstart = bend - nblk_e
    pos = bstart[eid] * MOE_BLOCK + rank
    row_tok = jnp.zeros((MOE_ROWS,), jnp.int32).at[pos].set(jnp.arange(N_ASSIGN, dtype=jnp.int32) // 2)
    nstep_e = (nblk_e + MOE_MACRO - 1) // MOE_MACRO
    send = jnp.cumsum(nstep_e)
    n_steps, n_blocks = send[-1], bend[-1]
    m = jnp.arange(MOE_NMACRO, dtype=jnp.int32)
    e_of_m = jnp.minimum(jnp.searchsorted(send, m, side='right'), N_EXPERTS - 1).astype(jnp.int32)
    local = m - (send - nstep_e)[e_of_m]
    valid = m < n_steps
    mexp = jnp.where(valid, e_of_m, e_of_m[jnp.maximum(n_steps - 1, 0)])
    mstart = jnp.where(valid, bstart[e_of_m] + MOE_MACRO * local, n_blocks + MOE_MACRO * (m - n_steps))
    mnsub = jnp.where(valid, jnp.clip(nblk_e[e_of_m] - MOE_MACRO * local, 0, MOE_MACRO), 0)
    return (pos.astype(jnp.int32), row_tok, mexp.astype(jnp.int32), mstart.astype(jnp.int32),
            mnsub.astype(jnp.int32))


def kernel(x_prompt, x_sample, state_gla, state_s5_re, state_s5_im, w_in, w_gla_gate_up, b_gla_gate_up, w_gla_norm, s5_a_re, s5_a_im, s5_b_re, s5_b_im, s5_c_re, s5_c_im, s5_d, s5_log_dt, w_s5_glu, b_s5_glu, w_branch, w_out, ln1_g, ln1_b, w_router_group, b_router_group, w_router_expert, b_router_expert, w_moe_gate, w_moe_up, w_moe_down, ln2_g, ln2_b):
    x_p = x_prompt.reshape(N_PROMPT, D_MODEL)
    x_s = x_sample.reshape(N_SAMPLE, D_MODEL)
    x_bf = jnp.concatenate([x_p.astype(BF16), x_s.astype(BF16)], axis=0)

    w_in_t = w_in.T
    qkvr = _mm_t(x_bf, w_in_t, row0=0, ncols=COL_A, tn=TN, out_dtype=F32, name="proj_qkvr")
    a_low = _mm_t(x_bf, w_in_t, row0=COL_A, ncols=128, tn=128, out_dtype=F32, name="proj_a")
    u = _mm_t(x_bf, w_in_t, row0=COL_U, ncols=BRANCH, tn=TN, out_dtype=F32, name="proj_u")
    gates = _mm_t(x_bf, w_in_t, row0=COL_G, ncols=2 * D_MODEL, tn=TN, out_dtype=F32, name="proj_gates",
                  epilogue=_sigmoid)

    wgu = jnp.pad(w_gla_gate_up, ((0, 128 - GLA_RANK), (0, 0)))
    bgu = b_gla_gate_up.reshape(1, GLA_DK)
    wn = w_gla_norm.reshape(1, BRANCH)
    o_p, gla_p_t = _gla_prompt(qkvr, a_low, wgu, bgu, wn)
    o_s, gla_s = _gla_sample(qkvr, a_low, wgu, bgu, wn, state_gla)
    o_all = jnp.concatenate([o_p, o_s], axis=0)
    gla_p = jnp.swapaxes(gla_p_t, 2, 3)

    s5w = _s5_weights(s5_a_re, s5_a_im, s5_b_re, s5_b_im, s5_c_re, s5_c_im, s5_d, s5_log_dt)
    y, s5_re_p, s5_im_p, s5_re_s, s5_im_s = _s5(
        u, s5w, state_s5_re.reshape(N_SAMPLE, -1), state_s5_im.reshape(N_SAMPLE, -1))
    z = _mm(y, w_s5_glu, col0=0, ncols=BRANCH, tn=TN, out_dtype=BF16, name="s5_glu",
            epilogue=lambda acc, yt, bt: yt * _sigmoid(acc + bt),
            extra=(y, b_s5_glu.reshape(1, BRANCH)),
            extra_specs=(pl.BlockSpec((TM, TN), lambda i, j: (i, j)), pl.BlockSpec((1, TN), lambda i, j: (0, j))))

    pre = _merge(o_all, z, w_branch, gates)
    w_r = jnp.concatenate([w_router_group,
                           jnp.moveaxis(w_router_expert, 0, 1).reshape(D_MODEL, N_EXPERTS),
                           jnp.zeros((D_MODEL, 128 - 8 - N_EXPERTS), F32)], axis=1)
    b_r = jnp.concatenate([b_router_group, b_router_expert.reshape(-1),
                           jnp.zeros((128 - 8 - N_EXPERTS,), F32)]).reshape(1, 128)
    mixed = _mm(pre, w_out, col0=0, ncols=D_MODEL, tn=TN, out_dtype=F32, name="out_proj")
    x1, x1p, ids, wts = _ln_router(mixed, x_p, x_s, ln1_g.reshape(1, D_MODEL), ln1_b.reshape(1, D_MODEL), w_r, b_r)

    pos, row_tok, mexp, mstart, mnsub = _route_positions(ids)
    eo = _moe_experts(x1p,
                      w_moe_gate.reshape(N_EXPERTS, D_MODEL, MOE_HIDDEN),
                      w_moe_up.reshape(N_EXPERTS, D_MODEL, MOE_HIDDEN),
                      w_moe_down.reshape(N_EXPERTS, MOE_HIDDEN, D_MODEL),
                      mexp, mstart, mnsub, row_tok)
    y_p, y_s = _combine(pos, eo, wts, x1, ln2_g.reshape(1, D_MODEL), ln2_b.reshape(1, D_MODEL))

    return (y_p.reshape(N_PROMPT_SEQ, SEQ, D_MODEL), y_s.reshape(N_SAMPLE, 1, D_MODEL),
            gla_p,
            s5_re_p.reshape(N_PROMPT_SEQ, S5_GROUPS, S5_STATE), s5_im_p.reshape(N_PROMPT_SEQ, S5_GROUPS, S5_STATE),
            gla_s,
            s5_re_s.reshape(N_SAMPLE, S5_GROUPS, S5_STATE), s5_im_s.reshape(N_SAMPLE, S5_GROUPS, S5_STATE))
```

```python
import functools
import math

import jax
import jax.numpy as jnp
import numpy as np
from jax import lax
from jax.experimental import pallas as pl
from jax.experimental.pallas import tpu as pltpu

F32 = jnp.float32
BF16 = jnp.bfloat16
HIGHEST = lax.Precision.HIGHEST

D_MODEL = 4096
N_PROMPT_SEQ = 4
SEQ = 2048
N_SAMPLE = 128
N_PROMPT = N_PROMPT_SEQ * SEQ
N_TOK = N_PROMPT + N_SAMPLE
BRANCH = D_MODEL // 2
GLA_HEADS = 4
GLA_DK = D_MODEL // 4
GLA_HDK = GLA_DK // GLA_HEADS
GLA_HDV = BRANCH // GLA_HEADS
GLA_RANK = 16
GLA_TAU = 16.0
GLA_CHUNK = 64
GLA_LEVELS = 6
S5_GROUP = 16
S5_GROUPS = BRANCH // S5_GROUP
S5_STATE = 64
S5_CHUNK = 8
S5_LANE_GROUPS = 8
S5_BLOCKS = S5_GROUPS // S5_LANE_GROUPS
N_EXPERTS = 64
MOE_HIDDEN = D_MODEL // 8
MOE_BLOCK = 128
MOE_HSPLIT = 2
LN_EPS = 1e-5
ALPHA = 2.0 ** 0.25
COL_A = 6144
COL_U = 6160
COL_G = 8208
IN_COLS = 16400

VMEM_LIMIT = 56 * 1024 * 1024
TM = 1040
TN = 512


_NT = (((1,), (1,)), ((), ()))
_TN = (((0,), (0,)), ((), ()))


def _cp(sem):
    return pltpu.CompilerParams(dimension_semantics=sem, vmem_limit_bytes=VMEM_LIMIT)


def _sigmoid(x):
    return 1.0 / (1.0 + jnp.exp(-x))


def _log_sigmoid(z):
    return jnp.minimum(z, 0.0) - jnp.log1p(jnp.exp(-jnp.abs(z)))


def _gelu_tanh(x):
    return 0.5 * x * (1.0 + jnp.tanh(math.sqrt(2.0 / math.pi) * (x + 0.044715 * (x * x * x))))


def _layer_norm(x):
    mu = jnp.mean(x, axis=-1, keepdims=True)
    xc = x - mu
    var = jnp.mean(xc * xc, axis=-1, keepdims=True)
    return xc * lax.rsqrt(var + LN_EPS)


def _mm_body(a_ref, w_ref, *rest, epilogue):
    *extra, o_ref = rest
    acc = jnp.dot(a_ref[...].astype(BF16), w_ref[...].astype(BF16), preferred_element_type=F32)
    o_ref[...] = epilogue(acc, *[e[...] for e in extra]).astype(o_ref.dtype)


def _mm(a, w, *, col0, ncols, tn, out_dtype, name, epilogue=lambda acc: acc, extra=(), extra_specs=()):
    m, k = a.shape
    cb0 = col0 // tn
    return pl.pallas_call(
        functools.partial(_mm_body, epilogue=epilogue),
        grid=(m // TM, ncols // tn),
        in_specs=[pl.BlockSpec((TM, k), lambda i, j: (i, 0)),
                  pl.BlockSpec((k, tn), lambda i, j: (0, cb0 + j)),
                  *extra_specs],
        out_specs=pl.BlockSpec((TM, tn), lambda i, j: (i, j)),
        out_shape=jax.ShapeDtypeStruct((m, ncols), out_dtype),
        compiler_params=_cp(("arbitrary", "arbitrary")),
        name=name,
    )(a, w, *extra)


def _mm_t_body(a_ref, wt_ref, o_ref, *, epilogue):
    acc = lax.dot_general(a_ref[...], wt_ref[...].astype(BF16), _NT, preferred_element_type=F32)
    o_ref[...] = epilogue(acc).astype(o_ref.dtype)


def _mm_t(a, wt, *, row0, ncols, tn, out_dtype, name, epilogue=lambda acc: acc):
    m, k = a.shape
    return pl.pallas_call(
        functools.partial(_mm_t_body, epilogue=epilogue),
        grid=(m // TM, ncols // tn),
        in_specs=[pl.BlockSpec((TM, k), lambda i, j: (i, 0)),
                  pl.BlockSpec((pl.Element(tn), pl.Element(k)),
                               lambda i, j: (pl.multiple_of(row0 + j * tn, 8), 0))],
        out_specs=pl.BlockSpec((TM, tn), lambda i, j: (i, j)),
        out_shape=jax.ShapeDtypeStruct((m, ncols), out_dtype),
        compiler_params=_cp(("arbitrary", "arbitrary")),
        name=name,
    )(a, wt)


def _gla_coeff_matrix():
    c = GLA_CHUNK
    t = np.arange(c)[:, None]
    u = np.arange(c)[None, :]
    blocks = [(u <= t), (u > t)]
    for lvl in range(1, GLA_LEVELS + 1):
        m = 1 << lvl
        half = m // 2
        mid = (t // m) * m + half - 1
        lower = (t % m) >= half
        blocks.append(np.where(lower, (u > mid) & (u <= t), (u > t) & (u <= mid)))
    return np.concatenate(blocks, axis=0).astype(np.float32)


def _gla_level_masks():
    c = GLA_CHUNK
    t = lax.broadcasted_iota(jnp.int32, (c, c), 0)
    s = lax.broadcasted_iota(jnp.int32, (c, c), 1)
    masks = []
    for lvl in range(1, GLA_LEVELS + 1):
        m = 1 << lvl
        half = m // 2
        masks.append(((t >> lvl) == (s >> lvl)) & ((t & (m - 1)) >= half) & ((s & (m - 1)) < half))
    return masks


def _gla_out_norm(o, r, wn):
    return _layer_norm(o) * wn * (r * _sigmoid(r))


def _gla_prompt_body(q_ref, k_ref, v_ref, r_ref, a_ref, wgu_ref, bgu_ref, wn_ref, cm_ref,
                     o_ref, st_ref, s_scr, *, n_sub):
    c = GLA_CHUNK

    @pl.when(pl.program_id(2) == 0)
    def _():
        s_scr[...] = jnp.zeros_like(s_scr)

    masks = _gla_level_masks()
    cm = cm_ref[...]
    nt, tn = _NT, _TN

    def chunk(ci, carry):
        for hh in range(GLA_HP):
            head_chunk(ci, hh)
        return carry

    def head_chunk(ci, hh):
        rows = pl.ds(pl.multiple_of(ci * c, c), c)
        kcols = slice(hh * GLA_HDK, (hh + 1) * GLA_HDK)
        vcols = slice(hh * GLA_HDV, (hh + 1) * GLA_HDV)
        q = q_ref[rows, kcols] * (GLA_HDK ** -0.5)
        k = k_ref[rows, kcols]
        v = v_ref[rows, vcols]
        z = jnp.dot(a_ref[rows, :], wgu_ref[:, kcols], preferred_element_type=F32, precision=HIGHEST) + bgu_ref[:, kcols]
        g = _log_sigmoid(z) * (1.0 / GLA_TAU)
        g_hi = g.astype(BF16)
        r1 = g - g_hi.astype(F32)
        g_mid = r1.astype(BF16)
        g_lo = (r1 - g_mid.astype(F32)).astype(BF16)
        g3 = jnp.concatenate([g_hi, g_mid, g_lo], axis=0)
        f = jnp.exp(jnp.dot(cm, g3, preferred_element_type=F32))
        st = s_scr[hh]
        vb = v.astype(BF16)
        o = lax.dot_general((q * f[0:c]).astype(BF16), st.astype(BF16), nt, preferred_element_type=F32)
        scores = jnp.zeros((c, c), F32)
        for lvl in range(GLA_LEVELS):
            fl = f[(2 + lvl) * c:(3 + lvl) * c]
            p = lax.dot_general((q * fl).astype(BF16), (k * fl).astype(BF16), nt, preferred_element_type=F32)
            scores = scores + jnp.where(masks[lvl], p, 0.0)
        diag = jnp.sum(q * k, axis=1, keepdims=True)
        o = o + jnp.dot(scores.astype(BF16), vb, preferred_element_type=F32) + diag * v
        kd = (k * f[c:2 * c]).astype(BF16)
        s_scr[hh] = st * f[c - 1:c, :] + lax.dot_general(vb, kd, tn, preferred_element_type=F32)
        o_ref[rows, vcols] = _gla_out_norm(o, r_ref[rows, vcols], wn_ref[:, vcols]).astype(o_ref.dtype)

    lax.fori_loop(0, n_sub, chunk, 0, unroll=True)

    @pl.when(pl.program_id(2) == pl.num_programs(2) - 1)
    def _():
        st_ref[0] = s_scr[...]


GLA_HP = 4


def _gla_prompt(qkvr, a_low, wgu, bgu, wn):
    n_sub = 4
    tt = GLA_CHUNK * n_sub
    nt_steps = SEQ // tt
    cm = jnp.asarray(np.tile(_gla_coeff_matrix(), (1, 3)), dtype=BF16)
    rows = lambda b, h, c: b * nt_steps + c
    wk, wv = GLA_HP * GLA_HDK, GLA_HP * GLA_HDV
    n_groups = GLA_HEADS // GLA_HP
    o, st = pl.pallas_call(
        functools.partial(_gla_prompt_body, n_sub=n_sub),
        grid=(N_PROMPT_SEQ, n_groups, nt_steps),
        in_specs=[
            pl.BlockSpec((tt, wk), lambda b, h, c: (rows(b, h, c), h)),
            pl.BlockSpec((tt, wk), lambda b, h, c: (rows(b, h, c), n_groups + h)),
            pl.BlockSpec((tt, wv), lambda b, h, c: (rows(b, h, c), n_groups + h)),
            pl.BlockSpec((tt, wv), lambda b, h, c: (rows(b, h, c), 2 * n_groups + h)),
            pl.BlockSpec((tt, 128), lambda b, h, c: (rows(b, h, c), 0)),
            pl.BlockSpec((128, wk), lambda b, h, c: (0, h)),
            pl.BlockSpec((1, wk), lambda b, h, c: (0, h)),
            pl.BlockSpec((1, wv), lambda b, h, c: (0, h)),
            pl.BlockSpec(cm.shape, lambda b, h, c: (0, 0)),
        ],
        out_specs=[
            pl.BlockSpec((tt, wv), lambda b, h, c: (rows(b, h, c), h)),
            pl.BlockSpec((1, GLA_HP, GLA_HDV, GLA_HDK), lambda b, h, c: (b, h, 0, 0)),
        ],
        out_shape=[
            jax.ShapeDtypeStruct((N_PROMPT, BRANCH), BF16),
            jax.ShapeDtypeStruct((N_PROMPT_SEQ, GLA_HEADS, GLA_HDV, GLA_HDK), F32),
        ],
        scratch_shapes=[pltpu.VMEM((GLA_HP, GLA_HDV, GLA_HDK), F32)],
        compiler_params=_cp(("arbitrary", "arbitrary", "arbitrary")),
        name="gla_prompt",
    )(qkvr, qkvr, qkvr, qkvr, a_low, wgu, bgu, wn, cm)
    return o, st


GLA_SB = 16


def _gla_sample_body(q_ref, k_ref, v_ref, r_ref, a_ref, wgu_ref, bgu_ref, wn_ref, s_ref, o_ref, so_ref):
    q = q_ref[...] * (GLA_HDK ** -0.5)
    k = k_ref[...]
    v = v_ref[...]
    z = jnp.dot(a_ref[...], wgu_ref[...], preferred_element_type=F32, precision=HIGHEST) + bgu_ref[...]
    eg = jnp.exp(_log_sigmoid(z) * (1.0 / GLA_TAU))
    qe = (q * eg).astype(BF16)
    eg_t = eg.T
    k_t = k.T
    rows = []
    for n in range(GLA_SB):
        s0 = s_ref[n, 0]
        rows.append(jnp.dot(qe, s0.astype(BF16), preferred_element_type=F32)[n:n + 1])
        so_ref[n, 0] = s0 * eg_t[:, n:n + 1] + k_t[:, n:n + 1] * v[n:n + 1, :]
    o = jnp.concatenate(rows, axis=0) + jnp.sum(q * k, axis=1, keepdims=True) * v
    o_ref[...] = _gla_out_norm(o, r_ref[...], wn_ref[...]).astype(o_ref.dtype)


def _gla_sample(qkvr, a_low, wgu, bgu, wn, state):
    r0 = N_PROMPT // GLA_SB
    o, st = pl.pallas_call(
        _gla_sample_body,
        grid=(GLA_HEADS, N_SAMPLE // GLA_SB),
        in_specs=[
            pl.BlockSpec((GLA_SB, GLA_HDK), lambda h, i: (r0 + i, h)),
            pl.BlockSpec((GLA_SB, GLA_HDK), lambda h, i: (r0 + i, GLA_HEADS + h)),
            pl.BlockSpec((GLA_SB, GLA_HDV), lambda h, i: (r0 + i, GLA_HEADS + h)),
            pl.BlockSpec((GLA_SB, GLA_HDV), lambda h, i: (r0 + i, 2 * GLA_HEADS + h)),
            pl.BlockSpec((GLA_SB, 128), lambda h, i: (r0 + i, 0)),
            pl.BlockSpec((128, GLA_HDK), lambda h, i: (0, h)),
            pl.BlockSpec((1, GLA_HDK), lambda h, i: (0, h)),
            pl.BlockSpec((1, GLA_HDV), lambda h, i: (0, h)),
            pl.BlockSpec((GLA_SB, 1, GLA_HDK, GLA_HDV), lambda h, i: (i, h, 0, 0)),
        ],
        out_specs=[
            pl.BlockSpec((GLA_SB, GLA_HDV), lambda h, i: (i, h)),
            pl.BlockSpec((GLA_SB, 1, GLA_HDK, GLA_HDV), lambda h, i: (i, h, 0, 0)),
        ],
        out_shape=[
            jax.ShapeDtypeStruct((N_SAMPLE, BRANCH), BF16),
            jax.ShapeDtypeStruct(state.shape, F32),
        ],
        compiler_params=_cp(("arbitrary", "arbitrary")),
        name="gla_sample",
    )(qkvr, qkvr, qkvr, qkvr, a_low, wgu, bgu, wn, state)
    return o, st


def _s5_weights(a_re, a_im, b_re, b_im, c_re, c_im, d, log_dt):
    L = S5_CHUNK
    lam_re = jnp.minimum(a_re, -1e-4)
    lam_im = a_im
    dt = jnp.exp(log_dt)[:, None]
    kk = jnp.arange(L + 1, dtype=F32)[:, None, None]
    pow_re = jnp.exp(lam_re * dt * kk) * jnp.cos(lam_im * dt * kk)
    pow_im = jnp.exp(lam_re * dt * kk) * jnp.sin(lam_im * dt * kk)
    lbar_re, lbar_im = pow_re[1], pow_im[1]
    den = lam_re * lam_re + lam_im * lam_im
    f_re = ((lbar_re - 1.0) * lam_re + lbar_im * lam_im) / den
    f_im = (lbar_im * lam_re - (lbar_re - 1.0) * lam_im) / den
    bb_re = f_re[..., None] * b_re - f_im[..., None] * b_im
    bb_im = f_re[..., None] * b_im + f_im[..., None] * b_re
    nb, ng = S5_BLOCKS, S5_LANE_GROUPS
    bb_re_t = jnp.swapaxes(bb_re, 1, 2)
    bb_im_t = jnp.swapaxes(bb_im, 1, 2)
    lb_re = (pow_re[:L, :, None, :] * bb_re_t - pow_im[:L, :, None, :] * bb_im_t).reshape(L, nb, 128, S5_STATE)
    lb_im = (pow_re[:L, :, None, :] * bb_im_t + pow_im[:L, :, None, :] * bb_re_t).reshape(L, nb, 128, S5_STATE)
    cl_re = (c_re[None] * pow_re[:, :, None, :] - c_im[None] * pow_im[:, :, None, :]).reshape(L + 1, nb, 128, S5_STATE)
    cl_im = (c_re[None] * pow_im[:, :, None, :] + c_im[None] * pow_re[:, :, None, :]).reshape(L + 1, nb, 128, S5_STATE)
    lb = jnp.concatenate([lb_re, lb_im], axis=-1)
    cl = jnp.concatenate([cl_re, -cl_im], axis=-1)
    mult = (L * jnp.arange(1, 9, dtype=F32))[:, None, None]
    sc_re = jnp.exp(lam_re * dt * mult) * jnp.cos(lam_im * dt * mult)
    sc_im = jnp.exp(lam_re * dt * mult) * jnp.sin(lam_im * dt * mult)

    def state_lanes(x):
        lead = x.shape[:-2]
        return jnp.moveaxis(x.reshape(lead + (nb, ng * S5_STATE)), -2, 0)

    scan_mult = jnp.concatenate([state_lanes(sc_re), state_lanes(sc_im)], axis=-1)
    lbar1 = jnp.concatenate([state_lanes(lbar_re[None]), state_lanes(lbar_im[None])], axis=-1)
    dvec = jnp.tile(d.reshape(nb, 1, 128), (1, 1, L))
    return lb, cl, scan_mult, lbar1, dvec


def _s5_expand(src):
    ng, p = S5_LANE_GROUPS, S5_STATE
    lane = lax.broadcasted_iota(jnp.int32, src.shape, 1)
    other = pltpu.roll(src, p, 1)
    re2 = jnp.where(lane < p, src, other)
    im2 = jnp.where(lane < p, other, src)
    full = jnp.concatenate([re2] * (ng // 2) + [im2] * (ng // 2), axis=1)
    row = lax.broadcasted_iota(jnp.int32, full.shape, 0)
    col = lax.broadcasted_iota(jnp.int32, full.shape, 1)
    same_group = (row >> 4) == ((col >> 6) & (ng - 1))
    return jnp.where(same_group, full, 0.0)


def _s5_lag_kernel(zpow, cpow0):
    z_hi = zpow.astype(BF16)
    z_lo = (zpow - z_hi.astype(F32)).astype(BF16)
    c_hi = cpow0.astype(BF16)
    c_lo = (cpow0 - c_hi.astype(F32)).astype(BF16)
    dot = functools.partial(lax.dot_general, dimension_numbers=_NT, preferred_element_type=F32)
    return dot(z_hi, c_hi) + dot(z_hi, c_lo) + dot(z_lo, c_hi)


def _s5_build_weights(lb_ref, cl_ref, t_scr, wz_scr, wc_scr):
    L = S5_CHUNK
    cpow0 = _s5_expand(cl_ref[0, 0])
    t_scr[...] = jnp.zeros_like(t_scr)
    for k in range(L):
        zpow = _s5_expand(lb_ref[k, 0])
        wz_scr[(L - 1 - k) * 128:(L - k) * 128, :] = zpow.astype(BF16)
        wc_scr[k * 128:(k + 1) * 128, :] = _s5_expand(cl_ref[k + 1, 0]).astype(BF16)
        bd = _s5_lag_kernel(zpow, cpow0).astype(BF16)
        for sp in range(L - k):
            t_scr[sp * 128:(sp + 1) * 128, (sp + k) * 128:(sp + k + 1) * 128] = bd


def _s5_body(u_ref, lb_ref, cl_ref, sm_ref, l1_ref, d_ref, hre_ref, him_ref,
             y_ref, fre_ref, fim_ref, sre_ref, sim_ref, t_ref, wz_ref, wc_ref):
    step = pl.program_id(1)

    @pl.when(step == 0)
    def _():
        _s5_build_weights(lb_ref, cl_ref, t_ref, wz_ref, wc_ref)

    @pl.when(step < N_PROMPT_SEQ)
    def _():
        _s5_prompt_step(u_ref, sm_ref, d_ref, y_ref, fre_ref, fim_ref, t_ref, wz_ref, wc_ref)

    @pl.when(step == N_PROMPT_SEQ)
    def _():
        _s5_sample_step(u_ref, l1_ref, d_ref, hre_ref, him_ref, y_ref, sre_ref, sim_ref, t_ref, wz_ref, wc_ref)


def _s5_prompt_step(u_ref, sm_ref, d_ref, y_ref, fre_ref, fim_ref, t_ref, wz_ref, wc_ref):
    L = S5_CHUNK
    n_rows = SEQ // L
    ns = S5_LANE_GROUPS * S5_STATE
    v = jnp.concatenate([u_ref[pl.ds(s, n_rows, stride=L), :] for s in range(L)], axis=1)
    vb = v.astype(BF16)
    z = jnp.dot(vb, wz_ref[...], preferred_element_type=F32)
    hr, hi = z[:, :ns], z[:, ns:]
    pos = lax.broadcasted_iota(jnp.int32, (n_rows, ns), 0)
    sm = sm_ref[0]
    for d in range(3):
        sh = 1 << d
        ar, ai = sm[sh - 1:sh, :ns], sm[sh - 1:sh, ns:]
        keep = (pos & 7) >= sh
        pr = jnp.where(keep, pltpu.roll(hr, sh, 0), 0.0)
        pi = jnp.where(keep, pltpu.roll(hi, sh, 0), 0.0)
        hr, hi = hr + ar * pr - ai * pi, hi + ar * pi + ai * pr
    gr, gi = sm[:, :ns], sm[:, ns:]
    out_r, out_i = [hr[0:8]], [hi[0:8]]
    for grp in range(1, n_rows // 8):
        cr = jnp.broadcast_to(out_r[-1][7:8], (8, ns))
        ci = jnp.broadcast_to(out_i[-1][7:8], (8, ns))
        out_r.append(hr[grp * 8:(grp + 1) * 8] + gr * cr - gi * ci)
        out_i.append(hi[grp * 8:(grp + 1) * 8] + gr * ci + gi * cr)
    hr = jnp.concatenate(out_r, axis=0)
    hi = jnp.concatenate(out_i, axis=0)
    fre_ref[0] = hr[n_rows - 1:n_rows]
    fim_ref[0] = hi[n_rows - 1:n_rows]
    first = pos >= 1
    h_prev = jnp.concatenate([jnp.where(first, pltpu.roll(hr, 1, 0), 0.0),
                              jnp.where(first, pltpu.roll(hi, 1, 0), 0.0)], axis=1)
    y = (jnp.dot(vb, t_ref[...], preferred_element_type=F32)
         + lax.dot_general(h_prev.astype(BF16), wc_ref[...], _NT, preferred_element_type=F32)
         + d_ref[0] * v)
    y = _gelu_tanh(y)
    for s in range(L):
        y_ref[pl.ds(s, n_rows, stride=L), :] = y[:, s * 128:(s + 1) * 128]


def _s5_sample_step(u_ref, l1_ref, d_ref, hre_ref, him_ref, y_ref, sre_ref, sim_ref, t_ref, wz_ref, wc_ref):
    L = S5_CHUNK
    ns = S5_LANE_GROUPS * S5_STATE
    bbar = wz_ref[(L - 1) * 128:L * 128, :]
    k0 = t_ref[0:128, 0:128]
    wc0 = wc_ref[0:128, :]
    us = u_ref[0:N_SAMPLE, :]
    usb = us.astype(BF16)
    h0r, h0i = hre_ref[...], him_ref[...]
    l1 = l1_ref[0]
    bu = jnp.dot(usb, bbar, preferred_element_type=F32)
    sre_ref[...] = l1[:, :ns] * h0r - l1[:, ns:] * h0i + bu[:, :ns]
    sim_ref[...] = l1[:, :ns] * h0i + l1[:, ns:] * h0r + bu[:, ns:]
    h0 = jnp.concatenate([h0r, h0i], axis=1).astype(BF16)
    ys = (jnp.dot(usb, k0, preferred_element_type=F32)
          + lax.dot_general(h0, wc0, _NT, preferred_element_type=F32)
          + d_ref[0, :, 0:128] * us)
    y_ref[0:N_SAMPLE, :] = _gelu_tanh(ys)


def _s5(u, weights, st_re, st_im):
    lb, cl, scan_mult, lbar1, dvec = weights
    L = S5_CHUNK
    ns = S5_LANE_GROUPS * S5_STATE
    last_seq = N_PROMPT_SEQ - 1
    wblk = lambda a: pl.BlockSpec((1,) + a.shape[1:], lambda j, b: (j, 0, 0))
    pblk = lambda a: pl.BlockSpec((a.shape[0], 1, 128, 128), lambda j, b: (0, j, 0, 0))
    sblk = pl.BlockSpec((N_SAMPLE, ns), lambda j, b: (0, j))
    fblk = pl.BlockSpec((1, 1, ns), lambda j, b: (jnp.minimum(b, last_seq), 0, j))
    mat = pltpu.VMEM((L * 128, L * 128), BF16)
    return pl.pallas_call(
        _s5_body,
        grid=(S5_BLOCKS, N_PROMPT_SEQ + 1),
        in_specs=[
            pl.BlockSpec((SEQ, 128), lambda j, b: (b, j)),
            pblk(lb), pblk(cl), wblk(scan_mult), wblk(lbar1), wblk(dvec), sblk, sblk,
        ],
        out_specs=[pl.BlockSpec((SEQ, 128), lambda j, b: (b, j)), fblk, fblk, sblk, sblk],
        out_shape=[
            jax.ShapeDtypeStruct((N_TOK, BRANCH), F32),
            jax.ShapeDtypeStruct((N_PROMPT_SEQ, 1, S5_GROUPS * S5_STATE), F32),
            jax.ShapeDtypeStruct((N_PROMPT_SEQ, 1, S5_GROUPS * S5_STATE), F32),
            jax.ShapeDtypeStruct((N_SAMPLE, S5_GROUPS * S5_STATE), F32),
            jax.ShapeDtypeStruct((N_SAMPLE, S5_GROUPS * S5_STATE), F32),
        ],
        scratch_shapes=[mat, mat, mat],
        compiler_params=_cp(("arbitrary", "arbitrary")),
        name="s5",
    )(u, lb, cl, scan_mult, lbar1, dvec, st_re, st_im)


def _merge_body(o_ref, z_ref, w0_ref, w1_ref, g0_ref, g1_ref, out_ref):
    p0 = jnp.dot(o_ref[...], w0_ref[0].astype(BF16), preferred_element_type=F32)
    p1 = jnp.dot(z_ref[...], w1_ref[0].astype(BF16), preferred_element_type=F32)
    out_ref[...] = (g0_ref[...] * p0 + g1_ref[...] * p1).astype(out_ref.dtype)


def _merge(o, z, w_branch, gates):
    ncb = D_MODEL // TN
    return pl.pallas_call(
        _merge_body,
        grid=(N_TOK // TM, ncb),
        in_specs=[
            pl.BlockSpec((TM, BRANCH), lambda i, j: (i, 0)),
            pl.BlockSpec((TM, BRANCH), lambda i, j: (i, 0)),
            pl.BlockSpec((1, BRANCH, TN), lambda i, j: (0, 0, j)),
            pl.BlockSpec((1, BRANCH, TN), lambda i, j: (1, 0, j)),
            pl.BlockSpec((TM, TN), lambda i, j: (i, j)),
            pl.BlockSpec((TM, TN), lambda i, j: (i, ncb + j)),
        ],
        out_specs=pl.BlockSpec((TM, TN), lambda i, j: (i, j)),
        out_shape=jax.ShapeDtypeStruct((N_TOK, D_MODEL), BF16),
        compiler_params=_cp(("arbitrary", "arbitrary")),
        name="merge",
    )(o, z, w_branch, w_branch, gates, gates)


LN_TM = 128
LN_PROMPT_TILES = N_PROMPT // LN_TM
HALF = D_MODEL // 2


def _pack_bf16_pair(x):
    bits = pltpu.bitcast(x.astype(BF16).astype(F32), jnp.uint32)
    return (bits[:, :HALF] & jnp.uint32(0xFFFF0000)) | (bits[:, HALF:] >> 16)


SLAB = HALF // 128


def _slab_store(ref, row0, x):
    for c in range(SLAB):
        ref[pl.ds(row0 * SLAB + c, x.shape[0], stride=SLAB), :] = x[:, c * 128:(c + 1) * 128]


def _slab_load(ref, row0, n):
    return jnp.concatenate([ref[pl.ds(row0 * SLAB + c, n, stride=SLAB), :] for c in range(SLAB)], axis=1)


def _slab_rows(first, n):
    return pl.ds(pl.multiple_of(first * SLAB, SLAB), n * SLAB)


def _unpack_bf16_pair(w):
    hi = pltpu.bitcast(w & jnp.uint32(0xFFFF0000), F32)
    lo = pltpu.bitcast(w << 16, F32)
    return hi, lo


def _stack_bf16_body(xp_ref, xs_ref, o_ref):
    o_ref[...] = jnp.where(pl.program_id(0) < LN_PROMPT_TILES, xp_ref[...], xs_ref[...]).astype(BF16)


def _stack_bf16(x_p, x_s):
    return pl.pallas_call(
        _stack_bf16_body,
        grid=(N_TOK // LN_TM,),
        in_specs=[
            pl.BlockSpec((LN_TM, D_MODEL), lambda i: (jnp.minimum(i, LN_PROMPT_TILES - 1), 0)),
            pl.BlockSpec((LN_TM, D_MODEL), lambda i: (jnp.maximum(i - LN_PROMPT_TILES, 0), 0)),
        ],
        out_specs=pl.BlockSpec((LN_TM, D_MODEL), lambda i: (i, 0)),
        out_shape=jax.ShapeDtypeStruct((N_TOK, D_MODEL), BF16),
        compiler_params=_cp(("arbitrary",)),
        name="stack_bf16",
    )(x_p, x_s)


def _ln_router_body(m_ref, xp_ref, xs_ref, g_ref, b_ref, wr_ref, br_ref, o_ref, op_ref, ids_ref, wts_ref):
    x = jnp.where(pl.program_id(0) < LN_PROMPT_TILES, xp_ref[...], xs_ref[...])
    x1 = _layer_norm(ALPHA * x + m_ref[...]) * g_ref[...] + b_ref[...]
    o_ref[...] = x1
    _slab_store(op_ref, 0, _pack_bf16_pair(x1))
    x_hi = x1.astype(BF16)
    x_mid = (x1 - x_hi.astype(F32)).astype(BF16)
    wr = wr_ref[...]
    both = jnp.dot(x_hi, wr, preferred_element_type=F32)
    logits = (both[:, :128] + both[:, 128:]
              + jnp.dot(x_mid, wr[:, :128], preferred_element_type=F32) + br_ref[...])
    _route(logits, ids_ref, wts_ref)


def _ln_router(mixed, x_p, x_s, g, b, w_r, b_r):
    w_hi = w_r.astype(BF16)
    w_mid = (w_r - w_hi.astype(F32)).astype(BF16)
    wr = jnp.concatenate([w_hi, w_mid], axis=1)
    row_blk = pl.BlockSpec((LN_TM, 128), lambda i: (i, 0))
    return pl.pallas_call(
        _ln_router_body,
        grid=(N_TOK // LN_TM,),
        in_specs=[
            pl.BlockSpec((LN_TM, D_MODEL), lambda i: (i, 0)),
            pl.BlockSpec((LN_TM, D_MODEL), lambda i: (jnp.minimum(i, LN_PROMPT_TILES - 1), 0)),
            pl.BlockSpec((LN_TM, D_MODEL), lambda i: (jnp.maximum(i - LN_PROMPT_TILES, 0), 0)),
            pl.BlockSpec((1, D_MODEL), lambda i: (0, 0)),
            pl.BlockSpec((1, D_MODEL), lambda i: (0, 0)),
            pl.BlockSpec((D_MODEL, 256), lambda i: (0, 0)),
            pl.BlockSpec((1, 128), lambda i: (0, 0)),
        ],
        out_specs=[pl.BlockSpec((LN_TM, D_MODEL), lambda i: (i, 0)),
                   pl.BlockSpec((LN_TM * SLAB, 128), lambda i: (i, 0)), row_blk, row_blk],
        out_shape=[jax.ShapeDtypeStruct((N_TOK, D_MODEL), F32),
                   jax.ShapeDtypeStruct((N_TOK * SLAB, 128), jnp.uint32),
                   jax.ShapeDtypeStruct((N_TOK, 128), jnp.int32),
                   jax.ShapeDtypeStruct((N_TOK, 128), F32)],
        compiler_params=_cp(("arbitrary",)),
        name="ln1_router",
    )(mixed, x_p, x_s, g, b, wr, b_r)


def _route(logits, ids_ref, wts_ref):
    lane = lax.broadcasted_iota(jnp.int32, logits.shape, 1)
    neg = -jnp.inf
    big = 1 << 20
    gl = jnp.where(lane < 8, logits, neg)
    gmax = jnp.max(gl, axis=1, keepdims=True)
    gidx = jnp.min(jnp.where(gl == gmax, lane, big), axis=1, keepdims=True)
    gprob = 1.0 / jnp.sum(jnp.exp(gl - gmax), axis=1, keepdims=True)
    in_group = (lane >= 8) & (lane < 8 + N_EXPERTS) & (((lane - 8) >> 3) == gidx)
    el = jnp.where(in_group, logits, neg)
    v1 = jnp.max(el, axis=1, keepdims=True)
    i1 = jnp.min(jnp.where(el == v1, lane, big), axis=1, keepdims=True)
    el2 = jnp.where(lane == i1, neg, el)
    v2 = jnp.max(el2, axis=1, keepdims=True)
    i2 = jnp.min(jnp.where(el2 == v2, lane, big), axis=1, keepdims=True)
    e2 = jnp.exp(v2 - v1)
    w1 = gprob / (1.0 + e2)
    w2 = gprob * e2 / (1.0 + e2)
    ids_ref[...] = jnp.where(lane == 0, i1 - 8, jnp.where(lane == 1, i2 - 8, 0))
    wts_ref[...] = jnp.where(lane == 0, w1, jnp.where(lane == 1, w2, 0.0))


N_ASSIGN = 2 * N_TOK
MOE_NBLOCKS = (N_ASSIGN + N_EXPERTS * (MOE_BLOCK - 1) + MOE_BLOCK - 1) // MOE_BLOCK
MOE_ROWS = MOE_NBLOCKS * MOE_BLOCK
MOE_HC = MOE_HIDDEN // MOE_HSPLIT


def _row_copy(src_hbm, row, dst, dst_row, sem):
    return pltpu.make_async_copy(src_hbm.at[_slab_rows(row, 1), :], dst.at[_slab_rows(dst_row, 1), :], sem)


MOE_MACRO = 4
MOE_NMACRO = N_ASSIGN // (MOE_MACRO * MOE_BLOCK) + N_EXPERTS + 1
MOE_DCOLS = 1024


def _block_copy(src, t, o_hbm, blk, sem):
    return pltpu.make_async_copy(src.at[_slab_rows(t * MOE_BLOCK, MOE_BLOCK), :],
                                 o_hbm.at[_slab_rows(blk * MOE_BLOCK, MOE_BLOCK), :], sem)


def _moe_body(mexp_ref, mstart_ref, mnsub_ref, tok_ref, x_hbm, wg_ref, wu_hbm, wd_hbm, o_hbm,
              stage, xb, acc, ostage, wubuf, wdbuf, gsem, osem, wsem):
    m = pl.program_id(0)
    h = pl.program_id(1)
    nsub = mnsub_ref[m]
    start = mstart_ref[m]
    last_m = pl.num_programs(0) - 1
    last_h = pl.num_programs(1) - 1

    def wait_blocks(n):
        def done(t, c):
            _block_copy(ostage, t, o_hbm, 0, osem).wait()
            return c

        lax.fori_loop(0, n, done, 0)

    half_rows = D_MODEL // 2

    def w_copies(step, hh):
        e = mexp_ref[step]
        cols = pl.ds(pl.multiple_of(hh * MOE_HC, MOE_HC), MOE_HC)
        return (
            (pltpu.make_async_copy(wu_hbm.at[e, pl.ds(0, half_rows), cols],
                                   wubuf.at[hh, pl.ds(0, half_rows), :], wsem.at[0, hh]), 0),
            (pltpu.make_async_copy(wu_hbm.at[e, pl.ds(half_rows, half_rows), cols],
                                   wubuf.at[hh, pl.ds(half_rows, half_rows), :], wsem.at[1, hh]), 1),
            (pltpu.make_async_copy(wd_hbm.at[e, cols, :], wdbuf.at[hh], wsem.at[2, hh]), 1),
        )

    def w_start(step, hh):
        for cp, prio in w_copies(step, hh):
            cp.start(priority=prio)

    def w_wait(step, hh):
        for cp, _ in w_copies(step, hh):
            cp.wait()

    @pl.when((m == 0) & (h == 0) & (nsub > 0))
    def _():
        w_start(0, 0)

    @pl.when((h == 0) & (nsub > 0))
    def _():
        w_start(m, 1)

    nxt = jnp.minimum(m + 1, last_m)

    @pl.when((h == last_h) & (m < last_m) & (mnsub_ref[nxt] > 0))
    def _():
        w_start(nxt, 0)

    def gather_start(step):
        def sub(t, c):
            base = (mstart_ref[step] + t) * MOE_BLOCK
            off = pl.multiple_of(t * MOE_BLOCK, MOE_BLOCK)
            for r in range(MOE_BLOCK):
                _row_copy(x_hbm, tok_ref[base + r], stage, off + r, gsem).start()
            return c

        lax.fori_loop(0, mnsub_ref[step], sub, 0)

    @pl.when((h == 0) & (m == 0))
    def _():
        gather_start(0)

    @pl.when((h == 0) & (nsub > 0))
    def _():
        def landed(t, c):
            pltpu.make_async_copy(x_hbm.at[_slab_rows(0, MOE_BLOCK), :],
                                  stage.at[_slab_rows(t * MOE_BLOCK, MOE_BLOCK), :], gsem).wait()
            return c

        lax.fori_loop(0, nsub, landed, 0)

        def sub(t, c):
            rows = pl.ds(pl.multiple_of(t * MOE_BLOCK, MOE_BLOCK), MOE_BLOCK)
            hi, lo = _unpack_bf16_pair(_slab_load(stage, t * MOE_BLOCK, MOE_BLOCK))
            xb[rows, 0:HALF] = hi.astype(BF16)
            xb[rows, HALF:D_MODEL] = lo.astype(BF16)
            return c

        lax.fori_loop(0, nsub, sub, 0)

    @pl.when((h == last_h) & (m < last_m))
    def _():
        gather_start(jnp.minimum(m + 1, last_m))

    for ns in range(1, MOE_MACRO + 1):
        @pl.when(nsub == ns)
        def _(ns=ns):
            rows = ns * MOE_BLOCK

            @pl.when(h == 0)
            def _():
                acc[0:rows, :] = jnp.zeros((rows, D_MODEL), F32)

            x = xb[0:rows, :]
            hg = jnp.dot(x, wg_ref[0].astype(BF16), preferred_element_type=F32)
            w_wait(m, h)
            hu = jnp.dot(x, wubuf[h].astype(BF16), preferred_element_type=F32)
            hh = (hg * _sigmoid(hg) * hu).astype(BF16)
            wd = wdbuf[h].astype(BF16)
            for cc in range(D_MODEL // MOE_DCOLS):
                cols = slice(cc * MOE_DCOLS, (cc + 1) * MOE_DCOLS)
                acc[0:rows, cols] += jnp.dot(hh, wd[:, cols], preferred_element_type=F32)

    @pl.when((h == last_h) & (m > 0))
    def _():
        wait_blocks(mnsub_ref[jnp.maximum(m - 1, 0)])

    @pl.when((h == last_h) & (nsub > 0))
    def _():
        def put(t, c):
            rows = pl.ds(pl.multiple_of(t * MOE_BLOCK, MOE_BLOCK), MOE_BLOCK)
            _slab_store(ostage, t * MOE_BLOCK, _pack_bf16_pair(acc[rows, :]))
            _block_copy(ostage, t, o_hbm, start + t, osem).start()
            return c

        lax.fori_loop(0, nsub, put, 0)

        @pl.when(m == last_m)
        def _():
            wait_blocks(nsub)

    @pl.when((h == last_h) & (nsub == 0))
    def _():
        ostage[0:MOE_BLOCK * SLAB, :] = jnp.zeros((MOE_BLOCK * SLAB, 128), jnp.uint32)
        for t in range(MOE_MACRO):
            @pl.when(start + t < MOE_NBLOCKS)
            def _(t=t):
                cp = _block_copy(ostage, 0, o_hbm, start + t, osem)
                cp.start()
                cp.wait()


def _moe_experts(x1p, wg, wu, wd, mexp, mstart, mnsub, row_tok):
    grid_spec = pltpu.PrefetchScalarGridSpec(
        num_scalar_prefetch=4,
        grid=(MOE_NMACRO, MOE_HSPLIT),
        in_specs=[
            pl.BlockSpec(memory_space=pl.ANY),
            pl.BlockSpec((1, D_MODEL, MOE_HC), lambda m, h, me, ms, mn, rt: (me[m], 0, h)),
            pl.BlockSpec(memory_space=pl.ANY),
            pl.BlockSpec(memory_space=pl.ANY),
        ],
        out_specs=pl.BlockSpec(memory_space=pl.ANY),
        scratch_shapes=[pltpu.VMEM((MOE_MACRO * MOE_BLOCK * SLAB, 128), jnp.uint32),
                        pltpu.VMEM((MOE_MACRO * MOE_BLOCK, D_MODEL), BF16),
                        pltpu.VMEM((MOE_MACRO * MOE_BLOCK, D_MODEL), F32),
                        pltpu.VMEM((MOE_MACRO * MOE_BLOCK * SLAB, 128), jnp.uint32),
                        pltpu.VMEM((MOE_HSPLIT, D_MODEL, MOE_HC), F32),
                        pltpu.VMEM((MOE_HSPLIT, MOE_HC, D_MODEL), F32),
                        pltpu.SemaphoreType.DMA(()), pltpu.SemaphoreType.DMA(()),
                        pltpu.SemaphoreType.DMA((3, MOE_HSPLIT))],
    )
    return pl.pallas_call(
        _moe_body,
        grid_spec=grid_spec,
        out_shape=jax.ShapeDtypeStruct((MOE_ROWS * SLAB, 128), jnp.uint32),
        compiler_params=_cp(("arbitrary", "arbitrary")),
        name="moe_experts",
    )(mexp, mstart, mnsub, row_tok, x1p, wg, wu, wd)


CMB_TM = 128
CMB_PROMPT_TILES = N_PROMPT // CMB_TM


def _combine_body(pos_ref, eo_hbm, wts_ref, x1_ref, g_ref, b_ref, yp_ref, ys_ref, buf, sem):
    i = pl.program_id(0)
    slot = i % 2

    def fetch(tile, s):
        base = 2 * tile * CMB_TM
        for r in range(CMB_TM):
            _row_copy(eo_hbm, pos_ref[base + 2 * r], buf.at[s, 0], r, sem.at[s]).start()
            _row_copy(eo_hbm, pos_ref[base + 2 * r + 1], buf.at[s, 1], r, sem.at[s]).start()

    @pl.when(i == 0)
    def _():
        fetch(0, 0)

    for s in range(2):
        @pl.when((i + 1 < pl.num_programs(0)) & (slot == s))
        def _(s=s):
            fetch(i + 1, 1 - s)

    for k in range(2):
        pltpu.make_async_copy(eo_hbm.at[_slab_rows(0, CMB_TM), :], buf.at[slot, k], sem.at[slot]).wait()
    w = wts_ref[...]
    hi0, lo0 = _unpack_bf16_pair(_slab_load(buf.at[slot, 0], 0, CMB_TM))
    hi1, lo1 = _unpack_bf16_pair(_slab_load(buf.at[slot, 1], 0, CMB_TM))
    w0, w1 = w[:, 0:1], w[:, 1:2]
    y = jnp.concatenate([w0 * hi0 + w1 * hi1, w0 * lo0 + w1 * lo1], axis=1)
    x2 = _layer_norm(ALPHA * x1_ref[...] + y) * g_ref[...] + b_ref[...]

    @pl.when(i < CMB_PROMPT_TILES)
    def _():
        yp_ref[...] = x2

    @pl.when(i >= CMB_PROMPT_TILES)
    def _():
        ys_ref[...] = x2


def _combine(pos, eo, wts, x1, g, b):
    grid_spec = pltpu.PrefetchScalarGridSpec(
        num_scalar_prefetch=1,
        grid=(N_TOK // CMB_TM,),
        in_specs=[
            pl.BlockSpec(memory_space=pl.ANY),
            pl.BlockSpec((CMB_TM, 128), lambda i, p: (i, 0)),
            pl.BlockSpec((CMB_TM, D_MODEL), lambda i, p: (i, 0)),
            pl.BlockSpec((1, D_MODEL), lambda i, p: (0, 0)),
            pl.BlockSpec((1, D_MODEL), lambda i, p: (0, 0)),
        ],
        out_specs=[
            pl.BlockSpec((CMB_TM, D_MODEL), lambda i, p: (jnp.minimum(i, CMB_PROMPT_TILES - 1), 0)),
            pl.BlockSpec((CMB_TM, D_MODEL), lambda i, p: (0, 0)),
        ],
        scratch_shapes=[pltpu.VMEM((2, 2, CMB_TM * SLAB, 128), jnp.uint32), pltpu.SemaphoreType.DMA((2,))],
    )
    return pl.pallas_call(
        _combine_body,
        grid_spec=grid_spec,
        out_shape=[jax.ShapeDtypeStruct((N_PROMPT, D_MODEL), F32),
                   jax.ShapeDtypeStruct((N_SAMPLE, D_MODEL), F32)],
        compiler_params=_cp(("arbitrary",)),
        name="combine_ln2",
    )(pos, eo, wts, x1, g, b)


def _route_positions(ids):
    eid = ids[:, :2].reshape(-1)
    onehot = (eid[:, None] == jnp.arange(N_EXPERTS, dtype=jnp.int32)[None, :]).astype(jnp.int32)
    csum = jnp.cumsum(onehot, axis=0)
    rank = jnp.take_along_axis(csum, eid[:, None], axis=1)[:, 0] - 1
    counts = csum[-1]
    nblk_e = (counts + MOE_BLOCK - 1) // MOE_BLOCK
    bend = jnp.cumsum(nblk_e)
    bstart = bend - nblk_e
    pos = bstart[eid] * MOE_BLOCK + rank
    row_tok = jnp.zeros((MOE_ROWS,), jnp.int32).at[pos].set(jnp.arange(N_ASSIGN, dtype=jnp.int32) // 2)
    nstep_e = (nblk_e + MOE_MACRO - 1) // MOE_MACRO
    send = jnp.cumsum(nstep_e)
    n_steps, n_blocks = send[-1], bend[-1]
    m = jnp.arange(MOE_NMACRO, dtype=jnp.int32)
    e_of_m = jnp.minimum(jnp.searchsorted(send, m, side='right'), N_EXPERTS - 1).astype(jnp.int32)
    local = m - (send - nstep_e)[e_of_m]
    valid = m < n_steps
    mexp = jnp.where(valid, e_of_m, e_of_m[jnp.maximum(n_steps - 1, 0)])
    mstart = jnp.where(valid, bstart[e_of_m] + MOE_MACRO * local, n_blocks + MOE_MACRO * (m - n_steps))
    mnsub = jnp.where(valid, jnp.clip(nblk_e[e_of_m] - MOE_MACRO * local, 0, MOE_MACRO), 0)
    return (pos.astype(jnp.int32), row_tok, mexp.astype(jnp.int32), mstart.astype(jnp.int32),
            mnsub.astype(jnp.int32))


def kernel(x_prompt, x_sample, state_gla, state_s5_re, state_s5_im, w_in, w_gla_gate_up, b_gla_gate_up, w_gla_norm, s5_a_re, s5_a_im, s5_b_re, s5_b_im, s5_c_re, s5_c_im, s5_d, s5_log_dt, w_s5_glu, b_s5_glu, w_branch, w_out, ln1_g, ln1_b, w_router_group, b_router_group, w_router_expert, b_router_expert, w_moe_gate, w_moe_up, w_moe_down, ln2_g, ln2_b):
    x_p = x_prompt.reshape(N_PROMPT, D_MODEL)
    x_s = x_sample.reshape(N_SAMPLE, D_MODEL)
    x_bf = _stack_bf16(x_p, x_s)

    w_in_t = w_in.T
    qkvr = _mm_t(x_bf, w_in_t, row0=0, ncols=COL_A, tn=TN, out_dtype=F32, name="proj_qkvr")
    a_low = _mm_t(x_bf, w_in_t, row0=COL_A, ncols=128, tn=128, out_dtype=F32, name="proj_a")
    u = _mm_t(x_bf, w_in_t, row0=COL_U, ncols=BRANCH, tn=TN, out_dtype=F32, name="proj_u")
    gates = _mm_t(x_bf, w_in_t, row0=COL_G, ncols=2 * D_MODEL, tn=TN, out_dtype=F32, name="proj_gates",
                  epilogue=_sigmoid)

    wgu = jnp.pad(w_gla_gate_up, ((0, 128 - GLA_RANK), (0, 0)))
    bgu = b_gla_gate_up.reshape(1, GLA_DK)
    wn = w_gla_norm.reshape(1, BRANCH)
    o_p, gla_p_t = _gla_prompt(qkvr, a_low, wgu, bgu, wn)
    o_s, gla_s = _gla_sample(qkvr, a_low, wgu, bgu, wn, state_gla)
    o_all = jnp.concatenate([o_p, o_s], axis=0)
    gla_p = jnp.swapaxes(gla_p_t, 2, 3)

    s5w = _s5_weights(s5_a_re, s5_a_im, s5_b_re, s5_b_im, s5_c_re, s5_c_im, s5_d, s5_log_dt)
    y, s5_re_p, s5_im_p, s5_re_s, s5_im_s = _s5(
        u, s5w, state_s5_re.reshape(N_SAMPLE, -1), state_s5_im.reshape(N_SAMPLE, -1))
    z = _mm(y, w_s5_glu, col0=0, ncols=BRANCH, tn=TN, out_dtype=BF16, name="s5_glu",
            epilogue=lambda acc, yt, bt: yt * _sigmoid(acc + bt),
            extra=(y, b_s5_glu.reshape(1, BRANCH)),
            extra_specs=(pl.BlockSpec((TM, TN), lambda i, j: (i, j)), pl.BlockSpec((1, TN), lambda i, j: (0, j))))

    pre = _merge(o_all, z, w_branch, gates)
    w_r = jnp.concatenate([w_router_group,
                           jnp.moveaxis(w_router_expert, 0, 1).reshape(D_MODEL, N_EXPERTS),
                           jnp.zeros((D_MODEL, 128 - 8 - N_EXPERTS), F32)], axis=1)
    b_r = jnp.concatenate([b_router_group, b_router_expert.reshape(-1),
                           jnp.zeros((128 - 8 - N_EXPERTS,), F32)]).reshape(1, 128)
    mixed = _mm(pre, w_out, col0=0, ncols=D_MODEL, tn=TN, out_dtype=F32, name="out_proj")
    x1, x1p, ids, wts = _ln_router(mixed, x_p, x_s, ln1_g.reshape(1, D_MODEL), ln1_b.reshape(1, D_MODEL), w_r, b_r)

    pos, row_tok, mexp, mstart, mnsub = _route_positions(ids)
    eo = _moe_experts(x1p,
                      w_moe_gate.reshape(N_EXPERTS, D_MODEL, MOE_HIDDEN),
                      w_moe_up.reshape(N_EXPERTS, D_MODEL, MOE_HIDDEN),
                      w_moe_down.reshape(N_EXPERTS, MOE_HIDDEN, D_MODEL),
                      mexp, mstart, mnsub, row_tok)
    y_p, y_s = _combine(pos, eo, wts, x1, ln2_g.reshape(1, D_MODEL), ln2_b.reshape(1, D_MODEL))

    return (y_p.reshape(N_PROMPT_SEQ, SEQ, D_MODEL), y_s.reshape(N_SAMPLE, 1, D_MODEL),
            gla_p,
            s5_re_p.reshape(N_PROMPT_SEQ, S5_GROUPS, S5_STATE), s5_im_p.reshape(N_PROMPT_SEQ, S5_GROUPS, S5_STATE),
            gla_s,
            s5_re_s.reshape(N_SAMPLE, S5_GROUPS, S5_STATE), s5_im_s.reshape(N_SAMPLE, S5_GROUPS, S5_STATE))
```

```python
import functools
import math

import jax
import jax.numpy as jnp
import numpy as np
from jax import lax
from jax.experimental import pallas as pl
from jax.experimental.pallas import tpu as pltpu

F32 = jnp.float32
BF16 = jnp.bfloat16
HIGHEST = lax.Precision.HIGHEST

D_MODEL = 4096
N_PROMPT_SEQ = 4
SEQ = 2048
N_SAMPLE = 128
N_PROMPT = N_PROMPT_SEQ * SEQ
N_TOK = N_PROMPT + N_SAMPLE
BRANCH = D_MODEL // 2
GLA_HEADS = 4
GLA_DK = D_MODEL // 4
GLA_HDK = GLA_DK // GLA_HEADS
GLA_HDV = BRANCH // GLA_HEADS
GLA_RANK = 16
GLA_TAU = 16.0
GLA_CHUNK = 64
GLA_LEVELS = 6
S5_GROUP = 16
S5_GROUPS = BRANCH // S5_GROUP
S5_STATE = 64
S5_CHUNK = 8
S5_LANE_GROUPS = 8
S5_BLOCKS = S5_GROUPS // S5_LANE_GROUPS
N_EXPERTS = 64
MOE_HIDDEN = D_MODEL // 8
MOE_BLOCK = 128
MOE_HSPLIT = 2
LN_EPS = 1e-5
ALPHA = 2.0 ** 0.25
COL_A = 6144
COL_U = 6160
COL_G = 8208
IN_COLS = 16400

VMEM_LIMIT = 56 * 1024 * 1024
TM = 1040
TN = 512


_NT = (((1,), (1,)), ((), ()))
_TN = (((0,), (0,)), ((), ()))


def _cp(sem):
    return pltpu.CompilerParams(dimension_semantics=sem, vmem_limit_bytes=VMEM_LIMIT)


def _sigmoid(x):
    return 1.0 / (1.0 + jnp.exp(-x))


def _log_sigmoid(z):
    return jnp.minimum(z, 0.0) - jnp.log1p(jnp.exp(-jnp.abs(z)))


def _gelu_tanh(x):
    return 0.5 * x * (1.0 + jnp.tanh(math.sqrt(2.0 / math.pi) * (x + 0.044715 * (x * x * x))))


def _layer_norm(x):
    mu = jnp.mean(x, axis=-1, keepdims=True)
    xc = x - mu
    var = jnp.mean(xc * xc, axis=-1, keepdims=True)
    return xc * lax.rsqrt(var + LN_EPS)


def _mm_body(a_ref, w_ref, *rest, epilogue):
    *extra, o_ref = rest
    acc = jnp.dot(a_ref[...].astype(BF16), w_ref[...].astype(BF16), preferred_element_type=F32)
    o_ref[...] = epilogue(acc, *[e[...] for e in extra]).astype(o_ref.dtype)


def _mm(a, w, *, col0, ncols, tn, out_dtype, name, epilogue=lambda acc: acc, extra=(), extra_specs=()):
    m, k = a.shape
    cb0 = col0 // tn
    return pl.pallas_call(
        functools.partial(_mm_body, epilogue=epilogue),
        grid=(m // TM, ncols // tn),
        in_specs=[pl.BlockSpec((TM, k), lambda i, j: (i, 0)),
                  pl.BlockSpec((k, tn), lambda i, j: (0, cb0 + j)),
                  *extra_specs],
        out_specs=pl.BlockSpec((TM, tn), lambda i, j: (i, j)),
        out_shape=jax.ShapeDtypeStruct((m, ncols), out_dtype),
        compiler_params=_cp(("arbitrary", "arbitrary")),
        name=name,
    )(a, w, *extra)


def _mm_t_body(a_ref, wt_ref, o_ref, *, epilogue):
    acc = lax.dot_general(a_ref[...], wt_ref[...].astype(BF16), _NT, preferred_element_type=F32)
    o_ref[...] = epilogue(acc).astype(o_ref.dtype)


def _mm_t(a, wt, *, row0, ncols, tn, out_dtype, name, epilogue=lambda acc: acc):
    m, k = a.shape
    return pl.pallas_call(
        functools.partial(_mm_t_body, epilogue=epilogue),
        grid=(m // TM, ncols // tn),
        in_specs=[pl.BlockSpec((TM, k), lambda i, j: (i, 0)),
                  pl.BlockSpec((pl.Element(tn), pl.Element(k)),
                               lambda i, j: (pl.multiple_of(row0 + j * tn, 8), 0))],
        out_specs=pl.BlockSpec((TM, tn), lambda i, j: (i, j)),
        out_shape=jax.ShapeDtypeStruct((m, ncols), out_dtype),
        compiler_params=_cp(("arbitrary", "arbitrary")),
        name=name,
    )(a, wt)


def _gla_coeff_matrix():
    c = GLA_CHUNK
    t = np.arange(c)[:, None]
    u = np.arange(c)[None, :]
    blocks = [(u <= t), (u > t)]
    for lvl in range(1, GLA_LEVELS + 1):
        m = 1 << lvl
        half = m // 2
        mid = (t // m) * m + half - 1
        lower = (t % m) >= half
        blocks.append(np.where(lower, (u > mid) & (u <= t), (u > t) & (u <= mid)))
    return np.concatenate(blocks, axis=0).astype(np.float32)


def _gla_level_masks():
    c = GLA_CHUNK
    t = lax.broadcasted_iota(jnp.int32, (c, c), 0)
    s = lax.broadcasted_iota(jnp.int32, (c, c), 1)
    masks = []
    for lvl in range(1, GLA_LEVELS + 1):
        m = 1 << lvl
        half = m // 2
        masks.append(((t >> lvl) == (s >> lvl)) & ((t & (m - 1)) >= half) & ((s & (m - 1)) < half))
    return masks


def _gla_out_norm(o, r, wn):
    return _layer_norm(o) * wn * (r * _sigmoid(r))


def _gla_prompt_body(q_ref, k_ref, v_ref, r_ref, a_ref, wgu_ref, bgu_ref, wn_ref, cm_ref,
                     o_ref, st_ref, s_scr, *, n_sub):
    c = GLA_CHUNK

    @pl.when(pl.program_id(2) == 0)
    def _():
        s_scr[...] = jnp.zeros_like(s_scr)

    masks = _gla_level_masks()
    cm = cm_ref[...]
    nt, tn = _NT, _TN

    def chunk(ci, carry):
        for hh in range(GLA_HP):
            head_chunk(ci, hh)
        return carry

    def head_chunk(ci, hh):
        rows = pl.ds(pl.multiple_of(ci * c, c), c)
        kcols = slice(hh * GLA_HDK, (hh + 1) * GLA_HDK)
        vcols = slice(hh * GLA_HDV, (hh + 1) * GLA_HDV)
        q = q_ref[rows, kcols] * (GLA_HDK ** -0.5)
        k = k_ref[rows, kcols]
        v = v_ref[rows, vcols]
        z = jnp.dot(a_ref[rows, :], wgu_ref[:, kcols], preferred_element_type=F32, precision=HIGHEST) + bgu_ref[:, kcols]
        g = _log_sigmoid(z) * (1.0 / GLA_TAU)
        g_hi = g.astype(BF16)
        r1 = g - g_hi.astype(F32)
        g_mid = r1.astype(BF16)
        g_lo = (r1 - g_mid.astype(F32)).astype(BF16)
        g3 = jnp.concatenate([g_hi, g_mid, g_lo], axis=0)
        f = jnp.exp(jnp.dot(cm, g3, preferred_element_type=F32))
        st = s_scr[hh]
        vb = v.astype(BF16)
        o = lax.dot_general((q * f[0:c]).astype(BF16), st.astype(BF16), nt, preferred_element_type=F32)
        scores = jnp.zeros((c, c), F32)
        for lvl in range(GLA_LEVELS):
            fl = f[(2 + lvl) * c:(3 + lvl) * c]
            p = lax.dot_general((q * fl).astype(BF16), (k * fl).astype(BF16), nt, preferred_element_type=F32)
            scores = scores + jnp.where(masks[lvl], p, 0.0)
        diag = jnp.sum(q * k, axis=1, keepdims=True)
        o = o + jnp.dot(scores.astype(BF16), vb, preferred_element_type=F32) + diag * v
        kd = (k * f[c:2 * c]).astype(BF16)
        s_scr[hh] = st * f[c - 1:c, :] + lax.dot_general(vb, kd, tn, preferred_element_type=F32)
        o_ref[rows, vcols] = _gla_out_norm(o, r_ref[rows, vcols], wn_ref[:, vcols]).astype(o_ref.dtype)

    lax.fori_loop(0, n_sub, chunk, 0, unroll=True)

    @pl.when(pl.program_id(2) == pl.num_programs(2) - 1)
    def _():
        st_ref[0] = s_scr[...]


GLA_HP = 4


def _gla_prompt(qkvr, a_low, wgu, bgu, wn):
    n_sub = 4
    tt = GLA_CHUNK * n_sub
    nt_steps = SEQ // tt
    cm = jnp.asarray(np.tile(_gla_coeff_matrix(), (1, 3)), dtype=BF16)
    rows = lambda b, h, c: b * nt_steps + c
    wk, wv = GLA_HP * GLA_HDK, GLA_HP * GLA_HDV
    n_groups = GLA_HEADS // GLA_HP
    o, st = pl.pallas_call(
        functools.partial(_gla_prompt_body, n_sub=n_sub),
        grid=(N_PROMPT_SEQ, n_groups, nt_steps),
        in_specs=[
            pl.BlockSpec((tt, wk), lambda b, h, c: (rows(b, h, c), h)),
            pl.BlockSpec((tt, wk), lambda b, h, c: (rows(b, h, c), n_groups + h)),
            pl.BlockSpec((tt, wv), lambda b, h, c: (rows(b, h, c), n_groups + h)),
            pl.BlockSpec((tt, wv), lambda b, h, c: (rows(b, h, c), 2 * n_groups + h)),
            pl.BlockSpec((tt, 128), lambda b, h, c: (rows(b, h, c), 0)),
            pl.BlockSpec((128, wk), lambda b, h, c: (0, h)),
            pl.BlockSpec((1, wk), lambda b, h, c: (0, h)),
            pl.BlockSpec((1, wv), lambda b, h, c: (0, h)),
            pl.BlockSpec(cm.shape, lambda b, h, c: (0, 0)),
        ],
        out_specs=[
            pl.BlockSpec((tt, wv), lambda b, h, c: (rows(b, h, c), h)),
            pl.BlockSpec((1, GLA_HP, GLA_HDV, GLA_HDK), lambda b, h, c: (b, h, 0, 0)),
        ],
        out_shape=[
            jax.ShapeDtypeStruct((N_PROMPT, BRANCH), BF16),
            jax.ShapeDtypeStruct((N_PROMPT_SEQ, GLA_HEADS, GLA_HDV, GLA_HDK), F32),
        ],
        scratch_shapes=[pltpu.VMEM((GLA_HP, GLA_HDV, GLA_HDK), F32)],
        compiler_params=_cp(("arbitrary", "arbitrary", "arbitrary")),
        name="gla_prompt",
    )(qkvr, qkvr, qkvr, qkvr, a_low, wgu, bgu, wn, cm)
    return o, st


GLA_SB = 16


def _gla_sample_body(q_ref, k_ref, v_ref, r_ref, a_ref, wgu_ref, bgu_ref, wn_ref, s_ref, o_ref, so_ref):
    q = q_ref[...] * (GLA_HDK ** -0.5)
    k = k_ref[...]
    v = v_ref[...]
    z = jnp.dot(a_ref[...], wgu_ref[...], preferred_element_type=F32, precision=HIGHEST) + bgu_ref[...]
    eg = jnp.exp(_log_sigmoid(z) * (1.0 / GLA_TAU))
    qe = (q * eg).astype(BF16)
    eg_t = eg.T
    k_t = k.T
    rows = []
    for n in range(GLA_SB):
        s0 = s_ref[n, 0]
        rows.append(jnp.dot(qe, s0.astype(BF16), preferred_element_type=F32)[n:n + 1])
        so_ref[n, 0] = s0 * eg_t[:, n:n + 1] + k_t[:, n:n + 1] * v[n:n + 1, :]
    o = jnp.concatenate(rows, axis=0) + jnp.sum(q * k, axis=1, keepdims=True) * v
    o_ref[...] = _gla_out_norm(o, r_ref[...], wn_ref[...]).astype(o_ref.dtype)


def _gla_sample(qkvr, a_low, wgu, bgu, wn, state):
    r0 = N_PROMPT // GLA_SB
    o, st = pl.pallas_call(
        _gla_sample_body,
        grid=(GLA_HEADS, N_SAMPLE // GLA_SB),
        in_specs=[
            pl.BlockSpec((GLA_SB, GLA_HDK), lambda h, i: (r0 + i, h)),
            pl.BlockSpec((GLA_SB, GLA_HDK), lambda h, i: (r0 + i, GLA_HEADS + h)),
            pl.BlockSpec((GLA_SB, GLA_HDV), lambda h, i: (r0 + i, GLA_HEADS + h)),
            pl.BlockSpec((GLA_SB, GLA_HDV), lambda h, i: (r0 + i, 2 * GLA_HEADS + h)),
            pl.BlockSpec((GLA_SB, 128), lambda h, i: (r0 + i, 0)),
            pl.BlockSpec((128, GLA_HDK), lambda h, i: (0, h)),
            pl.BlockSpec((1, GLA_HDK), lambda h, i: (0, h)),
            pl.BlockSpec((1, GLA_HDV), lambda h, i: (0, h)),
            pl.BlockSpec((GLA_SB, 1, GLA_HDK, GLA_HDV), lambda h, i: (i, h, 0, 0)),
        ],
        out_specs=[
            pl.BlockSpec((GLA_SB, GLA_HDV), lambda h, i: (i, h)),
            pl.BlockSpec((GLA_SB, 1, GLA_HDK, GLA_HDV), lambda h, i: (i, h, 0, 0)),
        ],
        out_shape=[
            jax.ShapeDtypeStruct((N_SAMPLE, BRANCH), BF16),
            jax.ShapeDtypeStruct(state.shape, F32),
        ],
        compiler_params=_cp(("arbitrary", "arbitrary")),
        name="gla_sample",
    )(qkvr, qkvr, qkvr, qkvr, a_low, wgu, bgu, wn, state)
    return o, st


def _s5_weights(a_re, a_im, b_re, b_im, c_re, c_im, d, log_dt):
    L = S5_CHUNK
    lam_re = jnp.minimum(a_re, -1e-4)
    lam_im = a_im
    dt = jnp.exp(log_dt)[:, None]
    kk = jnp.arange(L + 1, dtype=F32)[:, None, None]
    pow_re = jnp.exp(lam_re * dt * kk) * jnp.cos(lam_im * dt * kk)
    pow_im = jnp.exp(lam_re * dt * kk) * jnp.sin(lam_im * dt * kk)
    lbar_re, lbar_im = pow_re[1], pow_im[1]
    den = lam_re * lam_re + lam_im * lam_im
    f_re = ((lbar_re - 1.0) * lam_re + lbar_im * lam_im) / den
    f_im = (lbar_im * lam_re - (lbar_re - 1.0) * lam_im) / den
    bb_re = f_re[..., None] * b_re - f_im[..., None] * b_im
    bb_im = f_re[..., None] * b_im + f_im[..., None] * b_re
    nb, ng = S5_BLOCKS, S5_LANE_GROUPS
    bb_re_t = jnp.swapaxes(bb_re, 1, 2)
    bb_im_t = jnp.swapaxes(bb_im, 1, 2)
    lb_re = (pow_re[:L, :, None, :] * bb_re_t - pow_im[:L, :, None, :] * bb_im_t).reshape(L, nb, 128, S5_STATE)
    lb_im = (pow_re[:L, :, None, :] * bb_im_t + pow_im[:L, :, None, :] * bb_re_t).reshape(L, nb, 128, S5_STATE)
    cl_re = (c_re[None] * pow_re[:, :, None, :] - c_im[None] * pow_im[:, :, None, :]).reshape(L + 1, nb, 128, S5_STATE)
    cl_im = (c_re[None] * pow_im[:, :, None, :] + c_im[None] * pow_re[:, :, None, :]).reshape(L + 1, nb, 128, S5_STATE)
    lb = jnp.concatenate([lb_re, lb_im], axis=-1)
    cl = jnp.concatenate([cl_re, -cl_im], axis=-1)
    mult = (L * jnp.arange(1, 9, dtype=F32))[:, None, None]
    sc_re = jnp.exp(lam_re * dt * mult) * jnp.cos(lam_im * dt * mult)
    sc_im = jnp.exp(lam_re * dt * mult) * jnp.sin(lam_im * dt * mult)

    def state_lanes(x):
        lead = x.shape[:-2]
        return jnp.moveaxis(x.reshape(lead + (nb, ng * S5_STATE)), -2, 0)

    scan_mult = jnp.concatenate([state_lanes(sc_re), state_lanes(sc_im)], axis=-1)
    lbar1 = jnp.concatenate([state_lanes(lbar_re[None]), state_lanes(lbar_im[None])], axis=-1)
    dvec = jnp.tile(d.reshape(nb, 1, 128), (1, 1, L))
    return lb, cl, scan_mult, lbar1, dvec


def _s5_expand(src):
    ng, p = S5_LANE_GROUPS, S5_STATE
    lane = lax.broadcasted_iota(jnp.int32, src.shape, 1)
    other = pltpu.roll(src, p, 1)
    re2 = jnp.where(lane < p, src, other)
    im2 = jnp.where(lane < p, other, src)
    full = jnp.concatenate([re2] * (ng // 2) + [im2] * (ng // 2), axis=1)
    row = lax.broadcasted_iota(jnp.int32, full.shape, 0)
    col = lax.broadcasted_iota(jnp.int32, full.shape, 1)
    same_group = (row >> 4) == ((col >> 6) & (ng - 1))
    return jnp.where(same_group, full, 0.0)


def _s5_lag_kernel(zpow, cpow0):
    z_hi = zpow.astype(BF16)
    z_lo = (zpow - z_hi.astype(F32)).astype(BF16)
    c_hi = cpow0.astype(BF16)
    c_lo = (cpow0 - c_hi.astype(F32)).astype(BF16)
    dot = functools.partial(lax.dot_general, dimension_numbers=_NT, preferred_element_type=F32)
    return dot(z_hi, c_hi) + dot(z_hi, c_lo) + dot(z_lo, c_hi)


def _s5_build_weights(lb_ref, cl_ref, t_scr, wz_scr, wc_scr):
    L = S5_CHUNK
    cpow0 = _s5_expand(cl_ref[0, 0])
    t_scr[...] = jnp.zeros_like(t_scr)
    for k in range(L):
        zpow = _s5_expand(lb_ref[k, 0])
        wz_scr[(L - 1 - k) * 128:(L - k) * 128, :] = zpow.astype(BF16)
        wc_scr[k * 128:(k + 1) * 128, :] = _s5_expand(cl_ref[k + 1, 0]).astype(BF16)
        bd = _s5_lag_kernel(zpow, cpow0).astype(BF16)
        for sp in range(L - k):
            t_scr[sp * 128:(sp + 1) * 128, (sp + k) * 128:(sp + k + 1) * 128] = bd


def _s5_body(u_ref, lb_ref, cl_ref, sm_ref, l1_ref, d_ref, hre_ref, him_ref,
             y_ref, fre_ref, fim_ref, sre_ref, sim_ref, t_ref, wz_ref, wc_ref):
    step = pl.program_id(1)

    @pl.when(step == 0)
    def _():
        _s5_build_weights(lb_ref, cl_ref, t_ref, wz_ref, wc_ref)

    @pl.when(step < N_PROMPT_SEQ)
    def _():
        _s5_prompt_step(u_ref, sm_ref, d_ref, y_ref, fre_ref, fim_ref, t_ref, wz_ref, wc_ref)

    @pl.when(step == N_PROMPT_SEQ)
    def _():
        _s5_sample_step(u_ref, l1_ref, d_ref, hre_ref, him_ref, y_ref, sre_ref, sim_ref, t_ref, wz_ref, wc_ref)


def _s5_prompt_step(u_ref, sm_ref, d_ref, y_ref, fre_ref, fim_ref, t_ref, wz_ref, wc_ref):
    L = S5_CHUNK
    n_rows = SEQ // L
    ns = S5_LANE_GROUPS * S5_STATE
    v = jnp.concatenate([u_ref[pl.ds(s, n_rows, stride=L), :] for s in range(L)], axis=1)
    vb = v.astype(BF16)
    z = jnp.dot(vb, wz_ref[...], preferred_element_type=F32)
    hr, hi = z[:, :ns], z[:, ns:]
    pos = lax.broadcasted_iota(jnp.int32, (n_rows, ns), 0)
    sm = sm_ref[0]
    for d in range(3):
        sh = 1 << d
        ar, ai = sm[sh - 1:sh, :ns], sm[sh - 1:sh, ns:]
        keep = (pos & 7) >= sh
        pr = jnp.where(keep, pltpu.roll(hr, sh, 0), 0.0)
        pi = jnp.where(keep, pltpu.roll(hi, sh, 0), 0.0)
        hr, hi = hr + ar * pr - ai * pi, hi + ar * pi + ai * pr
    gr, gi = sm[:, :ns], sm[:, ns:]
    out_r, out_i = [hr[0:8]], [hi[0:8]]
    for grp in range(1, n_rows // 8):
        cr = jnp.broadcast_to(out_r[-1][7:8], (8, ns))
        ci = jnp.broadcast_to(out_i[-1][7:8], (8, ns))
        out_r.append(hr[grp * 8:(grp + 1) * 8] + gr * cr - gi * ci)
        out_i.append(hi[grp * 8:(grp + 1) * 8] + gr * ci + gi * cr)
    hr = jnp.concatenate(out_r, axis=0)
    hi = jnp.concatenate(out_i, axis=0)
    fre_ref[0] = hr[n_rows - 1:n_rows]
    fim_ref[0] = hi[n_rows - 1:n_rows]
    first = pos >= 1
    h_prev = jnp.concatenate([jnp.where(first, pltpu.roll(hr, 1, 0), 0.0),
                              jnp.where(first, pltpu.roll(hi, 1, 0), 0.0)], axis=1)
    y = (jnp.dot(vb, t_ref[...], preferred_element_type=F32)
         + lax.dot_general(h_prev.astype(BF16), wc_ref[...], _NT, preferred_element_type=F32)
         + d_ref[0] * v)
    y = _gelu_tanh(y)
    for s in range(L):
        y_ref[pl.ds(s, n_rows, stride=L), :] = y[:, s * 128:(s + 1) * 128]


def _s5_sample_step(u_ref, l1_ref, d_ref, hre_ref, him_ref, y_ref, sre_ref, sim_ref, t_ref, wz_ref, wc_ref):
    L = S5_CHUNK
    ns = S5_LANE_GROUPS * S5_STATE
    bbar = wz_ref[(L - 1) * 128:L * 128, :]
    k0 = t_ref[0:128, 0:128]
    wc0 = wc_ref[0:128, :]
    us = u_ref[0:N_SAMPLE, :]
    usb = us.astype(BF16)
    h0r, h0i = hre_ref[...], him_ref[...]
    l1 = l1_ref[0]
    bu = jnp.dot(usb, bbar, preferred_element_type=F32)
    sre_ref[...] = l1[:, :ns] * h0r - l1[:, ns:] * h0i + bu[:, :ns]
    sim_ref[...] = l1[:, :ns] * h0i + l1[:, ns:] * h0r + bu[:, ns:]
    h0 = jnp.concatenate([h0r, h0i], axis=1).astype(BF16)
    ys = (jnp.dot(usb, k0, preferred_element_type=F32)
          + lax.dot_general(h0, wc0, _NT, preferred_element_type=F32)
          + d_ref[0, :, 0:128] * us)
    y_ref[0:N_SAMPLE, :] = _gelu_tanh(ys)


def _s5(u, weights, st_re, st_im):
    lb, cl, scan_mult, lbar1, dvec = weights
    L = S5_CHUNK
    ns = S5_LANE_GROUPS * S5_STATE
    last_seq = N_PROMPT_SEQ - 1
    wblk = lambda a: pl.BlockSpec((1,) + a.shape[1:], lambda j, b: (j, 0, 0))
    pblk = lambda a: pl.BlockSpec((a.shape[0], 1, 128, 128), lambda j, b: (0, j, 0, 0))
    sblk = pl.BlockSpec((N_SAMPLE, ns), lambda j, b: (0, j))
    fblk = pl.BlockSpec((1, 1, ns), lambda j, b: (jnp.minimum(b, last_seq), 0, j))
    mat = pltpu.VMEM((L * 128, L * 128), BF16)
    return pl.pallas_call(
        _s5_body,
        grid=(S5_BLOCKS, N_PROMPT_SEQ + 1),
        in_specs=[
            pl.BlockSpec((SEQ, 128), lambda j, b: (b, j)),
            pblk(lb), pblk(cl), wblk(scan_mult), wblk(lbar1), wblk(dvec), sblk, sblk,
        ],
        out_specs=[pl.BlockSpec((SEQ, 128), lambda j, b: (b, j)), fblk, fblk, sblk, sblk],
        out_shape=[
            jax.ShapeDtypeStruct((N_TOK, BRANCH), F32),
            jax.ShapeDtypeStruct((N_PROMPT_SEQ, 1, S5_GROUPS * S5_STATE), F32),
            jax.ShapeDtypeStruct((N_PROMPT_SEQ, 1, S5_GROUPS * S5_STATE), F32),
            jax.ShapeDtypeStruct((N_SAMPLE, S5_GROUPS * S5_STATE), F32),
            jax.ShapeDtypeStruct((N_SAMPLE, S5_GROUPS * S5_STATE), F32),
        ],
        scratch_shapes=[mat, mat, mat],
        compiler_params=_cp(("arbitrary", "arbitrary")),
        name="s5",
    )(u, lb, cl, scan_mult, lbar1, dvec, st_re, st_im)


def _merge_body(o_ref, z_ref, w0_ref, w1_ref, g0_ref, g1_ref, out_ref):
    p0 = jnp.dot(o_ref[...], w0_ref[0].astype(BF16), preferred_element_type=F32)
    p1 = jnp.dot(z_ref[...], w1_ref[0].astype(BF16), preferred_element_type=F32)
    out_ref[...] = (g0_ref[...] * p0 + g1_ref[...] * p1).astype(out_ref.dtype)


def _merge(o, z, w_branch, gates):
    ncb = D_MODEL // TN
    return pl.pallas_call(
        _merge_body,
        grid=(N_TOK // TM, ncb),
        in_specs=[
            pl.BlockSpec((TM, BRANCH), lambda i, j: (i, 0)),
            pl.BlockSpec((TM, BRANCH), lambda i, j: (i, 0)),
            pl.BlockSpec((1, BRANCH, TN), lambda i, j: (0, 0, j)),
            pl.BlockSpec((1, BRANCH, TN), lambda i, j: (1, 0, j)),
            pl.BlockSpec((TM, TN), lambda i, j: (i, j)),
            pl.BlockSpec((TM, TN), lambda i, j: (i, ncb + j)),
        ],
        out_specs=pl.BlockSpec((TM, TN), lambda i, j: (i, j)),
        out_shape=jax.ShapeDtypeStruct((N_TOK, D_MODEL), BF16),
        compiler_params=_cp(("arbitrary", "arbitrary")),
        name="merge",
    )(o, z, w_branch, w_branch, gates, gates)


LN_TM = 128
LN_PROMPT_TILES = N_PROMPT // LN_TM
HALF = D_MODEL // 2


def _pack_bf16_pair(x):
    bits = pltpu.bitcast(x.astype(BF16).astype(F32), jnp.uint32)
    return (bits[:, :HALF] & jnp.uint32(0xFFFF0000)) | (bits[:, HALF:] >> 16)


SLAB = HALF // 128


def _slab_store(ref, row0, x):
    for c in range(SLAB):
        ref[pl.ds(row0 * SLAB + c, x.shape[0], stride=SLAB), :] = x[:, c * 128:(c + 1) * 128]


def _slab_load(ref, row0, n):
    return jnp.concatenate([ref[pl.ds(row0 * SLAB + c, n, stride=SLAB), :] for c in range(SLAB)], axis=1)


def _slab_rows(first, n):
    return pl.ds(pl.multiple_of(first * SLAB, SLAB), n * SLAB)


def _unpack_bf16_pair(w):
    hi = pltpu.bitcast(w & jnp.uint32(0xFFFF0000), F32)
    lo = pltpu.bitcast(w << 16, F32)
    return hi, lo


def _stack_bf16_body(xp_ref, xs_ref, o_ref):
    o_ref[...] = jnp.where(pl.program_id(0) < LN_PROMPT_TILES, xp_ref[...], xs_ref[...]).astype(BF16)


def _stack_bf16(x_p, x_s):
    return pl.pallas_call(
        _stack_bf16_body,
        grid=(N_TOK // LN_TM,),
        in_specs=[
            pl.BlockSpec((LN_TM, D_MODEL), lambda i: (jnp.minimum(i, LN_PROMPT_TILES - 1), 0)),
            pl.BlockSpec((LN_TM, D_MODEL), lambda i: (jnp.maximum(i - LN_PROMPT_TILES, 0), 0)),
        ],
        out_specs=pl.BlockSpec((LN_TM, D_MODEL), lambda i: (i, 0)),
        out_shape=jax.ShapeDtypeStruct((N_TOK, D_MODEL), BF16),
        compiler_params=_cp(("arbitrary",)),
        name="stack_bf16",
    )(x_p, x_s)


def _ln_router_body(m_ref, xp_ref, xs_ref, g_ref, b_ref, wr_ref, br_ref, o_ref, op_ref, ids_ref, wts_ref):
    x = jnp.where(pl.program_id(0) < LN_PROMPT_TILES, xp_ref[...], xs_ref[...])
    x1 = _layer_norm(ALPHA * x + m_ref[...]) * g_ref[...] + b_ref[...]
    o_ref[...] = x1
    _slab_store(op_ref, 0, _pack_bf16_pair(x1))
    x_hi = x1.astype(BF16)
    x_mid = (x1 - x_hi.astype(F32)).astype(BF16)
    wr = wr_ref[...]
    both = jnp.dot(x_hi, wr, preferred_element_type=F32)
    logits = (both[:, :128] + both[:, 128:]
              + jnp.dot(x_mid, wr[:, :128], preferred_element_type=F32) + br_ref[...])
    _route(logits, ids_ref, wts_ref)


def _ln_router(mixed, x_p, x_s, g, b, w_r, b_r):
    w_hi = w_r.astype(BF16)
    w_mid = (w_r - w_hi.astype(F32)).astype(BF16)
    wr = jnp.concatenate([w_hi, w_mid], axis=1)
    row_blk = pl.BlockSpec((LN_TM, 128), lambda i: (i, 0))
    return pl.pallas_call(
        _ln_router_body,
        grid=(N_TOK // LN_TM,),
        in_specs=[
            pl.BlockSpec((LN_TM, D_MODEL), lambda i: (i, 0)),
            pl.BlockSpec((LN_TM, D_MODEL), lambda i: (jnp.minimum(i, LN_PROMPT_TILES - 1), 0)),
            pl.BlockSpec((LN_TM, D_MODEL), lambda i: (jnp.maximum(i - LN_PROMPT_TILES, 0), 0)),
            pl.BlockSpec((1, D_MODEL), lambda i: (0, 0)),
            pl.BlockSpec((1, D_MODEL), lambda i: (0, 0)),
            pl.BlockSpec((D_MODEL, 256), lambda i: (0, 0)),
            pl.BlockSpec((1, 128), lambda i: (0, 0)),
        ],
        out_specs=[pl.BlockSpec((LN_TM, D_MODEL), lambda i: (i, 0)),
                   pl.BlockSpec((LN_TM * SLAB, 128), lambda i: (i, 0)), row_blk, row_blk],
        out_shape=[jax.ShapeDtypeStruct((N_TOK, D_MODEL), F32),
                   jax.ShapeDtypeStruct((N_TOK * SLAB, 128), jnp.uint32),
                   jax.ShapeDtypeStruct((N_TOK, 128), jnp.int32),
                   jax.ShapeDtypeStruct((N_TOK, 128), F32)],
        compiler_params=_cp(("arbitrary",)),
        name="ln1_router",
    )(mixed, x_p, x_s, g, b, wr, b_r)


def _route(logits, ids_ref, wts_ref):
    lane = lax.broadcasted_iota(jnp.int32, logits.shape, 1)
    neg = -jnp.inf
    big = 1 << 20
    gl = jnp.where(lane < 8, logits, neg)
    gmax = jnp.max(gl, axis=1, keepdims=True)
    gidx = jnp.min(jnp.where(gl == gmax, lane, big), axis=1, keepdims=True)
    gprob = 1.0 / jnp.sum(jnp.exp(gl - gmax), axis=1, keepdims=True)
    in_group = (lane >= 8) & (lane < 8 + N_EXPERTS) & (((lane - 8) >> 3) == gidx)
    el = jnp.where(in_group, logits, neg)
    v1 = jnp.max(el, axis=1, keepdims=True)
    i1 = jnp.min(jnp.where(el == v1, lane, big), axis=1, keepdims=True)
    el2 = jnp.where(lane == i1, neg, el)
    v2 = jnp.max(el2, axis=1, keepdims=True)
    i2 = jnp.min(jnp.where(el2 == v2, lane, big), axis=1, keepdims=True)
    e2 = jnp.exp(v2 - v1)
    w1 = gprob / (1.0 + e2)
    w2 = gprob * e2 / (1.0 + e2)
    ids_ref[...] = jnp.where(lane == 0, i1 - 8, jnp.where(lane == 1, i2 - 8, 0))
    wts_ref[...] = jnp.where(lane == 0, w1, jnp.where(lane == 1, w2, 0.0))


N_ASSIGN = 2 * N_TOK
MOE_NBLOCKS = (N_ASSIGN + N_EXPERTS * (MOE_BLOCK - 1) + MOE_BLOCK - 1) // MOE_BLOCK
MOE_ROWS = MOE_NBLOCKS * MOE_BLOCK
MOE_HC = MOE_HIDDEN // MOE_HSPLIT


def _row_copy(src_hbm, row, dst, dst_row, sem):
    return pltpu.make_async_copy(src_hbm.at[_slab_rows(row, 1), :], dst.at[_slab_rows(dst_row, 1), :], sem)


MOE_MACRO = 4
MOE_NMACRO = N_ASSIGN // (MOE_MACRO * MOE_BLOCK) + N_EXPERTS + 1
MOE_DCOLS = 1024


def _block_copy(src, t, o_hbm, blk, sem):
    return pltpu.make_async_copy(src.at[_slab_rows(t * MOE_BLOCK, MOE_BLOCK), :],
                                 o_hbm.at[_slab_rows(blk * MOE_BLOCK, MOE_BLOCK), :], sem)


def _moe_body(mexp_ref, mstart_ref, mnsub_ref, tok_ref, x_hbm, wg_ref, wu_hbm, wd_hbm, o_hbm,
              stage, xb, acc, ostage, wubuf, wdbuf, gsem, osem, wsem):
    m = pl.program_id(0)
    h = pl.program_id(1)
    nsub = mnsub_ref[m]
    start = mstart_ref[m]
    last_m = pl.num_programs(0) - 1
    last_h = pl.num_programs(1) - 1

    def wait_blocks(n):
        def done(t, c):
            _block_copy(ostage, t, o_hbm, 0, osem).wait()
            return c

        lax.fori_loop(0, n, done, 0)

    half_rows = D_MODEL // 2

    def w_copies(step, hh):
        e = mexp_ref[step]
        cols = pl.ds(pl.multiple_of(hh * MOE_HC, MOE_HC), MOE_HC)
        return (
            (pltpu.make_async_copy(wu_hbm.at[e, pl.ds(0, half_rows), cols],
                                   wubuf.at[hh, pl.ds(0, half_rows), :], wsem.at[0, hh]), 0),
            (pltpu.make_async_copy(wu_hbm.at[e, pl.ds(half_rows, half_rows), cols],
                                   wubuf.at[hh, pl.ds(half_rows, half_rows), :], wsem.at[1, hh]), 1),
            (pltpu.make_async_copy(wd_hbm.at[e, cols, :], wdbuf.at[hh], wsem.at[2, hh]), 1),
        )

    def w_start(step, hh):
        for cp, prio in w_copies(step, hh):
            cp.start(priority=prio)

    def w_wait(step, hh):
        for cp, _ in w_copies(step, hh):
            cp.wait()

    @pl.when((m == 0) & (h == 0) & (nsub > 0))
    def _():
        w_start(0, 0)

    @pl.when((h == 0) & (nsub > 0))
    def _():
        w_start(m, 1)

    nxt = jnp.minimum(m + 1, last_m)

    @pl.when((h == last_h) & (m < last_m) & (mnsub_ref[nxt] > 0))
    def _():
        w_start(nxt, 0)

    def gather_start(step):
        def sub(t, c):
            base = (mstart_ref[step] + t) * MOE_BLOCK
            off = pl.multiple_of(t * MOE_BLOCK, MOE_BLOCK)
            for r in range(MOE_BLOCK):
                _row_copy(x_hbm, tok_ref[base + r], stage, off + r, gsem).start()
            return c

        lax.fori_loop(0, mnsub_ref[step], sub, 0)

    @pl.when((h == 0) & (m == 0))
    def _():
        gather_start(0)

    @pl.when((h == 0) & (nsub > 0))
    def _():
        def landed(t, c):
            pltpu.make_async_copy(x_hbm.at[_slab_rows(0, MOE_BLOCK), :],
                                  stage.at[_slab_rows(t * MOE_BLOCK, MOE_BLOCK), :], gsem).wait()
            return c

        lax.fori_loop(0, nsub, landed, 0)

        def sub(t, c):
            rows = pl.ds(pl.multiple_of(t * MOE_BLOCK, MOE_BLOCK), MOE_BLOCK)
            hi, lo = _unpack_bf16_pair(_slab_load(stage, t * MOE_BLOCK, MOE_BLOCK))
            xb[rows, 0:HALF] = hi.astype(BF16)
            xb[rows, HALF:D_MODEL] = lo.astype(BF16)
            return c

        lax.fori_loop(0, nsub, sub, 0)

    @pl.when((h == last_h) & (m < last_m))
    def _():
        gather_start(jnp.minimum(m + 1, last_m))

    for ns in range(1, MOE_MACRO + 1):
        @pl.when(nsub == ns)
        def _(ns=ns):
            rows = ns * MOE_BLOCK

            @pl.when(h == 0)
            def _():
                acc[0:rows, :] = jnp.zeros((rows, D_MODEL), F32)

            x = xb[0:rows, :]
            hg = jnp.dot(x, wg_ref[0].astype(BF16), preferred_element_type=F32)
            w_wait(m, h)
            hu = jnp.dot(x, wubuf[h].astype(BF16), preferred_element_type=F32)
            hh = (hg * _sigmoid(hg) * hu).astype(BF16)
            wd = wdbuf[h].astype(BF16)
            for cc in range(D_MODEL // MOE_DCOLS):
                cols = slice(cc * MOE_DCOLS, (cc + 1) * MOE_DCOLS)
                acc[0:rows, cols] += jnp.dot(hh, wd[:, cols], preferred_element_type=F32)

    @pl.when((h == last_h) & (m > 0))
    def _():
        wait_blocks(mnsub_ref[jnp.maximum(m - 1, 0)])

    @pl.when((h == last_h) & (nsub > 0))
    def _():
        def put(t, c):
            rows = pl.ds(pl.multiple_of(t * MOE_BLOCK, MOE_BLOCK), MOE_BLOCK)
            _slab_store(ostage, t * MOE_BLOCK, _pack_bf16_pair(acc[rows, :]))
            _block_copy(ostage, t, o_hbm, start + t, osem).start()
            return c

        lax.fori_loop(0, nsub, put, 0)

        @pl.when(m == last_m)
        def _():
            wait_blocks(nsub)

    @pl.when((h == last_h) & (nsub == 0))
    def _():
        ostage[0:MOE_BLOCK * SLAB, :] = jnp.zeros((MOE_BLOCK * SLAB, 128), jnp.uint32)
        for t in range(MOE_MACRO):
            @pl.when(start + t < MOE_NBLOCKS)
            def _(t=t):
                cp = _block_copy(ostage, 0, o_hbm, start + t, osem)
                cp.start()
                cp.wait()


def _moe_experts(x1p, wg, wu, wd, mexp, mstart, mnsub, row_tok):
    grid_spec = pltpu.PrefetchScalarGridSpec(
        num_scalar_prefetch=4,
        grid=(MOE_NMACRO, MOE_HSPLIT),
        in_specs=[
            pl.BlockSpec(memory_space=pl.ANY),
            pl.BlockSpec((1, D_MODEL, MOE_HC), lambda m, h, me, ms, mn, rt: (me[m], 0, h)),
            pl.BlockSpec(memory_space=pl.ANY),
            pl.BlockSpec(memory_space=pl.ANY),
        ],
        out_specs=pl.BlockSpec(memory_space=pl.ANY),
        scratch_shapes=[pltpu.VMEM((MOE_MACRO * MOE_BLOCK * SLAB, 128), jnp.uint32),
                        pltpu.VMEM((MOE_MACRO * MOE_BLOCK, D_MODEL), BF16),
                        pltpu.VMEM((MOE_MACRO * MOE_BLOCK, D_MODEL), F32),
                        pltpu.VMEM((MOE_MACRO * MOE_BLOCK * SLAB, 128), jnp.uint32),
                        pltpu.VMEM((MOE_HSPLIT, D_MODEL, MOE_HC), F32),
                        pltpu.VMEM((MOE_HSPLIT, MOE_HC, D_MODEL), F32),
                        pltpu.SemaphoreType.DMA(()), pltpu.SemaphoreType.DMA(()),
                        pltpu.SemaphoreType.DMA((3, MOE_HSPLIT))],
    )
    return pl.pallas_call(
        _moe_body,
        grid_spec=grid_spec,
        out_shape=jax.ShapeDtypeStruct((MOE_ROWS * SLAB, 128), jnp.uint32),
        compiler_params=_cp(("arbitrary", "arbitrary")),
        name="moe_experts",
    )(mexp, mstart, mnsub, row_tok, x1p, wg, wu, wd)


CMB_TM = 128
CMB_PROMPT_TILES = N_PROMPT // CMB_TM


def _combine_body(pos_ref, eo_hbm, wts_ref, x1_ref, g_ref, b_ref, yp_ref, ys_ref, buf, sem):
    i = pl.program_id(0)
    slot = i % 2

    def fetch(tile, s):
        base = 2 * tile * CMB_TM
        for r in range(CMB_TM):
            _row_copy(eo_hbm, pos_ref[base + 2 * r], buf.at[s, 0], r, sem.at[s]).start()
            _row_copy(eo_hbm, pos_ref[base + 2 * r + 1], buf.at[s, 1], r, sem.at[s]).start()

    @pl.when(i == 0)
    def _():
        fetch(0, 0)

    for s in range(2):
        @pl.when((i + 1 < pl.num_programs(0)) & (slot == s))
        def _(s=s):
            fetch(i + 1, 1 - s)

    for k in range(2):
        pltpu.make_async_copy(eo_hbm.at[_slab_rows(0, CMB_TM), :], buf.at[slot, k], sem.at[slot]).wait()
    w = wts_ref[...]
    hi0, lo0 = _unpack_bf16_pair(_slab_load(buf.at[slot, 0], 0, CMB_TM))
    hi1, lo1 = _unpack_bf16_pair(_slab_load(buf.at[slot, 1], 0, CMB_TM))
    w0, w1 = w[:, 0:1], w[:, 1:2]
    y = jnp.concatenate([w0 * hi0 + w1 * hi1, w0 * lo0 + w1 * lo1], axis=1)
    x2 = _layer_norm(ALPHA * x1_ref[...] + y) * g_ref[...] + b_ref[...]

    @pl.when(i < CMB_PROMPT_TILES)
    def _():
        yp_ref[...] = x2

    @pl.when(i >= CMB_PROMPT_TILES)
    def _():
        ys_ref[...] = x2


def _combine(pos, eo, wts, x1, g, b):
    grid_spec = pltpu.PrefetchScalarGridSpec(
        num_scalar_prefetch=1,
        grid=(N_TOK // CMB_TM,),
        in_specs=[
            pl.BlockSpec(memory_space=pl.ANY),
            pl.BlockSpec((CMB_TM, 128), lambda i, p: (i, 0)),
            pl.BlockSpec((CMB_TM, D_MODEL), lambda i, p: (i, 0)),
            pl.BlockSpec((1, D_MODEL), lambda i, p: (0, 0)),
            pl.BlockSpec((1, D_MODEL), lambda i, p: (0, 0)),
        ],
        out_specs=[
            pl.BlockSpec((CMB_TM, D_MODEL), lambda i, p: (jnp.minimum(i, CMB_PROMPT_TILES - 1), 0)),
            pl.BlockSpec((CMB_TM, D_MODEL), lambda i, p: (0, 0)),
        ],
        scratch_shapes=[pltpu.VMEM((2, 2, CMB_TM * SLAB, 128), jnp.uint32), pltpu.SemaphoreType.DMA((2,))],
    )
    return pl.pallas_call(
        _combine_body,
        grid_spec=grid_spec,
        out_shape=[jax.ShapeDtypeStruct((N_PROMPT, D_MODEL), F32),
                   jax.ShapeDtypeStruct((N_SAMPLE, D_MODEL), F32)],
        compiler_params=_cp(("arbitrary",)),
        name="combine_ln2",
    )(pos, eo, wts, x1, g, b)


def _route_positions(ids):
    eid = ids[:, :2].reshape(-1)
    onehot = (eid[:, None] == jnp.arange(N_EXPERTS, dtype=jnp.int32)[None, :]).astype(F32)
    blocks = onehot.reshape(N_ASSIGN // 128, 128, N_EXPERTS)
    local = jnp.einsum('ab,nbe->nae', jnp.tril(jnp.ones((128, 128), F32)), blocks)
    totals = local[:, -1, :]
    csum = (local + (jnp.cumsum(totals, axis=0) - totals)[:, None, :]).reshape(N_ASSIGN, N_EXPERTS).astype(jnp.int32)
    rank = jnp.take_along_axis(csum, eid[:, None], axis=1)[:, 0] - 1
    counts = csum[-1]
    nblk_e = (counts + MOE_BLOCK - 1) // MOE_BLOCK
    bend = jnp.cumsum(nblk_e)
    bstart = bend - nblk_e
    pos = bstart[eid] * MOE_BLOCK + rank
    row_tok = jnp.zeros((MOE_ROWS,), jnp.int32).at[pos].set(jnp.arange(N_ASSIGN, dtype=jnp.int32) // 2)
    nstep_e = (nblk_e + MOE_MACRO - 1) // MOE_MACRO
    send = jnp.cumsum(nstep_e)
    n_steps, n_blocks = send[-1], bend[-1]
    m = jnp.arange(MOE_NMACRO, dtype=jnp.int32)
    e_of_m = jnp.minimum(jnp.searchsorted(send, m, side='right'), N_EXPERTS - 1).astype(jnp.int32)
    local = m - (send - nstep_e)[e_of_m]
    valid = m < n_steps
    mexp = jnp.where(valid, e_of_m, e_of_m[jnp.maximum(n_steps - 1, 0)])
    mstart = jnp.where(valid, bstart[e_of_m] + MOE_MACRO * local, n_blocks + MOE_MACRO * (m - n_steps))
    mnsub = jnp.where(valid, jnp.clip(nblk_e[e_of_m] - MOE_MACRO * local, 0, MOE_MACRO), 0)
    return (pos.astype(jnp.int32), row_tok, mexp.astype(jnp.int32), mstart.astype(jnp.int32),
            mnsub.astype(jnp.int32))


def kernel(x_prompt, x_sample, state_gla, state_s5_re, state_s5_im, w_in, w_gla_gate_up, b_gla_gate_up, w_gla_norm, s5_a_re, s5_a_im, s5_b_re, s5_b_im, s5_c_re, s5_c_im, s5_d, s5_log_dt, w_s5_glu, b_s5_glu, w_branch, w_out, ln1_g, ln1_b, w_router_group, b_router_group, w_router_expert, b_router_expert, w_moe_gate, w_moe_up, w_moe_down, ln2_g, ln2_b):
    x_p = x_prompt.reshape(N_PROMPT, D_MODEL)
    x_s = x_sample.reshape(N_SAMPLE, D_MODEL)
    x_bf = _stack_bf16(x_p, x_s)

    w_in_t = w_in.T
    qkvr = _mm_t(x_bf, w_in_t, row0=0, ncols=COL_A, tn=TN, out_dtype=F32, name="proj_qkvr")
    a_low = _mm_t(x_bf, w_in_t, row0=COL_A, ncols=128, tn=128, out_dtype=F32, name="proj_a")
    u = _mm_t(x_bf, w_in_t, row0=COL_U, ncols=BRANCH, tn=TN, out_dtype=F32, name="proj_u")
    gates = _mm_t(x_bf, w_in_t, row0=COL_G, ncols=2 * D_MODEL, tn=TN, out_dtype=F32, name="proj_gates",
                  epilogue=_sigmoid)

    wgu = jnp.pad(w_gla_gate_up, ((0, 128 - GLA_RANK), (0, 0)))
    bgu = b_gla_gate_up.reshape(1, GLA_DK)
    wn = w_gla_norm.reshape(1, BRANCH)
    o_p, gla_p_t = _gla_prompt(qkvr, a_low, wgu, bgu, wn)
    o_s, gla_s = _gla_sample(qkvr, a_low, wgu, bgu, wn, state_gla)
    o_all = jnp.concatenate([o_p, o_s], axis=0)
    gla_p = jnp.swapaxes(gla_p_t, 2, 3)

    s5w = _s5_weights(s5_a_re, s5_a_im, s5_b_re, s5_b_im, s5_c_re, s5_c_im, s5_d, s5_log_dt)
    y, s5_re_p, s5_im_p, s5_re_s, s5_im_s = _s5(
        u, s5w, state_s5_re.reshape(N_SAMPLE, -1), state_s5_im.reshape(N_SAMPLE, -1))
    z = _mm(y, w_s5_glu, col0=0, ncols=BRANCH, tn=TN, out_dtype=BF16, name="s5_glu",
            epilogue=lambda acc, yt, bt: yt * _sigmoid(acc + bt),
            extra=(y, b_s5_glu.reshape(1, BRANCH)),
            extra_specs=(pl.BlockSpec((TM, TN), lambda i, j: (i, j)), pl.BlockSpec((1, TN), lambda i, j: (0, j))))

    pre = _merge(o_all, z, w_branch, gates)
    w_r = jnp.concatenate([w_router_group,
                           jnp.moveaxis(w_router_expert, 0, 1).reshape(D_MODEL, N_EXPERTS),
                           jnp.zeros((D_MODEL, 128 - 8 - N_EXPERTS), F32)], axis=1)
    b_r = jnp.concatenate([b_router_group, b_router_expert.reshape(-1),
                           jnp.zeros((128 - 8 - N_EXPERTS,), F32)]).reshape(1, 128)
    mixed = _mm(pre, w_out, col0=0, ncols=D_MODEL, tn=TN, out_dtype=F32, name="out_proj")
    x1, x1p, ids, wts = _ln_router(mixed, x_p, x_s, ln1_g.reshape(1, D_MODEL), ln1_b.reshape(1, D_MODEL), w_r, b_r)

    pos, row_tok, mexp, mstart, mnsub = _route_positions(ids)
    eo = _moe_experts(x1p,
                      w_moe_gate.reshape(N_EXPERTS, D_MODEL, MOE_HIDDEN),
                      w_moe_up.reshape(N_EXPERTS, D_MODEL, MOE_HIDDEN),
                      w_moe_down.reshape(N_EXPERTS, MOE_HIDDEN, D_MODEL),
                      mexp, mstart, mnsub, row_tok)
    y_p, y_s = _combine(pos, eo, wts, x1, ln2_g.reshape(1, D_MODEL), ln2_b.reshape(1, D_MODEL))

    return (y_p.reshape(N_PROMPT_SEQ, SEQ, D_MODEL), y_s.reshape(N_SAMPLE, 1, D_MODEL),
            gla_p,
            s5_re_p.reshape(N_PROMPT_SEQ, S5_GROUPS, S5_STATE), s5_im_p.reshape(N_PROMPT_SEQ, S5_GROUPS, S5_STATE),
            gla_s,
            s5_re_s.reshape(N_SAMPLE, S5_GROUPS, S5_STATE), s5_im_s.reshape(N_SAMPLE, S5_GROUPS, S5_STATE))
```
